```python
import jax
import jax.numpy as jnp
from jax import lax
import numpy as np

D_MODEL = 2048
BATCH = 1
SEQ = 8192
DEPTH = 1
DEC_BATCH = 16
DEC_SEQ = 32
PAST_LEN = 4096

CHUNK = 64
N_META = 16
NORM_EPS = 1e-6
RWKV_HEADS = 16
RWKV_HEAD_DIM = 64
RWKV_DIM = RWKV_HEADS * RWKV_HEAD_DIM
DECAY_LORA = 64
ICLR_LORA = 64
GATE_LORA = 160
RWKV_PROJ = 3 * RWKV_DIM + DECAY_LORA + ICLR_LORA + GATE_LORA
RWKV_LN_EPS = 64e-5
ATT_HEADS = 16
ATT_KV_HEADS = 2
ATT_HEAD_DIM = 64
ATT_GROUP = ATT_HEADS // ATT_KV_HEADS
ATT_Q_DIM = ATT_HEADS * ATT_HEAD_DIM
ATT_KV_DIM = ATT_KV_HEADS * ATT_HEAD_DIM
ATT_PROJ = ATT_Q_DIM + 2 * ATT_KV_DIM
WINDOW = 128
WINDOW_CHUNKS = WINDOW // CHUNK
ROPE_THETA = 10000.0
N_BRANCHES = 2
IN_PROJ = RWKV_PROJ + ATT_PROJ + N_BRANCHES * D_MODEL
D_FF = 3 * D_MODEL
FFN_CONV = 3

kernel_name = 'hybrid_rwkv7_swa_sink_stream_step'


def rmsnorm(x, g):
    xf = x.astype(jnp.float32)
    y = xf * lax.rsqrt(jnp.mean(xf * xf, axis=-1, keepdims=True) + NORM_EPS)
    return (y * g.astype(jnp.float32)).astype(x.dtype)


def rope(x, pos):
    half = x.shape[-1] // 2
    inv_freq = ROPE_THETA ** (-jnp.arange(half, dtype=jnp.float32) / half)
    ang = pos.astype(jnp.float32)[:, None] * inv_freq[None, :]
    cos = jnp.cos(ang)[None, :, None, :]
    sin = jnp.sin(ang)[None, :, None, :]
    xf = x.astype(jnp.float32)
    x1, x2 = xf[..., :half], xf[..., half:]
    return jnp.concatenate([x1 * cos - x2 * sin, x2 * cos + x1 * sin], axis=-1).astype(x.dtype)


def sink_softmax(scores, sink):
    s = jnp.concatenate([jnp.broadcast_to(sink, scores.shape[:-1] + (1,)), scores], axis=-1)
    return jax.nn.softmax(s, axis=-1)[..., 1:]


def rwkv_time_mix(pr, shift_prev, s0, lw):
    B, T, _ = pr.shape
    f32 = jnp.float32
    prev = jnp.concatenate([shift_prev.astype(pr.dtype), pr[:, :-1]], axis=1)
    xs = pr + (prev - pr) * lw['rwkv_mu']
    c1, c2, c3 = RWKV_DIM, 2 * RWKV_DIM, 3 * RWKV_DIM
    c4, c5 = c3 + DECAY_LORA, c3 + DECAY_LORA + ICLR_LORA
    r, k, v = xs[..., :c1], xs[..., c1:c2], xs[..., c2:c3]
    xw, xa, xg = xs[..., c3:c4], xs[..., c4:c5], xs[..., c5:]
    w_pre = (lw['rwkv_w0'] + jnp.tanh(xw) @ lw['rwkv_w_lora_b']).astype(f32)
    decay = jnp.exp(-jnp.exp(-jax.nn.softplus(-w_pre) - 0.5))
    a = jax.nn.sigmoid(lw['rwkv_a0'] + xa @ lw['rwkv_a_lora_b']).astype(f32)
    g = jax.nn.sigmoid(xg) @ lw['rwkv_g_lora_b']
    kk = (k * lw['rwkv_k_k']).astype(f32)
    k = (k * (1.0 + (a - 1.0) * lw['rwkv_k_a'])).astype(f32)
    heads = lambda t: t.astype(f32).reshape(B, T, RWKV_HEADS, RWKV_HEAD_DIM)
    r, k, v, kk, a, decay = (heads(t) for t in (r, k, v, kk, a, decay))
    kk = kk * lax.rsqrt(jnp.maximum(jnp.sum(kk * kk, axis=-1, keepdims=True), 1e-24))

    def step(S, inp):
        r_t, w_t, k_t, v_t, kk_t, a_t = inp
        sa = jnp.einsum('bhvk,bhk->bhv', S, -kk_t)
        S = (S * w_t[:, :, None, :] + sa[..., None] * (kk_t * a_t)[:, :, None, :]
             + v_t[..., None] * k_t[:, :, None, :])
        return S, jnp.einsum('bhvk,bhk->bhv', S, r_t)

    seq = tuple(jnp.moveaxis(t, 1, 0) for t in (r, decay, k, v, kk, a))
    s_fin, y = lax.scan(step, s0.astype(f32), seq)
    y = jnp.moveaxis(y, 0, 1)
    mean = jnp.mean(y, axis=-1, keepdims=True)
    var = jnp.mean(jnp.square(y - mean), axis=-1, keepdims=True)
    yn = ((y - mean) * lax.rsqrt(var + RWKV_LN_EPS)).reshape(B, T, RWKV_DIM)
    yn = yn * lw['rwkv_ln_g'] + lw['rwkv_ln_b']
    bonus = jnp.sum(r * k * lw['rwkv_r_k'].astype(f32), axis=-1, keepdims=True) * v
    out = (yn + bonus.reshape(B, T, RWKV_DIM)) * g
    return out.astype(pr.dtype), s_fin.astype(s0.dtype), pr[:, -1:]


def attn_prompt(q, k, v, sinks):
    B, T = q.shape[0], q.shape[1]
    nc = (T - N_META) // CHUNK
    f32 = jnp.float32
    scale = ATT_HEAD_DIM ** -0.5
    sink = sinks.astype(f32).reshape(ATT_KV_HEADS, ATT_GROUP)
    km, vm = k[:, :N_META], v[:, :N_META]
    qm = q[:, :N_META].reshape(B, N_META, ATT_KV_HEADS, ATT_GROUP, ATT_HEAD_DIM)
    s_mm = jnp.einsum('bqhgd,bkhd->bhgqk', qm, km).astype(f32) * scale
    p_mm = sink_softmax(s_mm, sink[None, :, :, None, None]).astype(v.dtype)
    o_m = jnp.einsum('bhgqk,bkhd->bqhgd', p_mm, vm).reshape(B, N_META, ATT_Q_DIM)
    qc = q[:, N_META:].reshape(B, nc, CHUNK, ATT_KV_HEADS, ATT_GROUP, ATT_HEAD_DIM)
    pad = jnp.zeros((B, WINDOW_CHUNKS * CHUNK, ATT_KV_HEADS, ATT_HEAD_DIM), k.dtype)
    kp = jnp.concatenate([pad, k[:, N_META:]], axis=1).reshape(
        B, nc + WINDOW_CHUNKS, CHUNK, ATT_KV_HEADS, ATT_HEAD_DIM)
    vp = jnp.concatenate([pad.astype(v.dtype), v[:, N_META:]], axis=1).reshape(
        B, nc + WINDOW_CHUNKS, CHUNK, ATT_KV_HEADS, ATT_HEAD_DIM)
    kb = jnp.concatenate([kp[:, i:i + nc] for i in range(WINDOW_CHUNKS + 1)], axis=2)
    vb = jnp.concatenate([vp[:, i:i + nc] for i in range(WINDOW_CHUNKS + 1)], axis=2)
    key_chunk = (jnp.arange(nc)[:, None] - WINDOW_CHUNKS
                 + jnp.repeat(jnp.arange(WINDOW_CHUNKS + 1), CHUNK)[None, :])
    valid = key_chunk >= 0
    s_meta = jnp.einsum('bcqhgd,bkhd->bchgqk', qc, km).astype(f32) * scale
    s_band = jnp.einsum('bcqhgd,bckhd->bchgqk', qc, kb).astype(f32) * scale
    s_band = jnp.where(valid[None, :, None, None, None, :], s_band, -jnp.inf)
    p = sink_softmax(jnp.concatenate([s_meta, s_band], axis=-1),
                     sink[None, None, :, :, None, None]).astype(v.dtype)
    o_f = (jnp.einsum('bchgqk,bkhd->bcqhgd', p[..., :N_META], vm)
           + jnp.einsum('bchgqk,bckhd->bcqhgd', p[..., N_META:], vb))
    return jnp.concatenate([o_m, o_f.reshape(B, nc * CHUNK, ATT_Q_DIM)], axis=1)


def attn_sample(q, k_all, v_all, sinks):
    B, Tn = q.shape[0], q.shape[1]
    f32 = jnp.float32
    sink = sinks.astype(f32).reshape(ATT_KV_HEADS, ATT_GROUP)
    qg = q.reshape(B, Tn, ATT_KV_HEADS, ATT_GROUP, ATT_HEAD_DIM)
    s = jnp.einsum('bqhgd,bkhd->bhgqk', qg, k_all).astype(f32) * (ATT_HEAD_DIM ** -0.5)
    p = sink_softmax(s, sink[None, :, :, None, None]).astype(v_all.dtype)
    return jnp.einsum('bhgqk,bkhd->bqhgd', p, v_all).reshape(B, Tn, ATT_Q_DIM)


def conv_ffn(h, conv_prev, lw):
    T = h.shape[1]
    u = h @ lw['w_up']
    gate, val = u[..., :D_FF], u[..., D_FF:]
    padded = jnp.concatenate([conv_prev.astype(gate.dtype), gate], axis=1)
    conv = lw['ffn_conv_b'] + padded[:, 0:T] * lw['ffn_conv_w'][0]
    for i in range(1, FFN_CONV):
        conv = conv + padded[:, i:i + T] * lw['ffn_conv_w'][i]
    out = (jax.nn.gelu(conv, approximate=True) * val) @ lw['w_down']
    return out, padded[:, -(FFN_CONV - 1):]


def trunk_layer(x, pos, shift_prev, rwkv_s0, conv_prev, attend, lw):
    B, T, _ = x.shape
    h = rmsnorm(x, lw['g_pre_mix'])
    p = h @ lw['w_in']
    pr = p[..., :RWKV_PROJ]
    o = RWKV_PROJ
    q = p[..., o:o + ATT_Q_DIM].reshape(B, T, ATT_HEADS, ATT_HEAD_DIM)
    o += ATT_Q_DIM
    k = p[..., o:o + ATT_KV_DIM].reshape(B, T, ATT_KV_HEADS, ATT_HEAD_DIM)
    o += ATT_KV_DIM
    v = p[..., o:o + ATT_KV_DIM].reshape(B, T, ATT_KV_HEADS, ATT_HEAD_DIM)
    o += ATT_KV_DIM
    gate_a, gate_b = p[..., o:o + D_MODEL], p[..., o + D_MODEL:]
    q, k = rope(q, pos), rope(k, pos)
    ya, s_fin, shift_new = rwkv_time_mix(pr, shift_prev, rwkv_s0, lw)
    yb = attend(q, k, v)
    mixed = (jax.nn.sigmoid(gate_a) * (ya @ lw['w_branch_a'])
             + jax.nn.sigmoid(gate_b) * (yb @ lw['w_branch_b']))
    x = x + rmsnorm(mixed @ lw['w_out'], lw['g_post_mix'])
    f, conv_new = conv_ffn(rmsnorm(x, lw['g_pre_ffn']), conv_prev, lw)
    x = x + rmsnorm(f, lw['g_post_ffn'])
    return x, k, v, s_fin, shift_new, conv_new


def setup_inputs(seed: int = 0) -> dict:
    key = jax.random.key(seed)
    ks = iter(jax.random.split(key, 48))
    nrm = lambda shape, scale: scale * jax.random.normal(next(ks), shape, jnp.float32)
    L = DEPTH
    win_rows = min(WINDOW, PAST_LEN)
    return {
        'x_prompt': nrm((BATCH, SEQ, D_MODEL), 1.0),
        'x_sample': nrm((DEC_BATCH, DEC_SEQ, D_MODEL), 1.0),
        'cache_meta_k': nrm((L, DEC_BATCH, N_META, ATT_KV_HEADS, ATT_HEAD_DIM), 1.0),
        'cache_meta_v': nrm((L, DEC_BATCH, N_META, ATT_KV_HEADS, ATT_HEAD_DIM), 1.0),
        'cache_win_k': nrm((L, DEC_BATCH, win_rows, ATT_KV_HEADS, ATT_HEAD_DIM), 1.0),
        'cache_win_v': nrm((L, DEC_BATCH, win_rows, ATT_KV_HEADS, ATT_HEAD_DIM), 1.0),
        'state_rwkv': nrm((L, DEC_BATCH, RWKV_HEADS, RWKV_HEAD_DIM, RWKV_HEAD_DIM), 0.3),
        'state_shift': nrm((L, DEC_BATCH, 1, RWKV_PROJ), 1.0),
        'state_ffn_conv': nrm((L, DEC_BATCH, FFN_CONV - 1, D_FF), 1.0),
        'meta_tokens': nrm((N_META, D_MODEL), 1.0),
        'g_pre_mix': 1.0 + nrm((L, D_MODEL), 0.02),
        'w_in': nrm((L, D_MODEL, IN_PROJ), D_MODEL ** -0.5),
        'rwkv_mu': jax.random.uniform(next(ks), (L, RWKV_PROJ), jnp.float32),
        'rwkv_w0': -1.5 + nrm((L, RWKV_DIM), 0.5),
        'rwkv_w_lora_b': nrm((L, DECAY_LORA, RWKV_DIM), 0.5 * DECAY_LORA ** -0.5),
        'rwkv_a0': nrm((L, RWKV_DIM), 0.5),
        'rwkv_a_lora_b': nrm((L, ICLR_LORA, RWKV_DIM), 0.5 * ICLR_LORA ** -0.5),
        'rwkv_g_lora_b': nrm((L, GATE_LORA, RWKV_DIM), GATE_LORA ** -0.5),
        'rwkv_k_k': 0.85 + nrm((L, RWKV_DIM), 0.05),
        'rwkv_k_a': 1.0 + nrm((L, RWKV_DIM), 0.05),
        'rwkv_r_k': nrm((L, RWKV_HEADS, RWKV_HEAD_DIM), 0.1),
        'rwkv_ln_g': 1.0 + nrm((L, RWKV_DIM), 0.02),
        'rwkv_ln_b': nrm((L, RWKV_DIM), 0.01),
        'attn_sinks': nrm((L, ATT_HEADS), 0.5),
        'w_branch_a': nrm((L, RWKV_DIM, D_MODEL), RWKV_DIM ** -0.5),
        'w_branch_b': nrm((L, ATT_Q_DIM, D_MODEL), ATT_Q_DIM ** -0.5),
        'w_out': nrm((L, D_MODEL, D_MODEL), D_MODEL ** -0.5),
        'g_post_mix': 1.0 + nrm((L, D_MODEL), 0.02),
        'g_pre_ffn': 1.0 + nrm((L, D_MODEL), 0.02),
        'w_up': nrm((L, D_MODEL, 2 * D_FF), D_MODEL ** -0.5),
        'ffn_conv_w': nrm((L, FFN_CONV, D_FF), FFN_CONV ** -0.5),
        'ffn_conv_b': nrm((L, D_FF), 0.01),
        'w_down': nrm((L, D_FF, D_MODEL), D_FF ** -0.5),
        'g_post_ffn': 1.0 + nrm((L, D_MODEL), 0.02),
    }


def reference(x_prompt, x_sample, cache_meta_k, cache_meta_v, cache_win_k, cache_win_v,
              state_rwkv, state_shift, state_ffn_conv, meta_tokens,
              g_pre_mix, w_in, rwkv_mu, rwkv_w0, rwkv_w_lora_b, rwkv_a0, rwkv_a_lora_b,
              rwkv_g_lora_b, rwkv_k_k, rwkv_k_a, rwkv_r_k, rwkv_ln_g, rwkv_ln_b, attn_sinks,
              w_branch_a, w_branch_b, w_out, g_post_mix, g_pre_ffn, w_up, ffn_conv_w,
              ffn_conv_b, w_down, g_post_ffn):
    B, S, _ = x_prompt.shape
    Tn = x_sample.shape[1]
    dt = x_prompt.dtype
    xp = jnp.concatenate(
        [jnp.broadcast_to(meta_tokens.astype(dt)[None], (B, N_META, D_MODEL)), x_prompt], axis=1)
    xs = x_sample
    pos_p = jnp.arange(N_META + S, dtype=jnp.int32)
    pos_s = N_META + PAST_LEN + jnp.arange(Tn, dtype=jnp.int32)
    zero_shift = jnp.zeros((B, 1, RWKV_PROJ), dt)
    zero_state = jnp.zeros((B, RWKV_HEADS, RWKV_HEAD_DIM, RWKV_HEAD_DIM), dt)
    zero_conv = jnp.zeros((B, FFN_CONV - 1, D_FF), dt)
    p_mk, p_mv, p_wk, p_wv, p_s, p_sh, p_c = [], [], [], [], [], [], []
    s_wk, s_wv, s_s, s_sh, s_c = [], [], [], [], []
    for l in range(DEPTH):
        lw = dict(g_pre_mix=g_pre_mix[l], w_in=w_in[l], rwkv_mu=rwkv_mu[l], rwkv_w0=rwkv_w0[l],
                  rwkv_w_lora_b=rwkv_w_lora_b[l], rwkv_a0=rwkv_a0[l],
                  rwkv_a_lora_b=rwkv_a_lora_b[l], rwkv_g_lora_b=rwkv_g_lora_b[l],
                  rwkv_k_k=rwkv_k_k[l], rwkv_k_a=rwkv_k_a[l], rwkv_r_k=rwkv_r_k[l],
                  rwkv_ln_g=rwkv_ln_g[l], rwkv_ln_b=rwkv_ln_b[l], w_branch_a=w_branch_a[l],
                  w_branch_b=w_branch_b[l], w_out=w_out[l], g_post_mix=g_post_mix[l],
                  g_pre_ffn=g_pre_ffn[l], w_up=w_up[l], ffn_conv_w=ffn_conv_w[l],
                  ffn_conv_b=ffn_conv_b[l], w_down=w_down[l], g_post_ffn=g_post_ffn[l])
        sinks = attn_sinks[l]
        xp, kp, vp, sp, shp, cp = trunk_layer(
            xp, pos_p, zero_shift, zero_state, zero_conv,
            lambda q, k, v: attn_prompt(q, k, v, sinks), lw)
        p_mk.append(kp[:, :N_META])
        p_mv.append(vp[:, :N_META])
        p_wk.append(kp[:, -WINDOW:])
        p_wv.append(vp[:, -WINDOW:])
        p_s.append(sp)
        p_sh.append(shp)
        p_c.append(cp)
        mk_c, mv_c = cache_meta_k[l], cache_meta_v[l]
        wk_c, wv_c = cache_win_k[l], cache_win_v[l]
        xs, kn, vn, sn, shn, cn = trunk_layer(
            xs, pos_s, state_shift[l], state_rwkv[l], state_ffn_conv[l],
            lambda q, k, v: attn_sample(q, jnp.concatenate([mk_c, wk_c, k], axis=1),
                                        jnp.concatenate([mv_c, wv_c, v], axis=1), sinks), lw)
        s_wk.append(kn)
        s_wv.append(vn)
        s_s.append(sn)
        s_sh.append(shn)
        s_c.append(cn)
    y_prompt = xp[:, N_META:]
    y_sample = xs
    prompt_meta_k, prompt_meta_v = jnp.stack(p_mk), jnp.stack(p_mv)
    prompt_win_k, prompt_win_v = jnp.stack(p_wk), jnp.stack(p_wv)
    prompt_rwkv, prompt_shift, prompt_ffn_conv = jnp.stack(p_s), jnp.stack(p_sh), jnp.stack(p_c)
    sample_win_k, sample_win_v = jnp.stack(s_wk), jnp.stack(s_wv)
    sample_rwkv, sample_shift, sample_ffn_conv = jnp.stack(s_s), jnp.stack(s_sh), jnp.stack(s_c)
    return (y_prompt, y_sample, prompt_meta_k, prompt_meta_v, prompt_win_k, prompt_win_v,
            prompt_rwkv, prompt_shift, prompt_ffn_conv, sample_win_k, sample_win_v,
            sample_rwkv, sample_shift, sample_ffn_conv)
```

```python
import functools
import math

import jax
import jax.numpy as jnp
from jax import lax
from jax.experimental import pallas as pl
from jax.experimental.pallas import tpu as pltpu

F32 = jnp.float32
BF16 = jnp.bfloat16

D_MODEL = 2048
CHUNK = 64
N_META = 16
NORM_EPS = 1e-6
HEADS = 16
HEAD_DIM = 64
RWKV_DIM = HEADS * HEAD_DIM
DECAY_LORA = 64
ICLR_LORA = 64
GATE_LORA = 160
RWKV_PROJ = 3 * RWKV_DIM + DECAY_LORA + ICLR_LORA + GATE_LORA
RWKV_LN_EPS = 64e-5
KV_HEADS = 2
GROUP = HEADS // KV_HEADS
Q_DIM = HEADS * HEAD_DIM
KV_DIM = KV_HEADS * HEAD_DIM
WINDOW = 128
ROPE_THETA = 10000.0
D_FF = 3 * D_MODEL
FFN_CONV = 3
PAST_LEN = 4096

SUBLANES = 8
LANES = 128
V_LO = LANES // HEADS
V_HI = HEAD_DIM // V_LO

COL_R, COL_K, COL_V, COL_Q = 0, 1024, 2048, 3072
COL_GA, COL_GB = 4096, 6144
COL_LORA = 8192
LORA_W = 512
LORA_XW, LORA_XA, LORA_XG = 0, 128, 256
COL_KV = 8704
N_PACK = 8960
RWKV_PACK = 3 * RWKV_DIM + LORA_W

VMEM_LIMIT = 56 * 1024 * 1024


def _cparams(sem):
    return pltpu.CompilerParams(dimension_semantics=sem, vmem_limit_bytes=VMEM_LIMIT)


def _rms(x, g):
    return x * lax.rsqrt(jnp.mean(x * x, axis=-1, keepdims=True) + NORM_EPS) * g


def _split3(x):
    hi = x.astype(BF16)
    r1 = x - hi.astype(F32)
    mid = r1.astype(BF16)
    lo = (r1 - mid.astype(F32)).astype(BF16)
    return hi, mid, lo


def _seg_sum(x, ones_bd):
    parts = _split3(x)
    outs = []
    for j in range(x.shape[1] // LANES):
        sl = slice(j * LANES, (j + 1) * LANES)
        acc = jnp.dot(parts[0][:, sl], ones_bd, preferred_element_type=F32)
        acc = acc + jnp.dot(parts[1][:, sl], ones_bd, preferred_element_type=F32)
        acc = acc + jnp.dot(parts[2][:, sl], ones_bd, preferred_element_type=F32)
        outs.append(acc)
    return jnp.concatenate(outs, axis=1)


def _norm_matmul_kernel(x_ref, g_ref, w_ref, o_ref, h_ref):
    @pl.when(pl.program_id(1) == 0)
    def _():
        h_ref[...] = _rms(x_ref[...], g_ref[...]).astype(BF16)

    o_ref[...] = jnp.dot(h_ref[...], w_ref[...], preferred_element_type=F32)


def _norm_matmul(x, g, w, tm, tn):
    m, d = x.shape
    n = w.shape[1]
    return pl.pallas_call(
        _norm_matmul_kernel,
        grid=(m // tm, n // tn),
        in_specs=[pl.BlockSpec((tm, d), lambda i, j: (i, 0)),
                  pl.BlockSpec((1, d), lambda i, j: (0, 0)),
                  pl.BlockSpec((d, tn), lambda i, j: (0, j))],
        out_specs=pl.BlockSpec((tm, tn), lambda i, j: (i, j)),
        out_shape=jax.ShapeDtypeStruct((m, n), F32),
        scratch_shapes=[pltpu.VMEM((tm, d), BF16)],
        compiler_params=_cparams(("parallel", "arbitrary")),
        name="norm_matmul",
    )(x, g, w)


def _interleave_heads(a, b):
    lane = lax.broadcasted_iota(jnp.int32, (a.shape[0], LANES), 1)
    low = lane < HEAD_DIM
    outs = []
    for j in range(a.shape[1] // LANES):
        sl = slice(j * LANES, (j + 1) * LANES)
        aj, bj = a[:, sl], b[:, sl]
        outs.append(jnp.where(low, aj, pltpu.roll(bj, HEAD_DIM, 1)))
        outs.append(jnp.where(low, pltpu.roll(aj, HEAD_DIM, 1), bj))
    return jnp.concatenate(outs, axis=1)


def _rwkv_prep_kernel(r_ref, k_ref, v_ref, l_ref, prev_ref, mu_ref, w0_ref, a0_ref, kk_ref, ka_ref,
                      rk_ref, ww_ref, wa_ref, wg_ref, ones_ref,
                      ab_ref, wk_ref, ro_ref, vo_ref, g_ref, bonus_ref):
    rows = r_ref.shape[0]
    first = lax.broadcasted_iota(jnp.int32, (rows, 1), 0) == 0

    def shifted(x, lo, hi):
        prev = jnp.where(first, prev_ref[:, lo:hi], pltpu.roll(x, 1, 0))
        return x + (prev - x) * mu_ref[:, lo:hi]

    r = shifted(r_ref[...], 0, RWKV_DIM)
    k = shifted(k_ref[...], RWKV_DIM, 2 * RWKV_DIM)
    v = shifted(v_ref[...], 2 * RWKV_DIM, 3 * RWKV_DIM)
    lora = shifted(l_ref[...], 3 * RWKV_DIM, RWKV_PACK)
    xw = lora[:, LORA_XW:LORA_XW + LANES]
    xa = lora[:, LORA_XA:LORA_XA + LANES]
    xg = lora[:, LORA_XG:LORA_XG + 2 * LANES]

    w_pre = w0_ref[...] + jnp.dot(jnp.tanh(xw).astype(BF16), ww_ref[...], preferred_element_type=F32)
    decay = jnp.exp(-math.exp(-0.5) * jax.nn.sigmoid(w_pre))
    a = jax.nn.sigmoid(a0_ref[...] + jnp.dot(xa.astype(BF16), wa_ref[...], preferred_element_type=F32))
    g = jnp.dot(jax.nn.sigmoid(xg).astype(BF16), wg_ref[...], preferred_element_type=F32)

    ones_bd = ones_ref[...]
    kk = k * kk_ref[...]
    kk = kk * lax.rsqrt(jnp.maximum(_seg_sum(kk * kk, ones_bd), 1e-24))
    k_mod = k * (1.0 + (a - 1.0) * ka_ref[...])
    bonus = _seg_sum(r * k_mod * rk_ref[...], ones_bd) * v

    ab_ref[...] = _interleave_heads(-kk, kk * a)
    wk_ref[...] = _interleave_heads(decay, k_mod)
    ro_ref[...] = r
    vo_ref[...] = v
    g_ref[...] = g
    bonus_ref[...] = bonus


def _rwkv_prep(p_all, prev_rows, pw, tb):
    m = p_all.shape[0]
    nt = m // tb
    row = lambda c: pl.BlockSpec((tb, RWKV_DIM), lambda i, c=c: (i, c))
    full = lambda a: pl.BlockSpec(a.shape, lambda i: (0,) * a.ndim)
    params = (pw["mu"], pw["w0"], pw["a0"], pw["k_k"], pw["k_a"], pw["r_k"],
              pw["w_lora"], pw["a_lora"], pw["g_lora"], pw["ones_bd"])
    wide = jax.ShapeDtypeStruct((m, 2 * RWKV_DIM), F32)
    narrow = jax.ShapeDtypeStruct((m, RWKV_DIM), F32)
    return pl.pallas_call(
        _rwkv_prep_kernel,
        grid=(nt,),
        in_specs=[row(COL_R // RWKV_DIM), row(COL_K // RWKV_DIM), row(COL_V // RWKV_DIM),
                  pl.BlockSpec((tb, LORA_W), lambda i: (i, COL_LORA // LORA_W)),
                  pl.BlockSpec((None, 1, RWKV_PACK), lambda i: (i, 0, 0))]
                 + [full(a) for a in params],
        out_specs=[pl.BlockSpec((tb, 2 * RWKV_DIM), lambda i: (i, 0)),
                   pl.BlockSpec((tb, 2 * RWKV_DIM), lambda i: (i, 0))]
                  + [pl.BlockSpec((tb, RWKV_DIM), lambda i: (i, 0))] * 4,
        out_shape=[wide, wide, narrow, narrow, narrow, narrow],
        compiler_params=_cparams(("parallel",)),
        name="rwkv_prep",
    )(p_all, p_all, p_all, p_all, prev_rows, *params)


def _scan_kernel(ab_ref, wk_ref, r_ref, vb_ref, s0_ref, y_ref, st_ref, ops_ref, *, tb, nblk):
    @pl.when(pl.program_id(0) % nblk == 0)
    def _():
        st_ref[...] = s0_ref[...]

    def col_bcast(ref, t):
        base = pl.multiple_of(t * HEADS, HEADS)
        rep = jnp.concatenate(
            [jnp.broadcast_to(ref[pl.ds(base + h, 1), :], (V_LO, ref.shape[1])) for h in range(HEADS)],
            axis=0)
        return rep.T

    def stage(t, slot):
        ab = col_bcast(ab_ref, t)
        wk = col_bcast(wk_ref, t)
        ops_ref[slot, 0] = ab[:HEAD_DIM]
        ops_ref[slot, 1] = ab[HEAD_DIM:]
        ops_ref[slot, 2] = wk[:HEAD_DIM]
        ops_ref[slot, 3] = wk[HEAD_DIM:]
        ops_ref[slot, 4] = col_bcast(r_ref, t)

    def step(t, slot):
        alpha = ops_ref[slot, 0]
        beta = ops_ref[slot, 1]
        decay = ops_ref[slot, 2]
        kmod = ops_ref[slot, 3]
        rr = ops_ref[slot, 4]
        vrow = vb_ref[pl.ds(pl.multiple_of(t * V_HI, V_HI), V_HI), :]
        ys = []
        for vh in range(V_HI):
            s_old = st_ref[vh]
            u = jnp.sum(s_old * alpha, axis=0, keepdims=True)
            s_new = s_old * decay + beta * u + kmod * vrow[vh:vh + 1, :]
            st_ref[vh] = s_new
            ys.append(jnp.sum(s_new * rr, axis=0, keepdims=True))
        y_ref[pl.ds(pl.multiple_of(t * V_HI, V_HI), V_HI), :] = jnp.concatenate(ys, axis=0)

    stage(0, 0)

    def body(t, carry):
        stage(jnp.minimum(t + 1, tb - 1), (t + 1) % 2)
        step(t, t % 2)
        return carry

    lax.fori_loop(0, tb, body, 0)


def _rwkv_scan(ab, wk, r, vb, s0, tb, nblk):
    t_total = vb.shape[0] // V_HI
    nseq = s0.shape[0]
    grid = t_total // tb
    assert grid == nseq * nblk
    st_shape = (V_HI, HEAD_DIM, LANES)
    return pl.pallas_call(
        functools.partial(_scan_kernel, tb=tb, nblk=nblk),
        grid=(grid,),
        in_specs=[pl.BlockSpec((tb * HEADS, LANES), lambda i: (i, 0)),
                  pl.BlockSpec((tb * HEADS, LANES), lambda i: (i, 0)),
                  pl.BlockSpec((tb * HEADS, HEAD_DIM), lambda i: (i, 0)),
                  pl.BlockSpec((tb * V_HI, LANES), lambda i: (i, 0)),
                  pl.BlockSpec((None,) + st_shape, lambda i: (i // nblk, 0, 0, 0))],
        out_specs=[pl.BlockSpec((tb * V_HI, LANES), lambda i: (i, 0)),
                   pl.BlockSpec((None,) + st_shape, lambda i: (i // nblk, 0, 0, 0))],
        out_shape=[jax.ShapeDtypeStruct((t_total * V_HI, LANES), F32),
                   jax.ShapeDtypeStruct((nseq,) + st_shape, F32)],
        scratch_shapes=[pltpu.VMEM((2, 5, HEAD_DIM, LANES), F32)],
        compiler_params=_cparams(("arbitrary",)),
        name="rwkv_scan",
    )(ab, wk, r, vb, s0)


def _state_to_scan(s):
    b = s.shape[0]
    s = s.reshape(b, HEADS, V_HI, V_LO, HEAD_DIM)
    return s.transpose(0, 2, 4, 1, 3).reshape(b, V_HI, HEAD_DIM, LANES)


def _state_from_scan(st):
    b = st.shape[0]
    st = st.reshape(b, V_HI, HEAD_DIM, HEADS, V_LO)
    return st.transpose(0, 3, 1, 4, 2).reshape(b, HEADS, HEAD_DIM, HEAD_DIM)


def _v_to_scan(v):
    t = v.shape[0]
    return v.reshape(t, HEADS, V_HI, V_LO).transpose(0, 2, 1, 3).reshape(t * V_HI, LANES)


def _y_from_scan(yb):
    t = yb.shape[0] // V_HI
    return yb.reshape(t, V_HI, HEADS, V_LO).transpose(0, 2, 1, 3).reshape(t, RWKV_DIM)


def _rope_tile(x, cos, sin_signed):
    lane = lax.broadcasted_iota(jnp.int32, x.shape, 1)
    first_half = (lane % HEAD_DIM) < (HEAD_DIM // 2)
    rot = jnp.where(first_half, pltpu.roll(x, LANES - HEAD_DIM // 2, 1), pltpu.roll(x, HEAD_DIM // 2, 1))
    return x * cos + rot * sin_signed


def _rope_kernel(q_ref, kv_ref, cos_ref, sin_ref, qo_ref, ko_ref):
    cos = cos_ref[...]
    sin = sin_ref[...]
    for j in range(Q_DIM // LANES):
        sl = slice(j * LANES, (j + 1) * LANES)
        qo_ref[:, sl] = _rope_tile(q_ref[:, sl], cos, sin).astype(BF16)
    ko_ref[...] = _rope_tile(kv_ref[:, :KV_DIM], cos, sin)


def _rope(p_all, cos, sin, tb):
    m = p_all.shape[0]
    return pl.pallas_call(
        _rope_kernel,
        grid=(m // tb,),
        in_specs=[pl.BlockSpec((tb, Q_DIM), lambda i: (i, COL_Q // Q_DIM)),
                  pl.BlockSpec((tb, 2 * KV_DIM), lambda i: (i, COL_KV // (2 * KV_DIM))),
                  pl.BlockSpec((tb, LANES), lambda i: (i, 0)),
                  pl.BlockSpec((tb, LANES), lambda i: (i, 0))],
        out_specs=[pl.BlockSpec((tb, Q_DIM), lambda i: (i, 0)),
                   pl.BlockSpec((tb, KV_DIM), lambda i: (i, 0))],
        out_shape=[jax.ShapeDtypeStruct((m, Q_DIM), BF16),
                   jax.ShapeDtypeStruct((m, KV_DIM), F32)],
        compiler_params=_cparams(("parallel",)),
        name="rope",
    )(p_all, p_all, cos, sin)


def _rope_tables(pos):
    half = HEAD_DIM // 2
    inv_freq = ROPE_THETA ** (-jnp.arange(half, dtype=F32) / half)
    ang = pos.astype(F32)[:, None] * inv_freq[None, :]
    cos = jnp.cos(ang)
    sin = jnp.sin(ang)
    cos = jnp.concatenate([cos, cos, cos, cos], axis=1)
    sin = jnp.concatenate([-sin, sin, -sin, sin], axis=1)
    return cos, sin


def _attn_kernel(*refs, nseg, band_mask):
    q_ref, sink_ref = refs[0], refs[1]
    k_refs = refs[2:2 + nseg]
    v_refs = refs[2 + nseg:2 + 2 * nseg]
    o_ref = refs[2 + 2 * nseg]

    k_all = jnp.concatenate([r[...] for r in k_refs], axis=0).astype(BF16)
    v_all = jnp.concatenate([r[...] for r in v_refs], axis=0).astype(BF16)
    nkeys = k_all.shape[0]
    if band_mask:
        c = pl.program_id(0)
        col = lax.broadcasted_iota(jnp.int32, (1, nkeys), 1)
        first_valid = N_META + jnp.maximum(2 - c, 0) * CHUNK
        visible = (col < N_META) | (col >= first_valid)
        bias = jnp.where(visible, 0.0, -jnp.inf).astype(F32)
    scale = HEAD_DIM ** -0.5
    q = q_ref[...]
    outs = []
    for h in range(HEADS):
        kvh = h // GROUP
        kh = k_all[:, kvh * HEAD_DIM:(kvh + 1) * HEAD_DIM]
        vh = v_all[:, kvh * HEAD_DIM:(kvh + 1) * HEAD_DIM]
        qh = q[:, h * HEAD_DIM:(h + 1) * HEAD_DIM]
        s = lax.dot_general(qh, kh, (((1,), (1,)), ((), ())), preferred_element_type=F32) * scale
        if band_mask:
            s = s + bias
        sink = sink_ref[:, h:h + 1]
        m = jnp.maximum(jnp.max(s, axis=1, keepdims=True), sink)
        p = jnp.exp(s - m)
        den = jnp.sum(p, axis=1, keepdims=True) + jnp.exp(sink - m)
        p = (p / den).astype(BF16)
        outs.append(jnp.dot(p, vh, preferred_element_type=F32))
    o_ref[...] = jnp.concatenate(outs, axis=1).astype(BF16)


def _attention(q, sinks, k_segs, v_segs, tq, grid, band_mask):
    nseg = len(k_segs)
    seg_specs = [pl.BlockSpec(bs, im) for (_, bs, im) in k_segs + v_segs]
    return pl.pallas_call(
        functools.partial(_attn_kernel, nseg=nseg, band_mask=band_mask),
        grid=(grid,),
        in_specs=[pl.BlockSpec((tq, Q_DIM), lambda i: (i, 0)),
                  pl.BlockSpec((1, HEADS), lambda i: (0, 0))] + seg_specs,
        out_specs=pl.BlockSpec((tq, Q_DIM), lambda i: (i, 0)),
        out_shape=jax.ShapeDtypeStruct((grid * tq, Q_DIM), BF16),
        compiler_params=_cparams(("parallel",)),
        name="attention",
    )(q, sinks, *[a for (a, _, _) in k_segs + v_segs])


def _mix_kernel(y_ref, bonus_ref, g_ref, yb_ref, ga_ref, gb_ref, lng_ref, lnb_ref, ones_ref,
                wa_ref, wb_ref, o_ref, ya_ref):
    @pl.when(pl.program_id(1) == 0)
    def _():
        ones_bd = ones_ref[...]
        y = y_ref[...]
        mean = _seg_sum(y, ones_bd) * (1.0 / HEAD_DIM)
        d = y - mean
        var = _seg_sum(d * d, ones_bd) * (1.0 / HEAD_DIM)
        yn = d * lax.rsqrt(var + RWKV_LN_EPS) * lng_ref[...] + lnb_ref[...]
        ya_ref[...] = ((yn + bonus_ref[...]) * g_ref[...]).astype(BF16)

    pa = jnp.dot(ya_ref[...], wa_ref[...], preferred_element_type=F32)
    pb = jnp.dot(yb_ref[...], wb_ref[...], preferred_element_type=F32)
    o_ref[...] = (jax.nn.sigmoid(ga_ref[...]) * pa + jax.nn.sigmoid(gb_ref[...]) * pb).astype(BF16)


def _mix(y, bonus, g, yb, p_all, ln_g, ln_b, ones_bd, wa, wb, tm):
    m = y.shape[0]
    tn = 1024
    row = pl.BlockSpec((tm, RWKV_DIM), lambda i, j: (i, 0))
    one = lambda a: pl.BlockSpec(a.shape, lambda i, j: (0,) * a.ndim)
    return pl.pallas_call(
        _mix_kernel,
        grid=(m // tm, D_MODEL // tn),
        in_specs=[row, row, row, row,
                  pl.BlockSpec((tm, tn), lambda i, j: (i, COL_GA // tn + j)),
                  pl.BlockSpec((tm, tn), lambda i, j: (i, COL_GB // tn + j)),
                  one(ln_g), one(ln_b), one(ones_bd),
                  pl.BlockSpec((RWKV_DIM, tn), lambda i, j: (0, j)),
                  pl.BlockSpec((Q_DIM, tn), lambda i, j: (0, j))],
        out_specs=pl.BlockSpec((tm, tn), lambda i, j: (i, j)),
        out_shape=jax.ShapeDtypeStruct((m, D_MODEL), BF16),
        scratch_shapes=[pltpu.VMEM((tm, RWKV_DIM), BF16)],
        compiler_params=_cparams(("parallel", "arbitrary")),
        name="branch_mix",
    )(y, bonus, g, yb, p_all, p_all, ln_g, ln_b, ones_bd, wa, wb)


def _out_norm_kernel(mix_ref, w_ref, x_ref, g_ref, o_ref):
    z = jnp.dot(mix_ref[...], w_ref[...], preferred_element_type=F32)
    o_ref[...] = x_ref[...] + _rms(z, g_ref[...])


def _out_norm(mixed, w, x, g, tm):
    m = x.shape[0]
    return pl.pallas_call(
        _out_norm_kernel,
        grid=(m // tm,),
        in_specs=[pl.BlockSpec((tm, D_MODEL), lambda i: (i, 0)),
                  pl.BlockSpec((D_MODEL, D_MODEL), lambda i: (0, 0)),
                  pl.BlockSpec((tm, D_MODEL), lambda i: (i, 0)),
                  pl.BlockSpec((1, D_MODEL), lambda i: (0, 0))],
        out_specs=pl.BlockSpec((tm, D_MODEL), lambda i: (i, 0)),
        out_shape=jax.ShapeDtypeStruct((m, D_MODEL), F32),
        compiler_params=_cparams(("parallel",)),
        name="out_norm",
    )(mixed, w, x, g)


HALO = 16


def _gelu_tanh(x):
    return 0.5 * x * (1.0 + jnp.tanh(0.7978845608028654 * (x + 0.044715 * x * x * x)))


def _ffn_kernel(*refs, seq_len):
    if seq_len is None:
        (x_ref, halo_ref, gpre_ref, wg_ref, wv_ref, wd_ref, cw_ref, cb_ref, gpost_ref,
         o_ref, tail_ref, h_ref, acc_ref) = refs
    else:
        (x_ref, sa_ref, sb_ref, gpre_ref, wg_ref, wv_ref, wd_ref, cw_ref, cb_ref, gpost_ref,
         o_ref, tail_ref, h_ref, acc_ref) = refs
    tm = x_ref.shape[0]
    j = pl.program_id(1)
    top = HALO if seq_len is None else 0

    @pl.when(j == 0)
    def _():
        if seq_len is None:
            halo = _rms(halo_ref[...], gpre_ref[...])
            halo = jnp.where(pl.program_id(0) == 0, 0.0, halo)
            h_ref[:HALO] = halo.astype(BF16)
        h_ref[top:] = _rms(x_ref[...], gpre_ref[...]).astype(BF16)
        acc_ref[...] = jnp.zeros_like(acc_ref)

    h_all = h_ref[...]
    gate_all = jnp.dot(h_all, wg_ref[...], preferred_element_type=F32)
    val = jnp.dot(h_all[top:], wv_ref[...], preferred_element_type=F32)
    g0 = gate_all[top:]
    g1 = pltpu.roll(gate_all, 1, 0)[top:]
    g2 = pltpu.roll(gate_all, 2, 0)[top:]
    if seq_len is not None:
        nseq = tm // seq_len
        tf = g0.shape[1]
        tau = lax.broadcasted_iota(jnp.int32, (tm, 1), 0) % seq_len
        expand = lambda s: jnp.concatenate(
            [jnp.broadcast_to(s[b:b + 1, :], (seq_len, tf)) for b in range(nseq)], axis=0)
        st_a = expand(sa_ref[...])
        st_b = expand(sb_ref[...])
        g1 = jnp.where(tau == 0, st_b, g1)
        g2 = jnp.where(tau == 0, st_a, jnp.where(tau == 1, st_b, g2))
    conv = cb_ref[...] + g2 * cw_ref[0:1, :] + g1 * cw_ref[1:2, :] + g0 * cw_ref[2:3, :]
    act = (_gelu_tanh(conv) * val).astype(BF16)
    acc_ref[...] += jnp.dot(act, wd_ref[...], preferred_element_type=F32)

    if seq_len is None:
        tail_ref[...] = g0[tm - SUBLANES:]
    else:
        for b in range(tm // seq_len):
            tail_ref[b] = g0[(b + 1) * seq_len - SUBLANES:(b + 1) * seq_len]

    @pl.when(j == pl.num_programs(1) - 1)
    def _():
        o_ref[...] = x_ref[...] + _rms(acc_ref[...], gpost_ref[...])


def _ffn(x1, conv_prev, fw, tm, tf, seq_len):
    m = x1.shape[0]
    nm, nf = m // tm, D_FF // tf
    one = lambda a: pl.BlockSpec(a.shape, lambda i, j: (0,) * a.ndim)
    xspec = pl.BlockSpec((tm, D_MODEL), lambda i, j: (i, 0))
    wspecs = [pl.BlockSpec((D_MODEL, tf), lambda i, j: (0, j)),
              pl.BlockSpec((D_MODEL, tf), lambda i, j: (0, D_FF // tf + j)),
              pl.BlockSpec((tf, D_MODEL), lambda i, j: (j, 0)),
              pl.BlockSpec((FFN_CONV, tf), lambda i, j: (0, j)),
              pl.BlockSpec((1, tf), lambda i, j: (0, j)),
              one(fw["g_post"])]
    wargs = (fw["w_up"], fw["w_up"], fw["w_down"], fw["conv_w"], fw["conv_b"], fw["g_post"])
    if seq_len is None:
        assert tm % HALO == 0
        pre_specs = [xspec,
                     pl.BlockSpec((HALO, D_MODEL), lambda i, j: (jnp.maximum(i * (tm // HALO) - 1, 0), 0)),
                     one(fw["g_pre"])]
        pre_args = (x1, x1, fw["g_pre"])
        tail_shape = (nm, SUBLANES, D_FF)
        tail_spec = pl.BlockSpec((None, SUBLANES, tf), lambda i, j: (i, 0, j))
        hrows = tm + HALO
    else:
        nseq = tm // seq_len
        assert nm == 1
        pre_specs = [xspec,
                     pl.BlockSpec((nseq, tf), lambda i, j: (0, j)),
                     pl.BlockSpec((nseq, tf), lambda i, j: (0, j)),
                     one(fw["g_pre"])]
        pre_args = (x1, conv_prev[:, 0, :], conv_prev[:, 1, :], fw["g_pre"])
        tail_shape = (nseq, SUBLANES, D_FF)
        tail_spec = pl.BlockSpec((nseq, SUBLANES, tf), lambda i, j: (0, 0, j))
        hrows = tm
    return pl.pallas_call(
        functools.partial(_ffn_kernel, seq_len=seq_len),
        grid=(nm, nf),
        in_specs=pre_specs + wspecs,
        out_specs=[pl.BlockSpec((tm, D_MODEL), lambda i, j: (i, 0)), tail_spec],
        out_shape=[jax.ShapeDtypeStruct((m, D_MODEL), F32),
                   jax.ShapeDtypeStruct(tail_shape, F32)],
        scratch_shapes=[pltpu.VMEM((hrows, D_MODEL), BF16), pltpu.VMEM((tm, D_MODEL), F32)],
        compiler_params=_cparams(("parallel", "arbitrary")),
        name="conv_ffn",
    )(*pre_args, *wargs)


def _pack_cols(w):
    rows = w.shape[0]
    out = jnp.zeros((rows, N_PACK), w.dtype)
    o = 3 * RWKV_DIM
    out = out.at[:, COL_R:COL_R + 3 * RWKV_DIM].set(w[:, :o])
    out = out.at[:, COL_LORA + LORA_XW:COL_LORA + LORA_XW + DECAY_LORA].set(w[:, o:o + DECAY_LORA])
    o += DECAY_LORA
    out = out.at[:, COL_LORA + LORA_XA:COL_LORA + LORA_XA + ICLR_LORA].set(w[:, o:o + ICLR_LORA])
    o += ICLR_LORA
    out = out.at[:, COL_LORA + LORA_XG:COL_LORA + LORA_XG + GATE_LORA].set(w[:, o:o + GATE_LORA])
    o += GATE_LORA
    out = out.at[:, COL_Q:COL_Q + Q_DIM].set(w[:, o:o + Q_DIM])
    o += Q_DIM
    out = out.at[:, COL_KV:COL_KV + 2 * KV_DIM].set(w[:, o:o + 2 * KV_DIM])
    o += 2 * KV_DIM
    out = out.at[:, COL_GA:COL_GA + 2 * D_MODEL].set(w[:, o:o + 2 * D_MODEL])
    return out


def _pack_rwkv_row(x):
    rows = x.shape[0]
    lora = jnp.zeros((rows, LORA_W), x.dtype)
    o = 3 * RWKV_DIM
    lora = lora.at[:, LORA_XW:LORA_XW + DECAY_LORA].set(x[:, o:o + DECAY_LORA])
    o += DECAY_LORA
    lora = lora.at[:, LORA_XA:LORA_XA + ICLR_LORA].set(x[:, o:o + ICLR_LORA])
    o += ICLR_LORA
    lora = lora.at[:, LORA_XG:LORA_XG + GATE_LORA].set(x[:, o:o + GATE_LORA])
    return jnp.concatenate([x[:, :3 * RWKV_DIM], lora], axis=1)


def _unpack_rwkv_row(p_row):
    lora = p_row[:, COL_LORA:COL_LORA + LORA_W]
    return jnp.concatenate([p_row[:, :3 * RWKV_DIM],
                            lora[:, LORA_XW:LORA_XW + DECAY_LORA],
                            lora[:, LORA_XA:LORA_XA + ICLR_LORA],
                            lora[:, LORA_XG:LORA_XG + GATE_LORA]], axis=1)


def _pick_tile(m, candidates):
    for c in candidates:
        if m % c == 0:
            return c
    raise ValueError(f"no row tile in {candidates} divides {m}")


def _pad_rows(w, rows):
    return jnp.concatenate([w, jnp.zeros((rows - w.shape[0], w.shape[1]), w.dtype)], axis=0)


def _trunk(x, lw, *, tm, tm_mix, tb_prep, prev_rows_fn, scan_tb, scan_nblk, s0, cos, sin, attend, ffn_tm,
           ffn_seq_len, conv_prev):
    m = x.shape[0]
    p_all = _norm_matmul(x, lw["g_pre_mix"], lw["w_in"], tm, 1280)

    ab, wk, r, vx, g, bonus = _rwkv_prep(p_all, prev_rows_fn(p_all), lw, tb_prep)
    yb_scan, s_fin = _rwkv_scan(ab.reshape(m * HEADS, LANES), wk.reshape(m * HEADS, LANES),
                                r.reshape(m * HEADS, HEAD_DIM), _v_to_scan(vx), s0, scan_tb, scan_nblk)
    y = _y_from_scan(yb_scan)

    q_roped, k_roped = _rope(p_all, cos, sin, tm)
    v_att = p_all[:, COL_KV + KV_DIM:COL_KV + 2 * KV_DIM]
    att = attend(q_roped, k_roped, v_att)

    mixed = _mix(y, bonus, g, att, p_all, lw["ln_g"], lw["ln_b"], lw["ones_bd"],
                 lw["w_branch_a"], lw["w_branch_b"], tm_mix)
    x1 = _out_norm(mixed, lw["w_out"], x, lw["g_post_mix"], tm_mix)
    x2, tail = _ffn(x1, conv_prev, lw["ffn"], ffn_tm, 512, ffn_seq_len)
    return x2, p_all, k_roped, v_att, s_fin, tail


def kernel(x_prompt, x_sample, cache_meta_k, cache_meta_v, cache_win_k, cache_win_v, state_rwkv, state_shift, state_ffn_conv, meta_tokens, g_pre_mix, w_in, rwkv_mu, rwkv_w0, rwkv_w_lora_b, rwkv_a0, rwkv_a_lora_b, rwkv_g_lora_b, rwkv_k_k, rwkv_k_a, rwkv_r_k, rwkv_ln_g, rwkv_ln_b, attn_sinks, w_branch_a, w_branch_b, w_out, g_post_mix, g_pre_ffn, w_up, ffn_conv_w, ffn_conv_b, w_down, g_post_ffn):
    depth = w_in.shape[0]
    assert depth == 1
    l = 0
    seq = x_prompt.shape[1]
    nb, tn = x_sample.shape[0], x_sample.shape[1]
    row = lambda a: a.reshape(1, -1)
    lane = jnp.arange(LANES)
    ones_bd = (lane[:, None] // HEAD_DIM == lane[None, :] // HEAD_DIM).astype(BF16)
    lw = dict(
        g_pre_mix=row(g_pre_mix[l]),
        w_in=_pack_cols(w_in[l]).astype(BF16),
        mu=_pack_rwkv_row(row(rwkv_mu[l])),
        w0=row(rwkv_w0[l]), a0=row(rwkv_a0[l]), k_k=row(rwkv_k_k[l]), k_a=row(rwkv_k_a[l]),
        r_k=row(rwkv_r_k[l]),
        w_lora=_pad_rows(rwkv_w_lora_b[l], LANES).astype(BF16),
        a_lora=_pad_rows(rwkv_a_lora_b[l], LANES).astype(BF16),
        g_lora=_pad_rows(rwkv_g_lora_b[l], 2 * LANES).astype(BF16),
        ones_bd=ones_bd,
        ln_g=row(rwkv_ln_g[l]), ln_b=row(rwkv_ln_b[l]),
        w_branch_a=w_branch_a[l].astype(BF16), w_branch_b=w_branch_b[l].astype(BF16),
        w_out=w_out[l].astype(BF16), g_post_mix=row(g_post_mix[l]),
        ffn=dict(g_pre=row(g_pre_ffn[l]), w_up=w_up[l].astype(BF16), w_down=w_down[l].astype(BF16),
                 conv_w=ffn_conv_w[l], conv_b=row(ffn_conv_b[l]), g_post=row(g_post_ffn[l])),
    )
    sinks = row(attn_sinks[l])

    mp = N_META + seq
    xp = jnp.concatenate([meta_tokens.astype(x_prompt.dtype), x_prompt[0]], axis=0)
    nchunk = seq // CHUNK
    tm_p = _pick_tile(mp, (912, 144, 48))
    tm_mix_p = _pick_tile(mp, (432, 144, 48))
    tb_prep_p = _pick_tile(mp, (144, 48))
    cos_p, sin_p = _rope_tables(jnp.arange(mp, dtype=jnp.int32))

    def prev_rows_prompt(p_all):
        tails = p_all[tb_prep_p - 1:mp - 1:tb_prep_p]
        tails = jnp.concatenate([tails[:, :3 * RWKV_DIM], tails[:, COL_LORA:COL_LORA + LORA_W]], axis=1)
        zero = jnp.zeros((1, RWKV_PACK), F32)
        return jnp.concatenate([zero, tails], axis=0)[:, None, :]

    def attend_prompt(q, k, v):
        mk, mv = k[:N_META], v[:N_META]
        whole = lambda a: (a, (N_META, KV_DIM), lambda c: (0, 0))
        o_m = _attention(q[:N_META], sinks, [whole(mk)], [whole(mv)], N_META, 1, False)
        kf, vf = k[N_META:], v[N_META:]
        band = lambda a: [(a, (CHUNK, KV_DIM), lambda c, d=d: (jnp.maximum(c - d, 0), 0)) for d in (2, 1, 0)]
        o_f = _attention(q[N_META:], sinks, [whole(mk)] + band(kf), [whole(mv)] + band(vf),
                         CHUNK, nchunk, True)
        return jnp.concatenate([o_m, o_f], axis=0)

    s0_p = jnp.zeros((1, V_HI, HEAD_DIM, LANES), F32)
    scan_tb_p = _pick_tile(mp, (144, 48))
    xp2, p_p, k_p, v_p, sfin_p, tail_p = _trunk(
        xp, lw, tm=tm_p, tm_mix=tm_mix_p, tb_prep=tb_prep_p, prev_rows_fn=prev_rows_prompt,
        scan_tb=scan_tb_p, scan_nblk=mp // scan_tb_p, s0=s0_p, cos=cos_p, sin=sin_p,
        attend=attend_prompt, ffn_tm=tm_mix_p, ffn_seq_len=None, conv_prev=None)

    ms = nb * tn
    xs = x_sample.reshape(ms, D_MODEL)
    pos_s = N_META + PAST_LEN + jnp.arange(tn, dtype=jnp.int32)
    cos_s, sin_s = _rope_tables(jnp.tile(pos_s, nb))
    nwin = cache_win_k.shape[2]
    cmk = cache_meta_k[l].reshape(nb, N_META, KV_DIM)
    cmv = cache_meta_v[l].reshape(nb, N_META, KV_DIM)
    cwk = cache_win_k[l].reshape(nb, nwin, KV_DIM)
    cwv = cache_win_v[l].reshape(nb, nwin, KV_DIM)

    def prev_rows_sample(p_all):
        return _pack_rwkv_row(state_shift[l].reshape(nb, RWKV_PROJ))[:, None, :]

    def attend_sample(q, k, v):
        per_b = lambda a, n: (a, (None, n, KV_DIM), lambda b: (b, 0, 0))
        own = lambda a: (a, (tn, KV_DIM), lambda b: (b, 0))
        return _attention(q, sinks, [per_b(cmk, N_META), per_b(cwk, nwin), own(k)],
                          [per_b(cmv, N_META), per_b(cwv, nwin), own(v)], tn, nb, False)

    xs2, p_s, k_s, v_s, sfin_s, tail_s = _trunk(
        xs, lw, tm=ms, tm_mix=ms, tb_prep=tn, prev_rows_fn=prev_rows_sample, scan_tb=tn, scan_nblk=1,
        s0=_state_to_scan(state_rwkv[l]), cos=cos_s, sin=sin_s, attend=attend_sample,
        ffn_tm=ms, ffn_seq_len=tn, conv_prev=state_ffn_conv[l])

    kv5 = lambda a, n: a.reshape(1, -1, n, KV_HEADS, HEAD_DIM)
    y_prompt = xp2[N_META:][None]
    y_sample = xs2.reshape(nb, tn, D_MODEL)
    prompt_meta_k = kv5(k_p[:N_META], N_META)
    prompt_meta_v = kv5(v_p[:N_META], N_META)
    prompt_win_k = kv5(k_p[mp - WINDOW:], WINDOW)
    prompt_win_v = kv5(v_p[mp - WINDOW:], WINDOW)
    prompt_rwkv = _state_from_scan(sfin_p)[None]
    prompt_shift = _unpack_rwkv_row(p_p[mp - 1:mp]).reshape(1, 1, 1, RWKV_PROJ)
    prompt_ffn_conv = tail_p[-1, SUBLANES - (FFN_CONV - 1):].reshape(1, 1, FFN_CONV - 1, D_FF)
    sample_win_k = kv5(k_s, tn)
    sample_win_v = kv5(v_s, tn)
    sample_rwkv = _state_from_scan(sfin_s)[None]
    sample_shift = _unpack_rwkv_row(p_s[tn - 1::tn]).reshape(1, nb, 1, RWKV_PROJ)
    sample_ffn_conv = tail_s[:, SUBLANES - (FFN_CONV - 1):].reshape(1, nb, FFN_CONV - 1, D_FF)
    return (y_prompt, y_sample, prompt_meta_k, prompt_meta_v, prompt_win_k, prompt_win_v,
            prompt_rwkv, prompt_shift, prompt_ffn_conv, sample_win_k, sample_win_v,
            sample_rwkv, sample_shift, sample_ffn_conv)
```

```python
import functools
import math

import numpy as np
import jax
import jax.numpy as jnp
from jax import lax
from jax.experimental import pallas as pl
from jax.experimental.pallas import tpu as pltpu

F32 = jnp.float32
BF16 = jnp.bfloat16

D_MODEL = 2048
CHUNK = 64
N_META = 16
NORM_EPS = 1e-6
HEADS = 16
HEAD_DIM = 64
RWKV_DIM = HEADS * HEAD_DIM
DECAY_LORA = 64
ICLR_LORA = 64
GATE_LORA = 160
RWKV_PROJ = 3 * RWKV_DIM + DECAY_LORA + ICLR_LORA + GATE_LORA
RWKV_LN_EPS = 64e-5
KV_HEADS = 2
GROUP = HEADS // KV_HEADS
Q_DIM = HEADS * HEAD_DIM
KV_DIM = KV_HEADS * HEAD_DIM
WINDOW = 128
ROPE_THETA = 10000.0
D_FF = 3 * D_MODEL
FFN_CONV = 3
PAST_LEN = 4096

SUBLANES = 8
LANES = 128
V_LO = LANES // HEADS
V_HI = HEAD_DIM // V_LO
K_BLOCKS = HEAD_DIM // SUBLANES

COL_R, COL_K, COL_V, COL_Q = 0, 1024, 2048, 3072
COL_GA, COL_GB = 4096, 6144
COL_LORA = 8192
LORA_W = 512
LORA_XW, LORA_XA, LORA_XG = 0, 128, 256
COL_KV = 8704
N_PACK = 8960
RWKV_PACK = 3 * RWKV_DIM + LORA_W

VMEM_LIMIT = 56 * 1024 * 1024


def _scan_order():
    p = np.arange(RWKV_DIM)
    vh, h, vl = p // LANES, (p % LANES) // V_LO, p % V_LO
    return h * HEAD_DIM + vh * V_LO + vl


SCAN_ORDER = _scan_order()
NATURAL_ORDER = np.argsort(SCAN_ORDER)


def _cparams(sem):
    return pltpu.CompilerParams(dimension_semantics=sem, vmem_limit_bytes=VMEM_LIMIT)


def _rms(x, g):
    return x * lax.rsqrt(jnp.mean(x * x, axis=-1, keepdims=True) + NORM_EPS) * g


def _split3(x):
    hi = x.astype(BF16)
    r1 = x - hi.astype(F32)
    mid = r1.astype(BF16)
    lo = (r1 - mid.astype(F32)).astype(BF16)
    return hi, mid, lo


def _dot3(x, mat):
    hi, mid, lo = _split3(x)
    acc = jnp.dot(hi, mat, preferred_element_type=F32)
    acc = acc + jnp.dot(mid, mat, preferred_element_type=F32)
    return acc + jnp.dot(lo, mat, preferred_element_type=F32)


def _seg_sum(x, ones_bd):
    hi, mid, lo = _split3(x)
    outs = []
    for j in range(x.shape[1] // LANES):
        sl = slice(j * LANES, (j + 1) * LANES)
        acc = jnp.dot(hi[:, sl], ones_bd, preferred_element_type=F32)
        acc = acc + jnp.dot(mid[:, sl], ones_bd, preferred_element_type=F32)
        acc = acc + jnp.dot(lo[:, sl], ones_bd, preferred_element_type=F32)
        outs.append(acc)
    return jnp.concatenate(outs, axis=1)


def _tile_lanes(x, n):
    return jnp.concatenate([x] * n, axis=1)


def _norm_matmul_kernel(x_ref, g_ref, w_ref, o_ref, h_ref):
    @pl.when(pl.program_id(1) == 0)
    def _():
        h_ref[...] = _rms(x_ref[...], g_ref[...]).astype(BF16)

    o_ref[...] = jnp.dot(h_ref[...], w_ref[...], preferred_element_type=F32)


def _norm_matmul(x, g, w, tm, tn):
    m, d = x.shape
    n = w.shape[1]
    return pl.pallas_call(
        _norm_matmul_kernel,
        grid=(m // tm, n // tn),
        in_specs=[pl.BlockSpec((tm, d), lambda i, j: (i, 0)),
                  pl.BlockSpec((1, d), lambda i, j: (0, 0)),
                  pl.BlockSpec((d, tn), lambda i, j: (0, j))],
        out_specs=pl.BlockSpec((tm, tn), lambda i, j: (i, j)),
        out_shape=jax.ShapeDtypeStruct((m, n), F32),
        scratch_shapes=[pltpu.VMEM((tm, d), BF16)],
        compiler_params=_cparams(("parallel", "arbitrary")),
        name="norm_matmul",
    )(x, g, w)


def _store_head_pairs(ref, a, b):
    lane = lax.broadcasted_iota(jnp.int32, (a.shape[0], LANES), 1)
    low = lane < HEAD_DIM
    for j in range(a.shape[1] // LANES):
        sl = slice(j * LANES, (j + 1) * LANES)
        aj, bj = a[:, sl], b[:, sl]
        ref[2 * j] = jnp.where(low, aj, pltpu.roll(bj, HEAD_DIM, 1))
        ref[2 * j + 1] = jnp.where(low, pltpu.roll(aj, HEAD_DIM, 1), bj)


def _rwkv_prep_kernel(r_ref, k_ref, v_ref, l_ref, prev_ref, mu_ref, w0_ref, a0_ref, kk_ref, ka_ref,
                      rk_ref, ww_ref, wa_ref, wg_ref, ones_ref, hsum_ref,
                      ab_ref, wk_ref, rr_ref, vo_ref, g_ref, bonus_ref):
    rows = r_ref.shape[0]
    first = lax.broadcasted_iota(jnp.int32, (rows, 1), 0) == 0

    def shifted(x, lo, hi):
        prev = jnp.where(first, prev_ref[:, lo:hi], pltpu.roll(x, 1, 0))
        return x + (prev - x) * mu_ref[:, lo:hi]

    r = shifted(r_ref[...], 0, RWKV_DIM)
    k = shifted(k_ref[...], RWKV_DIM, 2 * RWKV_DIM)
    v = shifted(v_ref[...], 2 * RWKV_DIM, 3 * RWKV_DIM)
    lora = shifted(l_ref[...], 3 * RWKV_DIM, RWKV_PACK)
    xw = lora[:, LORA_XW:LORA_XW + LANES]
    xa = lora[:, LORA_XA:LORA_XA + LANES]
    xg = lora[:, LORA_XG:LORA_XG + 2 * LANES]

    w_pre = w0_ref[...] + jnp.dot(jnp.tanh(xw).astype(BF16), ww_ref[...], preferred_element_type=F32)
    decay = jnp.exp(-math.exp(-0.5) * jax.nn.sigmoid(w_pre))
    a = jax.nn.sigmoid(a0_ref[...] + jnp.dot(xa.astype(BF16), wa_ref[...], preferred_element_type=F32))
    g = jnp.dot(jax.nn.sigmoid(xg).astype(BF16), wg_ref[...], preferred_element_type=F32)

    kk = k * kk_ref[...]
    kk = kk * lax.rsqrt(jnp.maximum(_seg_sum(kk * kk, ones_ref[...]), 1e-24))
    k_mod = k * (1.0 + (a - 1.0) * ka_ref[...])
    rk_sum = _dot3(r * k_mod * rk_ref[...], hsum_ref[...])

    _store_head_pairs(ab_ref, -kk, kk * a)
    _store_head_pairs(wk_ref, decay, k_mod)
    _store_head_pairs(rr_ref, r, r)
    for vh in range(V_HI):
        vo_ref[vh] = v[:, vh * LANES:(vh + 1) * LANES]
    g_ref[...] = g
    bonus_ref[...] = _tile_lanes(rk_sum, V_HI) * v


def _rwkv_prep(p_all, prev_rows, pw, tb):
    m = p_all.shape[0]
    nt = m // tb
    row = lambda c: pl.BlockSpec((tb, RWKV_DIM), lambda i, c=c: (i, c))
    full = lambda a: pl.BlockSpec(a.shape, lambda i: (0,) * a.ndim)
    params = (pw["mu"], pw["w0"], pw["a0"], pw["k_k"], pw["k_a"], pw["r_k"],
              pw["w_lora"], pw["a_lora"], pw["g_lora"], pw["ones_bd"], pw["hsum_nat"])
    per_head = jax.ShapeDtypeStruct((HEADS, m, LANES), F32)
    per_vhi = jax.ShapeDtypeStruct((V_HI, m, LANES), F32)
    narrow = jax.ShapeDtypeStruct((m, RWKV_DIM), F32)
    return pl.pallas_call(
        _rwkv_prep_kernel,
        grid=(nt,),
        in_specs=[row(COL_R // RWKV_DIM), row(COL_K // RWKV_DIM), row(COL_V // RWKV_DIM),
                  pl.BlockSpec((tb, LORA_W), lambda i: (i, COL_LORA // LORA_W)),
                  pl.BlockSpec((None, 1, RWKV_PACK), lambda i: (i, 0, 0))]
                 + [full(a) for a in params],
        out_specs=[pl.BlockSpec((HEADS, tb, LANES), lambda i: (0, i, 0))] * 3
                  + [pl.BlockSpec((V_HI, tb, LANES), lambda i: (0, i, 0))]
                  + [pl.BlockSpec((tb, RWKV_DIM), lambda i: (i, 0))] * 2,
        out_shape=[per_head, per_head, per_head, per_vhi, narrow, narrow],
        compiler_params=_cparams(("parallel",)),
        name="rwkv_prep",
    )(p_all, p_all, p_all, p_all, prev_rows, *params)


def _sublane_allsum(x):
    x = x + pltpu.roll(x, 4, 0)
    x = x + pltpu.roll(x, 2, 0)
    return x + pltpu.roll(x, 1, 0)


def _scan_kernel(ab_ref, wk_ref, rr_ref, v_ref, s0_ref, y_ref, st_ref, ops_ref, *, tb, nblk):
    @pl.when(pl.program_id(0) % nblk == 0)
    def _():
        st_ref[...] = s0_ref[...]

    def col_tiles(ref, t):
        rep = jnp.concatenate(
            [jnp.broadcast_to(ref[h, pl.ds(t, 1), :], (V_LO, LANES)) for h in range(HEADS)], axis=0)
        return rep.T

    def stage(t, slot):
        ab = col_tiles(ab_ref, t)
        wk = col_tiles(wk_ref, t)
        rr = col_tiles(rr_ref, t)
        ops_ref[slot, 0] = ab[:HEAD_DIM]
        ops_ref[slot, 1] = ab[HEAD_DIM:]
        ops_ref[slot, 2] = wk[:HEAD_DIM]
        ops_ref[slot, 3] = wk[HEAD_DIM:]
        ops_ref[slot, 4] = rr[:HEAD_DIM]

    def op_tile(slot, which, kb):
        return ops_ref[slot, which, kb * SUBLANES:(kb + 1) * SUBLANES, :]

    def step(t, slot):
        ksl = lambda kb: slice(kb * SUBLANES, (kb + 1) * SUBLANES)
        acc = [None] * V_HI
        for kb in range(K_BLOCKS):
            alpha = op_tile(slot, 0, kb)
            for vh in range(V_HI):
                p = st_ref[vh, ksl(kb), :] * alpha
                acc[vh] = p if acc[vh] is None else acc[vh] + p
        u = [_sublane_allsum(acc[vh]) for vh in range(V_HI)]
        vb = [jnp.broadcast_to(v_ref[vh, pl.ds(t, 1), :], (SUBLANES, LANES)) for vh in range(V_HI)]
        yacc = [None] * V_HI
        for kb in range(K_BLOCKS):
            beta = op_tile(slot, 1, kb)
            decay = op_tile(slot, 2, kb)
            kmod = op_tile(slot, 3, kb)
            rr = op_tile(slot, 4, kb)
            for vh in range(V_HI):
                s_new = st_ref[vh, ksl(kb), :] * decay + beta * u[vh] + kmod * vb[vh]
                st_ref[vh, ksl(kb), :] = s_new
                p = s_new * rr
                yacc[vh] = p if yacc[vh] is None else yacc[vh] + p
        for vh in range(V_HI):
            y_ref[vh, pl.ds(t, 1), :] = _sublane_allsum(yacc[vh])[0:1, :]

    stage(0, 0)

    def body(i, carry):
        base = pl.multiple_of(i * SUBLANES, SUBLANES)
        following = pl.multiple_of(jnp.minimum(base + SUBLANES, tb - SUBLANES), SUBLANES)
        for j in range(SUBLANES):
            nxt = base + (j + 1) if j + 1 < SUBLANES else following
            stage(nxt, (j + 1) % 2)
            step(base + j, j % 2)
        return carry

    lax.fori_loop(0, tb // SUBLANES, body, 0)


def _rwkv_scan(ab, wk, rr, v, s0, tb, nblk):
    t_total = v.shape[1]
    nseq = s0.shape[0]
    grid = t_total // tb
    assert grid == nseq * nblk and tb % SUBLANES == 0
    st_shape = (V_HI, HEAD_DIM, LANES)
    per_head = pl.BlockSpec((HEADS, tb, LANES), lambda i: (0, i, 0))
    per_vhi = pl.BlockSpec((V_HI, tb, LANES), lambda i: (0, i, 0))
    state = pl.BlockSpec((None,) + st_shape, lambda i: (i // nblk, 0, 0, 0))
    return pl.pallas_call(
        functools.partial(_scan_kernel, tb=tb, nblk=nblk),
        grid=(grid,),
        in_specs=[per_head, per_head, per_head, per_vhi, state],
        out_specs=[per_vhi, state],
        out_shape=[jax.ShapeDtypeStruct((V_HI, t_total, LANES), F32),
                   jax.ShapeDtypeStruct((nseq,) + st_shape, F32)],
        scratch_shapes=[pltpu.VMEM((2, 5, HEAD_DIM, LANES), F32)],
        compiler_params=_cparams(("arbitrary",)),
        name="rwkv_scan",
    )(ab, wk, rr, v, s0)


def _state_to_scan(s):
    b = s.shape[0]
    s = s.reshape(b, HEADS, V_HI, V_LO, HEAD_DIM)
    return s.transpose(0, 2, 4, 1, 3).reshape(b, V_HI, HEAD_DIM, LANES)


def _state_from_scan(st):
    b = st.shape[0]
    st = st.reshape(b, V_HI, HEAD_DIM, HEADS, V_LO)
    return st.transpose(0, 3, 1, 4, 2).reshape(b, HEADS, HEAD_DIM, HEAD_DIM)


def _rope_tile(x, cos, sin_signed):
    lane = lax.broadcasted_iota(jnp.int32, x.shape, 1)
    first_half = (lane % HEAD_DIM) < (HEAD_DIM // 2)
    rot = jnp.where(first_half, pltpu.roll(x, LANES - HEAD_DIM // 2, 1), pltpu.roll(x, HEAD_DIM // 2, 1))
    return x * cos + rot * sin_signed


def _rope_kernel(q_ref, kv_ref, cos_ref, sin_ref, qo_ref, ko_ref):
    cos = cos_ref[...]
    sin = sin_ref[...]
    for j in range(Q_DIM // LANES):
        sl = slice(j * LANES, (j + 1) * LANES)
        qo_ref[:, sl] = _rope_tile(q_ref[:, sl], cos, sin).astype(BF16)
    ko_ref[...] = _rope_tile(kv_ref[:, :KV_DIM], cos, sin)


def _rope(p_all, cos, sin, tb):
    m = p_all.shape[0]
    return pl.pallas_call(
        _rope_kernel,
        grid=(m // tb,),
        in_specs=[pl.BlockSpec((tb, Q_DIM), lambda i: (i, COL_Q // Q_DIM)),
                  pl.BlockSpec((tb, 2 * KV_DIM), lambda i: (i, COL_KV // (2 * KV_DIM))),
                  pl.BlockSpec((tb, LANES), lambda i: (i, 0)),
                  pl.BlockSpec((tb, LANES), lambda i: (i, 0))],
        out_specs=[pl.BlockSpec((tb, Q_DIM), lambda i: (i, 0)),
                   pl.BlockSpec((tb, KV_DIM), lambda i: (i, 0))],
        out_shape=[jax.ShapeDtypeStruct((m, Q_DIM), BF16),
                   jax.ShapeDtypeStruct((m, KV_DIM), F32)],
        compiler_params=_cparams(("parallel",)),
        name="rope",
    )(p_all, p_all, cos, sin)


def _rope_tables(pos):
    half = HEAD_DIM // 2
    inv_freq = ROPE_THETA ** (-jnp.arange(half, dtype=F32) / half)
    ang = pos.astype(F32)[:, None] * inv_freq[None, :]
    cos = jnp.cos(ang)
    sin = jnp.sin(ang)
    cos = jnp.concatenate([cos, cos, cos, cos], axis=1)
    sin = jnp.concatenate([-sin, sin, -sin, sin], axis=1)
    return cos, sin


PAIRS = GROUP // 2


def _attn_kernel(*refs, nseg, band_mask):
    q_ref, sink_ref = refs[0], refs[1]
    k_refs = refs[2:2 + nseg]
    v_refs = refs[2 + nseg:2 + 2 * nseg]
    o_ref = refs[2 + 2 * nseg]
    tq = q_ref.shape[0]

    k_all = jnp.concatenate([r[...] for r in k_refs], axis=0)
    v_all = jnp.concatenate([r[...] for r in v_refs], axis=0)
    nkeys = k_all.shape[0]
    low = lax.broadcasted_iota(jnp.int32, (nkeys, LANES), 1) < HEAD_DIM
    k_swap = pltpu.roll(k_all, HEAD_DIM, 1)
    v_swap = pltpu.roll(v_all, HEAD_DIM, 1)
    if band_mask:
        c = pl.program_id(0)
        col = lax.broadcasted_iota(jnp.int32, (1, nkeys), 1)
        first_valid = N_META + jnp.maximum(2 - c, 0) * CHUNK
        visible = (col < N_META) | (col >= first_valid)
        bias = jnp.where(visible, 0.0, -jnp.inf).astype(F32)
    scale = HEAD_DIM ** -0.5
    groups = []
    for kvh in range(KV_HEADS):
        k_src, k_oth = (k_all, k_swap) if kvh == 0 else (k_swap, k_all)
        v_src, v_oth = (v_all, v_swap) if kvh == 0 else (v_swap, v_all)
        halves = ((jnp.where(low, k_src, 0.0).astype(BF16), jnp.where(low, v_src, 0.0).astype(BF16)),
                  (jnp.where(low, 0.0, k_oth).astype(BF16), jnp.where(low, 0.0, v_oth).astype(BF16)))
        qp = jnp.concatenate(
            [q_ref[:, (kvh * PAIRS + p) * LANES:(kvh * PAIRS + p + 1) * LANES] for p in range(PAIRS)], axis=0)
        total = None
        for parity, (kx, vx) in enumerate(halves):
            sink = jnp.concatenate(
                [jnp.broadcast_to(sink_ref[:, kvh * GROUP + 2 * p + parity:kvh * GROUP + 2 * p + parity + 1],
                                  (tq, 1)) for p in range(PAIRS)], axis=0)
            s = lax.dot_general(qp, kx, (((1,), (1,)), ((), ())), preferred_element_type=F32) * scale
            if band_mask:
                s = s + bias
            m = jnp.maximum(jnp.max(s, axis=1, keepdims=True), sink)
            e = jnp.exp(s - m)
            den = jnp.sum(e, axis=1, keepdims=True) + jnp.exp(sink - m)
            o = jnp.dot(e.astype(BF16), vx, preferred_element_type=F32) * (1.0 / den)
            total = o if total is None else total + o
        groups.extend(total[p * tq:(p + 1) * tq] for p in range(PAIRS))
    o_ref[...] = jnp.concatenate(groups, axis=1).astype(BF16)


def _attention(q, sinks, k_segs, v_segs, tq, grid, band_mask):
    nseg = len(k_segs)
    seg_specs = [pl.BlockSpec(bs, im) for (_, bs, im) in k_segs + v_segs]
    return pl.pallas_call(
        functools.partial(_attn_kernel, nseg=nseg, band_mask=band_mask),
        grid=(grid,),
        in_specs=[pl.BlockSpec((tq, Q_DIM), lambda i: (i, 0)),
                  pl.BlockSpec((1, HEADS), lambda i: (0, 0))] + seg_specs,
        out_specs=pl.BlockSpec((tq, Q_DIM), lambda i: (i, 0)),
        out_shape=jax.ShapeDtypeStruct((grid * tq, Q_DIM), BF16),
        compiler_params=_cparams(("parallel",)),
        name="attention",
    )(q, sinks, *[a for (a, _, _) in k_segs + v_segs])


def _mix_kernel(y_ref, bonus_ref, g_ref, yb_ref, ga_ref, gb_ref, lng_ref, lnb_ref, hsum_ref,
                wa_ref, wb_ref, o_ref, ya_ref):
    @pl.when(pl.program_id(1) == 0)
    def _():
        hsum = hsum_ref[...]
        y = jnp.concatenate([y_ref[vh] for vh in range(V_HI)], axis=1)
        mean = _dot3(y, hsum) * (1.0 / HEAD_DIM)
        d = y - _tile_lanes(mean, V_HI)
        var = _dot3(d * d, hsum) * (1.0 / HEAD_DIM)
        yn = d * _tile_lanes(lax.rsqrt(var + RWKV_LN_EPS), V_HI) * lng_ref[...] + lnb_ref[...]
        ya_ref[...] = ((yn + bonus_ref[...]) * g_ref[...]).astype(BF16)

    pa = jnp.dot(ya_ref[...], wa_ref[...], preferred_element_type=F32)
    pb = jnp.dot(yb_ref[...], wb_ref[...], preferred_element_type=F32)
    o_ref[...] = (jax.nn.sigmoid(ga_ref[...]) * pa + jax.nn.sigmoid(gb_ref[...]) * pb).astype(BF16)


def _mix(y, bonus, g, yb, p_all, ln_g, ln_b, hsum, wa, wb, tm):
    m = bonus.shape[0]
    tn = 1024
    row = pl.BlockSpec((tm, RWKV_DIM), lambda i, j: (i, 0))
    one = lambda a: pl.BlockSpec(a.shape, lambda i, j: (0,) * a.ndim)
    return pl.pallas_call(
        _mix_kernel,
        grid=(m // tm, D_MODEL // tn),
        in_specs=[pl.BlockSpec((V_HI, tm, LANES), lambda i, j: (0, i, 0)), row, row, row,
                  pl.BlockSpec((tm, tn), lambda i, j: (i, COL_GA // tn + j)),
                  pl.BlockSpec((tm, tn), lambda i, j: (i, COL_GB // tn + j)),
                  one(ln_g), one(ln_b), one(hsum),
                  pl.BlockSpec((RWKV_DIM, tn), lambda i, j: (0, j)),
                  pl.BlockSpec((Q_DIM, tn), lambda i, j: (0, j))],
        out_specs=pl.BlockSpec((tm, tn), lambda i, j: (i, j)),
        out_shape=jax.ShapeDtypeStruct((m, D_MODEL), BF16),
        scratch_shapes=[pltpu.VMEM((tm, RWKV_DIM), BF16)],
        compiler_params=_cparams(("parallel", "arbitrary")),
        name="branch_mix",
    )(y, bonus, g, yb, p_all, p_all, ln_g, ln_b, hsum, wa, wb)


def _out_norm_kernel(mix_ref, w_ref, x_ref, g_ref, o_ref):
    z = jnp.dot(mix_ref[...], w_ref[...], preferred_element_type=F32)
    o_ref[...] = x_ref[...] + _rms(z, g_ref[...])


def _out_norm(mixed, w, x, g, tm):
    m = x.shape[0]
    return pl.pallas_call(
        _out_norm_kernel,
        grid=(m // tm,),
        in_specs=[pl.BlockSpec((tm, D_MODEL), lambda i: (i, 0)),
                  pl.BlockSpec((D_MODEL, D_MODEL), lambda i: (0, 0)),
                  pl.BlockSpec((tm, D_MODEL), lambda i: (i, 0)),
                  pl.BlockSpec((1, D_MODEL), lambda i: (0, 0))],
        out_specs=pl.BlockSpec((tm, D_MODEL), lambda i: (i, 0)),
        out_shape=jax.ShapeDtypeStruct((m, D_MODEL), F32),
        compiler_params=_cparams(("parallel",)),
        name="out_norm",
    )(mixed, w, x, g)


HALO = 16


def _gelu_tanh(x):
    return 0.5 * x * (1.0 + jnp.tanh(0.7978845608028654 * (x + 0.044715 * x * x * x)))


def _ffn_kernel(*refs, seq_len):
    if seq_len is None:
        (x_ref, halo_ref, gpre_ref, wg_ref, wv_ref, wd_ref, cw_ref, cb_ref, gpost_ref,
         o_ref, tail_ref, h_ref, acc_ref) = refs
    else:
        (x_ref, sa_ref, sb_ref, gpre_ref, wg_ref, wv_ref, wd_ref, cw_ref, cb_ref, gpost_ref,
         o_ref, tail_ref, h_ref, acc_ref) = refs
    tm = x_ref.shape[0]
    j = pl.program_id(1)
    top = HALO if seq_len is None else 0

    @pl.when(j == 0)
    def _():
        if seq_len is None:
            halo = _rms(halo_ref[...], gpre_ref[...])
            halo = jnp.where(pl.program_id(0) == 0, 0.0, halo)
            h_ref[:HALO] = halo.astype(BF16)
        h_ref[top:] = _rms(x_ref[...], gpre_ref[...]).astype(BF16)
        acc_ref[...] = jnp.zeros_like(acc_ref)

    h_all = h_ref[...]
    gate_all = jnp.dot(h_all, wg_ref[...], preferred_element_type=F32)
    val = jnp.dot(h_all[top:], wv_ref[...], preferred_element_type=F32)
    g0 = gate_all[top:]
    g1 = pltpu.roll(gate_all, 1, 0)[top:]
    g2 = pltpu.roll(gate_all, 2, 0)[top:]
    if seq_len is not None:
        nseq = tm // seq_len
        tf = g0.shape[1]
        tau = lax.broadcasted_iota(jnp.int32, (tm, 1), 0) % seq_len
        expand = lambda s: jnp.concatenate(
            [jnp.broadcast_to(s[b:b + 1, :], (seq_len, tf)) for b in range(nseq)], axis=0)
        st_a = expand(sa_ref[...])
        st_b = expand(sb_ref[...])
        g1 = jnp.where(tau == 0, st_b, g1)
        g2 = jnp.where(tau == 0, st_a, jnp.where(tau == 1, st_b, g2))
    conv = cb_ref[...] + g2 * cw_ref[0:1, :] + g1 * cw_ref[1:2, :] + g0 * cw_ref[2:3, :]
    act = (_gelu_tanh(conv) * val).astype(BF16)
    acc_ref[...] += jnp.dot(act, wd_ref[...], preferred_element_type=F32)

    if seq_len is None:
        tail_ref[...] = g0[tm - SUBLANES:]
    else:
        for b in range(tm // seq_len):
            tail_ref[b] = g0[(b + 1) * seq_len - SUBLANES:(b + 1) * seq_len]

    @pl.when(j == pl.num_programs(1) - 1)
    def _():
        o_ref[...] = x_ref[...] + _rms(acc_ref[...], gpost_ref[...])


def _ffn(x1, conv_prev, fw, tm, tf, seq_len):
    m = x1.shape[0]
    nm, nf = m // tm, D_FF // tf
    one = lambda a: pl.BlockSpec(a.shape, lambda i, j: (0,) * a.ndim)
    xspec = pl.BlockSpec((tm, D_MODEL), lambda i, j: (i, 0))
    wspecs = [pl.BlockSpec((D_MODEL, tf), lambda i, j: (0, j)),
              pl.BlockSpec((D_MODEL, tf), lambda i, j: (0, D_FF // tf + j)),
              pl.BlockSpec((tf, D_MODEL), lambda i, j: (j, 0)),
              pl.BlockSpec((FFN_CONV, tf), lambda i, j: (0, j)),
              pl.BlockSpec((1, tf), lambda i, j: (0, j)),
              one(fw["g_post"])]
    wargs = (fw["w_up"], fw["w_up"], fw["w_down"], fw["conv_w"], fw["conv_b"], fw["g_post"])
    if seq_len is None:
        assert tm % HALO == 0
        pre_specs = [xspec,
                     pl.BlockSpec((HALO, D_MODEL), lambda i, j: (jnp.maximum(i * (tm // HALO) - 1, 0), 0)),
                     one(fw["g_pre"])]
        pre_args = (x1, x1, fw["g_pre"])
        tail_shape = (nm, SUBLANES, D_FF)
        tail_spec = pl.BlockSpec((None, SUBLANES, tf), lambda i, j: (i, 0, j))
        hrows = tm + HALO
    else:
        nseq = tm // seq_len
        assert nm == 1
        pre_specs = [xspec,
                     pl.BlockSpec((nseq, tf), lambda i, j: (0, j)),
                     pl.BlockSpec((nseq, tf), lambda i, j: (0, j)),
                     one(fw["g_pre"])]
        pre_args = (x1, conv_prev[:, 0, :], conv_prev[:, 1, :], fw["g_pre"])
        tail_shape = (nseq, SUBLANES, D_FF)
        tail_spec = pl.BlockSpec((nseq, SUBLANES, tf), lambda i, j: (0, 0, j))
        hrows = tm
    return pl.pallas_call(
        functools.partial(_ffn_kernel, seq_len=seq_len),
        grid=(nm, nf),
        in_specs=pre_specs + wspecs,
        out_specs=[pl.BlockSpec((tm, D_MODEL), lambda i, j: (i, 0)), tail_spec],
        out_shape=[jax.ShapeDtypeStruct((m, D_MODEL), F32),
                   jax.ShapeDtypeStruct(tail_shape, F32)],
        scratch_shapes=[pltpu.VMEM((hrows, D_MODEL), BF16), pltpu.VMEM((tm, D_MODEL), F32)],
        compiler_params=_cparams(("parallel", "arbitrary")),
        name="conv_ffn",
    )(*pre_args, *wargs)


def _pack_lora(x):
    z = lambda n: jnp.zeros((x.shape[0], n), x.dtype)
    xw = x[:, :DECAY_LORA]
    xa = x[:, DECAY_LORA:DECAY_LORA + ICLR_LORA]
    xg = x[:, DECAY_LORA + ICLR_LORA:]
    return jnp.concatenate([xw, z(LORA_XA - DECAY_LORA), xa, z(LORA_XG - LORA_XA - ICLR_LORA),
                            xg, z(LORA_W - LORA_XG - GATE_LORA)], axis=1)


def _unpack_lora(x):
    return jnp.concatenate([x[:, LORA_XW:LORA_XW + DECAY_LORA], x[:, LORA_XA:LORA_XA + ICLR_LORA],
                            x[:, LORA_XG:LORA_XG + GATE_LORA]], axis=1)


def _pack_cols(w):
    c = 3 * RWKV_DIM
    lora = w[:, c:RWKV_PROJ]
    q0 = RWKV_PROJ
    kv0 = q0 + Q_DIM
    g0 = kv0 + 2 * KV_DIM
    return jnp.concatenate([w[:, :2 * RWKV_DIM], w[:, 2 * RWKV_DIM:c][:, SCAN_ORDER], w[:, q0:kv0],
                            w[:, g0:g0 + 2 * D_MODEL], _pack_lora(lora), w[:, kv0:g0]], axis=1)


def _pack_rwkv_row(x):
    c = 3 * RWKV_DIM
    return jnp.concatenate([x[:, :2 * RWKV_DIM], x[:, 2 * RWKV_DIM:c][:, SCAN_ORDER], _pack_lora(x[:, c:])],
                           axis=1)


def _unpack_rwkv_row(p_row):
    return jnp.concatenate([p_row[:, :2 * RWKV_DIM], p_row[:, COL_V:COL_V + RWKV_DIM][:, NATURAL_ORDER],
                            _unpack_lora(p_row[:, COL_LORA:COL_LORA + LORA_W])], axis=1)


def _pick_tile(m, candidates):
    for c in candidates:
        if m % c == 0:
            return c
    raise ValueError(f"no row tile in {candidates} divides {m}")


def _pad_rows(w, rows):
    return jnp.concatenate([w, jnp.zeros((rows - w.shape[0], w.shape[1]), w.dtype)], axis=0)


def _trunk(x, lw, *, tm, tm_mix, tb_prep, prev_rows_fn, scan_tb, scan_nblk, s0, cos, sin, attend, ffn_tm,
           ffn_seq_len, conv_prev):
    p_all = _norm_matmul(x, lw["g_pre_mix"], lw["w_in"], tm, 1280)

    ab, wk, rr, v_scan, g, bonus = _rwkv_prep(p_all, prev_rows_fn(p_all), lw, tb_prep)
    y, s_fin = _rwkv_scan(ab, wk, rr, v_scan, s0, scan_tb, scan_nblk)

    q_roped, k_roped = _rope(p_all, cos, sin, tm)
    v_att = p_all[:, COL_KV + KV_DIM:COL_KV + 2 * KV_DIM]
    att = attend(q_roped, k_roped, v_att)

    mixed = _mix(y, bonus, g, att, p_all, lw["ln_g"], lw["ln_b"], lw["hsum_scan"],
                 lw["w_branch_a"], lw["w_branch_b"], tm_mix)
    x1 = _out_norm(mixed, lw["w_out"], x, lw["g_post_mix"], tm_mix)
    x2, tail = _ffn(x1, conv_prev, lw["ffn"], ffn_tm, 512, ffn_seq_len)
    return x2, p_all, k_roped, v_att, s_fin, tail


def kernel(x_prompt, x_sample, cache_meta_k, cache_meta_v, cache_win_k, cache_win_v, state_rwkv, state_shift, state_ffn_conv, meta_tokens, g_pre_mix, w_in, rwkv_mu, rwkv_w0, rwkv_w_lora_b, rwkv_a0, rwkv_a_lora_b, rwkv_g_lora_b, rwkv_k_k, rwkv_k_a, rwkv_r_k, rwkv_ln_g, rwkv_ln_b, attn_sinks, w_branch_a, w_branch_b, w_out, g_post_mix, g_pre_ffn, w_up, ffn_conv_w, ffn_conv_b, w_down, g_post_ffn):
    depth = w_in.shape[0]
    assert depth == 1
    l = 0
    seq = x_prompt.shape[1]
    nb, tn = x_sample.shape[0], x_sample.shape[1]
    row = lambda a: a.reshape(1, -1)
    lane = np.arange(LANES)
    chan = np.arange(RWKV_DIM)
    ones_bd = jnp.asarray(lane[:, None] // HEAD_DIM == lane[None, :] // HEAD_DIM, BF16)
    hsum_nat = jnp.asarray((chan // HEAD_DIM)[:, None] == (lane // V_LO)[None, :], BF16)
    hsum_scan = jnp.asarray(((chan % LANES) // V_LO)[:, None] == (lane // V_LO)[None, :], BF16)
    lw = dict(
        g_pre_mix=row(g_pre_mix[l]),
        w_in=_pack_cols(w_in[l].astype(BF16)),
        mu=_pack_rwkv_row(row(rwkv_mu[l])),
        w0=row(rwkv_w0[l]), a0=row(rwkv_a0[l]), k_k=row(rwkv_k_k[l]), k_a=row(rwkv_k_a[l]),
        r_k=row(rwkv_r_k[l]),
        w_lora=_pad_rows(rwkv_w_lora_b[l], LANES).astype(BF16),
        a_lora=_pad_rows(rwkv_a_lora_b[l], LANES).astype(BF16),
        g_lora=_pad_rows(rwkv_g_lora_b[l][:, SCAN_ORDER], 2 * LANES).astype(BF16),
        ones_bd=ones_bd, hsum_nat=hsum_nat, hsum_scan=hsum_scan,
        ln_g=row(rwkv_ln_g[l][SCAN_ORDER]), ln_b=row(rwkv_ln_b[l][SCAN_ORDER]),
        w_branch_a=w_branch_a[l][SCAN_ORDER, :].astype(BF16), w_branch_b=w_branch_b[l].astype(BF16),
        w_out=w_out[l].astype(BF16), g_post_mix=row(g_post_mix[l]),
        ffn=dict(g_pre=row(g_pre_ffn[l]), w_up=w_up[l].astype(BF16), w_down=w_down[l].astype(BF16),
                 conv_w=ffn_conv_w[l], conv_b=row(ffn_conv_b[l]), g_post=row(g_post_ffn[l])),
    )
    sinks = row(attn_sinks[l])

    mp = N_META + seq
    xp = jnp.concatenate([meta_tokens.astype(x_prompt.dtype), x_prompt[0]], axis=0)
    nchunk = seq // CHUNK
    tm_p = _pick_tile(mp, (912, 144, 48))
    tm_mix_p = _pick_tile(mp, (432, 144, 48))
    tb_prep_p = _pick_tile(mp, (144, 48))
    cos_p, sin_p = _rope_tables(jnp.arange(mp, dtype=jnp.int32))

    def prev_rows_prompt(p_all):
        tails = p_all[tb_prep_p - 1:mp - 1:tb_prep_p]
        tails = jnp.concatenate([tails[:, :3 * RWKV_DIM], tails[:, COL_LORA:COL_LORA + LORA_W]], axis=1)
        zero = jnp.zeros((1, RWKV_PACK), F32)
        return jnp.concatenate([zero, tails], axis=0)[:, None, :]

    def attend_prompt(q, k, v):
        mk, mv = k[:N_META], v[:N_META]
        whole = lambda a: (a, (N_META, KV_DIM), lambda c: (0, 0))
        o_m = _attention(q[:N_META], sinks, [whole(mk)], [whole(mv)], N_META, 1, False)
        kf, vf = k[N_META:], v[N_META:]
        band = lambda a: [(a, (CHUNK, KV_DIM), lambda c, d=d: (jnp.maximum(c - d, 0), 0)) for d in (2, 1, 0)]
        o_f = _attention(q[N_META:], sinks, [whole(mk)] + band(kf), [whole(mv)] + band(vf),
                         CHUNK, nchunk, True)
        return jnp.concatenate([o_m, o_f], axis=0)

    s0_p = jnp.zeros((1, V_HI, HEAD_DIM, LANES), F32)
    scan_tb_p = _pick_tile(mp, (144, 48))
    xp2, p_p, k_p, v_p, sfin_p, tail_p = _trunk(
        xp, lw, tm=tm_p, tm_mix=tm_mix_p, tb_prep=tb_prep_p, prev_rows_fn=prev_rows_prompt,
        scan_tb=scan_tb_p, scan_nblk=mp // scan_tb_p, s0=s0_p, cos=cos_p, sin=sin_p,
        attend=attend_prompt, ffn_tm=tm_mix_p, ffn_seq_len=None, conv_prev=None)

    ms = nb * tn
    xs = x_sample.reshape(ms, D_MODEL)
    pos_s = N_META + PAST_LEN + jnp.arange(tn, dtype=jnp.int32)
    cos_s, sin_s = _rope_tables(jnp.tile(pos_s, nb))
    nwin = cache_win_k.shape[2]
    cmk = cache_meta_k[l].reshape(nb, N_META, KV_DIM)
    cmv = cache_meta_v[l].reshape(nb, N_META, KV_DIM)
    cwk = cache_win_k[l].reshape(nb, nwin, KV_DIM)
    cwv = cache_win_v[l].reshape(nb, nwin, KV_DIM)

    def prev_rows_sample(p_all):
        return _pack_rwkv_row(state_shift[l].reshape(nb, RWKV_PROJ))[:, None, :]

    def attend_sample(q, k, v):
        per_b = lambda a, n: (a, (None, n, KV_DIM), lambda b: (b, 0, 0))
        own = lambda a: (a, (tn, KV_DIM), lambda b: (b, 0))
        return _attention(q, sinks, [per_b(cmk, N_META), per_b(cwk, nwin), own(k)],
                          [per_b(cmv, N_META), per_b(cwv, nwin), own(v)], tn, nb, False)

    xs2, p_s, k_s, v_s, sfin_s, tail_s = _trunk(
        xs, lw, tm=ms, tm_mix=ms, tb_prep=tn, prev_rows_fn=prev_rows_sample, scan_tb=tn, scan_nblk=1,
        s0=_state_to_scan(state_rwkv[l]), cos=cos_s, sin=sin_s, attend=attend_sample,
        ffn_tm=ms, ffn_seq_len=tn, conv_prev=state_ffn_conv[l])

    kv5 = lambda a, n: a.reshape(1, -1, n, KV_HEADS, HEAD_DIM)
    y_prompt = xp2[N_META:][None]
    y_sample = xs2.reshape(nb, tn, D_MODEL)
    prompt_meta_k = kv5(k_p[:N_META], N_META)
    prompt_meta_v = kv5(v_p[:N_META], N_META)
    prompt_win_k = kv5(k_p[mp - WINDOW:], WINDOW)
    prompt_win_v = kv5(v_p[mp - WINDOW:], WINDOW)
    prompt_rwkv = _state_from_scan(sfin_p)[None]
    prompt_shift = _unpack_rwkv_row(p_p[mp - 1:mp]).reshape(1, 1, 1, RWKV_PROJ)
    prompt_ffn_conv = tail_p[-1, SUBLANES - (FFN_CONV - 1):].reshape(1, 1, FFN_CONV - 1, D_FF)
    sample_win_k = kv5(k_s, tn)
    sample_win_v = kv5(v_s, tn)
    sample_rwkv = _state_from_scan(sfin_s)[None]
    sample_shift = _unpack_rwkv_row(p_s[tn - 1::tn]).reshape(1, nb, 1, RWKV_PROJ)
    sample_ffn_conv = tail_s[:, SUBLANES - (FFN_CONV - 1):].reshape(1, nb, FFN_CONV - 1, D_FF)
    return (y_prompt, y_sample, prompt_meta_k, prompt_meta_v, prompt_win_k, prompt_win_v,
            prompt_rwkv, prompt_shift, prompt_ffn_conv, sample_win_k, sample_win_v,
            sample_rwkv, sample_shift, sample_ffn_conv)
```

```python
import functools
import math

import numpy as np
import jax
import jax.numpy as jnp
from jax import lax
from jax.experimental import pallas as pl
from jax.experimental.pallas import tpu as pltpu

F32 = jnp.float32
BF16 = jnp.bfloat16

D_MODEL = 2048
CHUNK = 64
N_META = 16
NORM_EPS = 1e-6
HEADS = 16
HEAD_DIM = 64
RWKV_DIM = HEADS * HEAD_DIM
DECAY_LORA = 64
ICLR_LORA = 64
GATE_LORA = 160
RWKV_PROJ = 3 * RWKV_DIM + DECAY_LORA + ICLR_LORA + GATE_LORA
RWKV_LN_EPS = 64e-5
KV_HEADS = 2
GROUP = HEADS // KV_HEADS
Q_DIM = HEADS * HEAD_DIM
KV_DIM = KV_HEADS * HEAD_DIM
WINDOW = 128
ROPE_THETA = 10000.0
D_FF = 3 * D_MODEL
FFN_CONV = 3
PAST_LEN = 4096

SUBLANES = 8
LANES = 128
V_LO = LANES // HEADS
V_HI = HEAD_DIM // V_LO
K_BLOCKS = HEAD_DIM // SUBLANES

COL_R, COL_K, COL_V, COL_Q = 0, 1024, 2048, 3072
COL_GA, COL_GB = 4096, 6144
COL_LORA = 8192
LORA_W = 512
LORA_XW, LORA_XA, LORA_XG = 0, 128, 256
COL_KV = 8704
N_PACK = 8960
RWKV_PACK = 3 * RWKV_DIM + LORA_W

VMEM_LIMIT = 56 * 1024 * 1024


def _scan_order():
    p = np.arange(RWKV_DIM)
    vh, h, vl = p // LANES, (p % LANES) // V_LO, p % V_LO
    return h * HEAD_DIM + vh * V_LO + vl


SCAN_ORDER = _scan_order()
NATURAL_ORDER = np.argsort(SCAN_ORDER)


def _cparams(sem):
    return pltpu.CompilerParams(dimension_semantics=sem, vmem_limit_bytes=VMEM_LIMIT)


def _rms(x, g):
    return x * lax.rsqrt(jnp.mean(x * x, axis=-1, keepdims=True) + NORM_EPS) * g


def _split3(x):
    hi = x.astype(BF16)
    r1 = x - hi.astype(F32)
    mid = r1.astype(BF16)
    lo = (r1 - mid.astype(F32)).astype(BF16)
    return hi, mid, lo


def _dot3(x, mat):
    hi, mid, lo = _split3(x)
    acc = jnp.dot(hi, mat, preferred_element_type=F32)
    acc = acc + jnp.dot(mid, mat, preferred_element_type=F32)
    return acc + jnp.dot(lo, mat, preferred_element_type=F32)


def _seg_sum(x, ones_bd):
    hi, mid, lo = _split3(x)
    outs = []
    for j in range(x.shape[1] // LANES):
        sl = slice(j * LANES, (j + 1) * LANES)
        acc = jnp.dot(hi[:, sl], ones_bd, preferred_element_type=F32)
        acc = acc + jnp.dot(mid[:, sl], ones_bd, preferred_element_type=F32)
        acc = acc + jnp.dot(lo[:, sl], ones_bd, preferred_element_type=F32)
        outs.append(acc)
    return jnp.concatenate(outs, axis=1)


def _tile_lanes(x, n):
    return jnp.concatenate([x] * n, axis=1)


def _norm_matmul_kernel(x_ref, g_ref, w_ref, o_ref, h_ref):
    @pl.when(pl.program_id(1) == 0)
    def _():
        h_ref[...] = _rms(x_ref[...], g_ref[...]).astype(BF16)

    o_ref[...] = jnp.dot(h_ref[...], w_ref[...], preferred_element_type=F32)


def _norm_matmul(x, g, w, tm, tn):
    m, d = x.shape
    n = w.shape[1]
    return pl.pallas_call(
        _norm_matmul_kernel,
        grid=(m // tm, n // tn),
        in_specs=[pl.BlockSpec((tm, d), lambda i, j: (i, 0)),
                  pl.BlockSpec((1, d), lambda i, j: (0, 0)),
                  pl.BlockSpec((d, tn), lambda i, j: (0, j))],
        out_specs=pl.BlockSpec((tm, tn), lambda i, j: (i, j)),
        out_shape=jax.ShapeDtypeStruct((m, n), F32),
        scratch_shapes=[pltpu.VMEM((tm, d), BF16)],
        compiler_params=_cparams(("parallel", "arbitrary")),
        name="norm_matmul",
    )(x, g, w)


def _store_head_pairs(ref, a, b):
    lane = lax.broadcasted_iota(jnp.int32, (a.shape[0], LANES), 1)
    low = lane < HEAD_DIM
    for j in range(a.shape[1] // LANES):
        sl = slice(j * LANES, (j + 1) * LANES)
        aj, bj = a[:, sl], b[:, sl]
        ref[2 * j] = jnp.where(low, aj, pltpu.roll(bj, HEAD_DIM, 1))
        ref[2 * j + 1] = jnp.where(low, pltpu.roll(aj, HEAD_DIM, 1), bj)


def _rwkv_prep_kernel(r_ref, k_ref, v_ref, l_ref, prev_ref, mu_ref, w0_ref, a0_ref, kk_ref, ka_ref,
                      rk_ref, ww_ref, wa_ref, wg_ref, ones_ref, hsum_ref,
                      ab_ref, wk_ref, rr_ref, vo_ref, g_ref, bonus_ref):
    rows = r_ref.shape[0]
    first = lax.broadcasted_iota(jnp.int32, (rows, 1), 0) == 0

    def shifted(x, lo, hi):
        prev = jnp.where(first, prev_ref[:, lo:hi], pltpu.roll(x, 1, 0))
        return x + (prev - x) * mu_ref[:, lo:hi]

    r = shifted(r_ref[...], 0, RWKV_DIM)
    k = shifted(k_ref[...], RWKV_DIM, 2 * RWKV_DIM)
    v = shifted(v_ref[...], 2 * RWKV_DIM, 3 * RWKV_DIM)
    lora = shifted(l_ref[...], 3 * RWKV_DIM, RWKV_PACK)
    xw = lora[:, LORA_XW:LORA_XW + LANES]
    xa = lora[:, LORA_XA:LORA_XA + LANES]
    xg = lora[:, LORA_XG:LORA_XG + 2 * LANES]

    w_pre = w0_ref[...] + jnp.dot(jnp.tanh(xw).astype(BF16), ww_ref[...], preferred_element_type=F32)
    decay = jnp.exp(-math.exp(-0.5) * jax.nn.sigmoid(w_pre))
    a = jax.nn.sigmoid(a0_ref[...] + jnp.dot(xa.astype(BF16), wa_ref[...], preferred_element_type=F32))
    g = jnp.dot(jax.nn.sigmoid(xg).astype(BF16), wg_ref[...], preferred_element_type=F32)

    kk = k * kk_ref[...]
    kk = kk * lax.rsqrt(jnp.maximum(_seg_sum(kk * kk, ones_ref[...]), 1e-24))
    k_mod = k * (1.0 + (a - 1.0) * ka_ref[...])
    rk_sum = _dot3(r * k_mod * rk_ref[...], hsum_ref[...])

    _store_head_pairs(ab_ref, -kk, kk * a)
    _store_head_pairs(wk_ref, decay, k_mod)
    _store_head_pairs(rr_ref, r, r)
    for vh in range(V_HI):
        vo_ref[vh] = v[:, vh * LANES:(vh + 1) * LANES]
    g_ref[...] = g
    bonus_ref[...] = _tile_lanes(rk_sum, V_HI) * v


def _rwkv_prep(p_all, prev_rows, pw, tb):
    m = p_all.shape[0]
    nt = m // tb
    row = lambda c: pl.BlockSpec((tb, RWKV_DIM), lambda i, c=c: (i, c))
    full = lambda a: pl.BlockSpec(a.shape, lambda i: (0,) * a.ndim)
    params = (pw["mu"], pw["w0"], pw["a0"], pw["k_k"], pw["k_a"], pw["r_k"],
              pw["w_lora"], pw["a_lora"], pw["g_lora"], pw["ones_bd"], pw["hsum_nat"])
    per_head = jax.ShapeDtypeStruct((HEADS, m, LANES), F32)
    per_vhi = jax.ShapeDtypeStruct((V_HI, m, LANES), F32)
    narrow = jax.ShapeDtypeStruct((m, RWKV_DIM), F32)
    return pl.pallas_call(
        _rwkv_prep_kernel,
        grid=(nt,),
        in_specs=[row(COL_R // RWKV_DIM), row(COL_K // RWKV_DIM), row(COL_V // RWKV_DIM),
                  pl.BlockSpec((tb, LORA_W), lambda i: (i, COL_LORA // LORA_W)),
                  pl.BlockSpec((None, 1, RWKV_PACK), lambda i: (i, 0, 0))]
                 + [full(a) for a in params],
        out_specs=[pl.BlockSpec((HEADS, tb, LANES), lambda i: (0, i, 0))] * 3
                  + [pl.BlockSpec((V_HI, tb, LANES), lambda i: (0, i, 0))]
                  + [pl.BlockSpec((tb, RWKV_DIM), lambda i: (i, 0))] * 2,
        out_shape=[per_head, per_head, per_head, per_vhi, narrow, narrow],
        compiler_params=_cparams(("parallel",)),
        name="rwkv_prep",
    )(p_all, p_all, p_all, p_all, prev_rows, *params)


def _sublane_allsum(x):
    x = x + pltpu.roll(x, 4, 0)
    x = x + pltpu.roll(x, 2, 0)
    return x + pltpu.roll(x, 1, 0)


def _scan_kernel(ab_ref, wk_ref, rr_ref, v_ref, s0_ref, y_ref, st_ref, ops_ref, *, tb, nblk):
    @pl.when(pl.program_id(0) % nblk == 0)
    def _():
        st_ref[...] = s0_ref[...]

    def col_tiles(ref, t):
        rep = jnp.concatenate(
            [jnp.broadcast_to(ref[h, pl.ds(t, 1), :], (V_LO, LANES)) for h in range(HEADS)], axis=0)
        return rep.T

    def stage(t, slot):
        ab = col_tiles(ab_ref, t)
        wk = col_tiles(wk_ref, t)
        rr = col_tiles(rr_ref, t)
        ops_ref[slot, 0] = ab[:HEAD_DIM]
        ops_ref[slot, 1] = ab[HEAD_DIM:]
        ops_ref[slot, 2] = wk[:HEAD_DIM]
        ops_ref[slot, 3] = wk[HEAD_DIM:]
        ops_ref[slot, 4] = rr[:HEAD_DIM]

    def op_tile(slot, which, kb):
        return ops_ref[slot, which, kb * SUBLANES:(kb + 1) * SUBLANES, :]

    def step(t, slot):
        ksl = lambda kb: slice(kb * SUBLANES, (kb + 1) * SUBLANES)
        acc = [None] * V_HI
        for kb in range(K_BLOCKS):
            alpha = op_tile(slot, 0, kb)
            for vh in range(V_HI):
                p = st_ref[vh, ksl(kb), :] * alpha
                acc[vh] = p if acc[vh] is None else acc[vh] + p
        u = [_sublane_allsum(acc[vh]) for vh in range(V_HI)]
        vb = [jnp.broadcast_to(v_ref[vh, pl.ds(t, 1), :], (SUBLANES, LANES)) for vh in range(V_HI)]
        yacc = [None] * V_HI
        for kb in range(K_BLOCKS):
            beta = op_tile(slot, 1, kb)
            decay = op_tile(slot, 2, kb)
            kmod = op_tile(slot, 3, kb)
            rr = op_tile(slot, 4, kb)
            for vh in range(V_HI):
                s_new = st_ref[vh, ksl(kb), :] * decay + beta * u[vh] + kmod * vb[vh]
                st_ref[vh, ksl(kb), :] = s_new
                p = s_new * rr
                yacc[vh] = p if yacc[vh] is None else yacc[vh] + p
        for vh in range(V_HI):
            y_ref[vh, pl.ds(t, 1), :] = _sublane_allsum(yacc[vh])[0:1, :]

    stage(0, 0)

    def body(i, carry):
        base = pl.multiple_of(i * SUBLANES, SUBLANES)
        following = pl.multiple_of(jnp.minimum(base + SUBLANES, tb - SUBLANES), SUBLANES)
        for j in range(SUBLANES):
            nxt = base + (j + 1) if j + 1 < SUBLANES else following
            stage(nxt, (j + 1) % 2)
            step(base + j, j % 2)
        return carry

    lax.fori_loop(0, tb // SUBLANES, body, 0)


def _rwkv_scan(ab, wk, rr, v, s0, tb, nblk):
    t_total = v.shape[1]
    nseq = s0.shape[0]
    grid = t_total // tb
    assert grid == nseq * nblk and tb % SUBLANES == 0
    st_shape = (V_HI, HEAD_DIM, LANES)
    per_head = pl.BlockSpec((HEADS, tb, LANES), lambda i: (0, i, 0))
    per_vhi = pl.BlockSpec((V_HI, tb, LANES), lambda i: (0, i, 0))
    state = pl.BlockSpec((None,) + st_shape, lambda i: (i // nblk, 0, 0, 0))
    return pl.pallas_call(
        functools.partial(_scan_kernel, tb=tb, nblk=nblk),
        grid=(grid,),
        in_specs=[per_head, per_head, per_head, per_vhi, state],
        out_specs=[per_vhi, state],
        out_shape=[jax.ShapeDtypeStruct((V_HI, t_total, LANES), F32),
                   jax.ShapeDtypeStruct((nseq,) + st_shape, F32)],
        scratch_shapes=[pltpu.VMEM((2, 5, HEAD_DIM, LANES), F32)],
        compiler_params=_cparams(("arbitrary",)),
        name="rwkv_scan",
    )(ab, wk, rr, v, s0)


def _state_to_scan(s):
    b = s.shape[0]
    s = s.reshape(b, HEADS, V_HI, V_LO, HEAD_DIM)
    return s.transpose(0, 2, 4, 1, 3).reshape(b, V_HI, HEAD_DIM, LANES)


def _state_from_scan(st):
    b = st.shape[0]
    st = st.reshape(b, V_HI, HEAD_DIM, HEADS, V_LO)
    return st.transpose(0, 3, 1, 4, 2).reshape(b, HEADS, HEAD_DIM, HEAD_DIM)


def _rope_tile(x, cos, sin_signed):
    lane = lax.broadcasted_iota(jnp.int32, x.shape, 1)
    first_half = (lane % HEAD_DIM) < (HEAD_DIM // 2)
    rot = jnp.where(first_half, pltpu.roll(x, LANES - HEAD_DIM // 2, 1), pltpu.roll(x, HEAD_DIM // 2, 1))
    return x * cos + rot * sin_signed


def _rope_kernel(q_ref, kv_ref, cos_ref, sin_ref, qo_ref, ko_ref, vo_ref):
    cos = cos_ref[...]
    sin = sin_ref[...]
    for j in range(Q_DIM // LANES):
        sl = slice(j * LANES, (j + 1) * LANES)
        qo_ref[:, sl] = _rope_tile(q_ref[:, sl], cos, sin).astype(BF16)
    ko_ref[...] = _rope_tile(kv_ref[:, :KV_DIM], cos, sin)
    vo_ref[...] = kv_ref[:, KV_DIM:]


def _rope(p_all, cos, sin, tb):
    m = p_all.shape[0]
    return pl.pallas_call(
        _rope_kernel,
        grid=(m // tb,),
        in_specs=[pl.BlockSpec((tb, Q_DIM), lambda i: (i, COL_Q // Q_DIM)),
                  pl.BlockSpec((tb, 2 * KV_DIM), lambda i: (i, COL_KV // (2 * KV_DIM))),
                  pl.BlockSpec((tb, LANES), lambda i: (i, 0)),
                  pl.BlockSpec((tb, LANES), lambda i: (i, 0))],
        out_specs=[pl.BlockSpec((tb, Q_DIM), lambda i: (i, 0)),
                   pl.BlockSpec((tb, KV_DIM), lambda i: (i, 0)),
                   pl.BlockSpec((tb, KV_DIM), lambda i: (i, 0))],
        out_shape=[jax.ShapeDtypeStruct((m, Q_DIM), BF16),
                   jax.ShapeDtypeStruct((m, KV_DIM), F32),
                   jax.ShapeDtypeStruct((m, KV_DIM), F32)],
        compiler_params=_cparams(("parallel",)),
        name="rope",
    )(p_all, p_all, cos, sin)


def _rope_tables(pos):
    half = HEAD_DIM // 2
    inv_freq = ROPE_THETA ** (-jnp.arange(half, dtype=F32) / half)
    ang = pos.astype(F32)[:, None] * inv_freq[None, :]
    cos = jnp.cos(ang)
    sin = jnp.sin(ang)
    cos = jnp.concatenate([cos, cos, cos, cos], axis=1)
    sin = jnp.concatenate([-sin, sin, -sin, sin], axis=1)
    return cos, sin


PAIRS = GROUP // 2


def _attn_kernel(*refs, nseg, band_mask):
    q_ref, sink_ref = refs[0], refs[1]
    k_refs = refs[2:2 + nseg]
    v_refs = refs[2 + nseg:2 + 2 * nseg]
    o_ref = refs[2 + 2 * nseg]
    tq = q_ref.shape[0]

    k_all = jnp.concatenate([r[...] for r in k_refs], axis=0)
    v_all = jnp.concatenate([r[...] for r in v_refs], axis=0)
    nkeys = k_all.shape[0]
    low = lax.broadcasted_iota(jnp.int32, (nkeys, LANES), 1) < HEAD_DIM
    k_swap = pltpu.roll(k_all, HEAD_DIM, 1)
    v_swap = pltpu.roll(v_all, HEAD_DIM, 1)
    if band_mask:
        c = pl.program_id(0)
        col = lax.broadcasted_iota(jnp.int32, (1, nkeys), 1)
        first_valid = N_META + jnp.maximum(2 - c, 0) * CHUNK
        visible = (col < N_META) | (col >= first_valid)
        bias = jnp.where(visible, 0.0, -jnp.inf).astype(F32)
    scale = HEAD_DIM ** -0.5
    ones_kv = jnp.ones((nkeys, LANES), BF16)
    groups = []
    for kvh in range(KV_HEADS):
        k_src, k_oth = (k_all, k_swap) if kvh == 0 else (k_swap, k_all)
        v_src, v_oth = (v_all, v_swap) if kvh == 0 else (v_swap, v_all)
        halves = ((jnp.where(low, k_src, 0.0).astype(BF16), jnp.where(low, v_src, 0.0).astype(BF16)),
                  (jnp.where(low, 0.0, k_oth).astype(BF16), jnp.where(low, 0.0, v_oth).astype(BF16)))
        qp = jnp.concatenate(
            [q_ref[:, (kvh * PAIRS + p) * LANES:(kvh * PAIRS + p + 1) * LANES] for p in range(PAIRS)], axis=0)
        scores = []
        for kx, _ in halves:
            s = lax.dot_general(qp, kx, (((1,), (1,)), ((), ())), preferred_element_type=F32) * scale
            scores.append(s + bias if band_mask else s)
        total = None
        for parity, (_, vx) in enumerate(halves):
            sink = jnp.concatenate(
                [jnp.broadcast_to(sink_ref[:, kvh * GROUP + 2 * p + parity:kvh * GROUP + 2 * p + parity + 1],
                                  (tq, 1)) for p in range(PAIRS)], axis=0)
            s = scores[parity]
            m = jnp.maximum(jnp.max(s, axis=1, keepdims=True), sink)
            e = jnp.exp(s - m).astype(BF16)
            den = jnp.dot(e, ones_kv, preferred_element_type=F32) + jnp.exp(sink - m)
            o = jnp.dot(e, vx, preferred_element_type=F32) / den
            total = o if total is None else total + o
        groups.extend(total[p * tq:(p + 1) * tq] for p in range(PAIRS))
    o_ref[...] = jnp.concatenate(groups, axis=1).astype(BF16)


def _attention(q, sinks, k_segs, v_segs, tq, grid, band_mask):
    nseg = len(k_segs)
    seg_specs = [pl.BlockSpec(bs, im) for (_, bs, im) in k_segs + v_segs]
    return pl.pallas_call(
        functools.partial(_attn_kernel, nseg=nseg, band_mask=band_mask),
        grid=(grid,),
        in_specs=[pl.BlockSpec((tq, Q_DIM), lambda i: (i, 0)),
                  pl.BlockSpec((1, HEADS), lambda i: (0, 0))] + seg_specs,
        out_specs=pl.BlockSpec((tq, Q_DIM), lambda i: (i, 0)),
        out_shape=jax.ShapeDtypeStruct((grid * tq, Q_DIM), BF16),
        compiler_params=_cparams(("parallel",)),
        name="attention",
    )(q, sinks, *[a for (a, _, _) in k_segs + v_segs])


def _mix_kernel(y_ref, bonus_ref, g_ref, yb_ref, ga_ref, gb_ref, lng_ref, lnb_ref, hsum_ref,
                wa_ref, wb_ref, o_ref, ya_ref):
    @pl.when(pl.program_id(1) == 0)
    def _():
        hsum = hsum_ref[...]
        mean = _dot3(sum(y_ref[vh] for vh in range(V_HI)), hsum) * (1.0 / HEAD_DIM)
        d = [y_ref[vh] - mean for vh in range(V_HI)]
        var = _dot3(sum(dv * dv for dv in d), hsum) * (1.0 / HEAD_DIM)
        inv = lax.rsqrt(var + RWKV_LN_EPS)
        for vh in range(V_HI):
            sl = slice(vh * LANES, (vh + 1) * LANES)
            yn = d[vh] * inv * lng_ref[:, sl] + lnb_ref[:, sl]
            ya_ref[:, sl] = ((yn + bonus_ref[:, sl]) * g_ref[:, sl]).astype(BF16)

    pa = jnp.dot(ya_ref[...], wa_ref[...], preferred_element_type=F32)
    pb = jnp.dot(yb_ref[...], wb_ref[...], preferred_element_type=F32)
    o_ref[...] = (jax.nn.sigmoid(ga_ref[...]) * pa + jax.nn.sigmoid(gb_ref[...]) * pb).astype(BF16)


def _mix(y, bonus, g, yb, p_all, ln_g, ln_b, hsum, wa, wb, tm):
    m = bonus.shape[0]
    tn = 1024
    row = pl.BlockSpec((tm, RWKV_DIM), lambda i, j: (i, 0))
    one = lambda a: pl.BlockSpec(a.shape, lambda i, j: (0,) * a.ndim)
    return pl.pallas_call(
        _mix_kernel,
        grid=(m // tm, D_MODEL // tn),
        in_specs=[pl.BlockSpec((V_HI, tm, LANES), lambda i, j: (0, i, 0)), row, row, row,
                  pl.BlockSpec((tm, tn), lambda i, j: (i, COL_GA // tn + j)),
                  pl.BlockSpec((tm, tn), lambda i, j: (i, COL_GB // tn + j)),
                  one(ln_g), one(ln_b), one(hsum),
                  pl.BlockSpec((RWKV_DIM, tn), lambda i, j: (0, j)),
                  pl.BlockSpec((Q_DIM, tn), lambda i, j: (0, j))],
        out_specs=pl.BlockSpec((tm, tn), lambda i, j: (i, j)),
        out_shape=jax.ShapeDtypeStruct((m, D_MODEL), BF16),
        scratch_shapes=[pltpu.VMEM((tm, RWKV_DIM), BF16)],
        compiler_params=_cparams(("parallel", "arbitrary")),
        name="branch_mix",
    )(y, bonus, g, yb, p_all, p_all, ln_g, ln_b, hsum, wa, wb)


def _out_norm_kernel(mix_ref, w_ref, x_ref, g_ref, o_ref):
    z = jnp.dot(mix_ref[...], w_ref[...], preferred_element_type=F32)
    o_ref[...] = x_ref[...] + _rms(z, g_ref[...])


def _out_norm(mixed, w, x, g, tm):
    m = x.shape[0]
    return pl.pallas_call(
        _out_norm_kernel,
        grid=(m // tm,),
        in_specs=[pl.BlockSpec((tm, D_MODEL), lambda i: (i, 0)),
                  pl.BlockSpec((D_MODEL, D_MODEL), lambda i: (0, 0)),
                  pl.BlockSpec((tm, D_MODEL), lambda i: (i, 0)),
                  pl.BlockSpec((1, D_MODEL), lambda i: (0, 0))],
        out_specs=pl.BlockSpec((tm, D_MODEL), lambda i: (i, 0)),
        out_shape=jax.ShapeDtypeStruct((m, D_MODEL), F32),
        compiler_params=_cparams(("parallel",)),
        name="out_norm",
    )(mixed, w, x, g)


HALO = 16


def _gelu_tanh(x):
    return 0.5 * x * (1.0 + jnp.tanh(0.7978845608028654 * (x + 0.044715 * x * x * x)))


def _ffn_kernel(*refs, seq_len):
    if seq_len is None:
        (x_ref, halo_ref, gpre_ref, wg_ref, wv_ref, wd_ref, cw_ref, cb_ref, gpost_ref,
         o_ref, tail_ref, h_ref, acc_ref) = refs
    else:
        (x_ref, sa_ref, sb_ref, gpre_ref, wg_ref, wv_ref, wd_ref, cw_ref, cb_ref, gpost_ref,
         o_ref, tail_ref, h_ref, acc_ref) = refs
    tm = x_ref.shape[0]
    j = pl.program_id(1)
    top = HALO if seq_len is None else 0

    @pl.when(j == 0)
    def _():
        if seq_len is None:
            halo = _rms(halo_ref[...], gpre_ref[...])
            halo = jnp.where(pl.program_id(0) == 0, 0.0, halo)
            h_ref[:HALO] = halo.astype(BF16)
        h_ref[top:] = _rms(x_ref[...], gpre_ref[...]).astype(BF16)
        acc_ref[...] = jnp.zeros_like(acc_ref)

    h_all = h_ref[...]
    gate_all = jnp.dot(h_all, wg_ref[...], preferred_element_type=F32)
    val = jnp.dot(h_all[top:], wv_ref[...], preferred_element_type=F32)
    g0 = gate_all[top:]
    g1 = pltpu.roll(gate_all, 1, 0)[top:]
    g2 = pltpu.roll(gate_all, 2, 0)[top:]
    if seq_len is not None:
        nseq = tm // seq_len
        tf = g0.shape[1]
        tau = lax.broadcasted_iota(jnp.int32, (tm, 1), 0) % seq_len
        expand = lambda s: jnp.concatenate(
            [jnp.broadcast_to(s[b:b + 1, :], (seq_len, tf)) for b in range(nseq)], axis=0)
        st_a = expand(sa_ref[...])
        st_b = expand(sb_ref[...])
        g1 = jnp.where(tau == 0, st_b, g1)
        g2 = jnp.where(tau == 0, st_a, jnp.where(tau == 1, st_b, g2))
    conv = cb_ref[...] + g2 * cw_ref[0:1, :] + g1 * cw_ref[1:2, :] + g0 * cw_ref[2:3, :]
    act = (_gelu_tanh(conv) * val).astype(BF16)
    acc_ref[...] += jnp.dot(act, wd_ref[...], preferred_element_type=F32)

    if seq_len is None:
        tail_ref[...] = g0[tm - SUBLANES:]
    else:
        for b in range(tm // seq_len):
            tail_ref[b] = g0[(b + 1) * seq_len - SUBLANES:(b + 1) * seq_len]

    @pl.when(j == pl.num_programs(1) - 1)
    def _():
        o_ref[...] = x_ref[...] + _rms(acc_ref[...], gpost_ref[...])


def _ffn(x1, conv_prev, fw, tm, tf, seq_len):
    m = x1.shape[0]
    nm, nf = m // tm, D_FF // tf
    one = lambda a: pl.BlockSpec(a.shape, lambda i, j: (0,) * a.ndim)
    xspec = pl.BlockSpec((tm, D_MODEL), lambda i, j: (i, 0), pipeline_mode=pl.Buffered(1))
    wspecs = [pl.BlockSpec((D_MODEL, tf), lambda i, j: (0, j)),
              pl.BlockSpec((D_MODEL, tf), lambda i, j: (0, D_FF // tf + j)),
              pl.BlockSpec((tf, D_MODEL), lambda i, j: (j, 0)),
              pl.BlockSpec((FFN_CONV, tf), lambda i, j: (0, j)),
              pl.BlockSpec((1, tf), lambda i, j: (0, j)),
              one(fw["g_post"])]
    wargs = (fw["w_up"], fw["w_up"], fw["w_down"], fw["conv_w"], fw["conv_b"], fw["g_post"])
    if seq_len is None:
        assert tm % HALO == 0
        pre_specs = [xspec,
                     pl.BlockSpec((HALO, D_MODEL), lambda i, j: (jnp.maximum(i * (tm // HALO) - 1, 0), 0)),
                     one(fw["g_pre"])]
        pre_args = (x1, x1, fw["g_pre"])
        tail_shape = (nm, SUBLANES, D_FF)
        tail_spec = pl.BlockSpec((None, SUBLANES, tf), lambda i, j: (i, 0, j))
        hrows = tm + HALO
    else:
        nseq = tm // seq_len
        assert nm == 1
        pre_specs = [xspec,
                     pl.BlockSpec((nseq, tf), lambda i, j: (0, j)),
                     pl.BlockSpec((nseq, tf), lambda i, j: (0, j)),
                     one(fw["g_pre"])]
        pre_args = (x1, conv_prev[:, 0, :], conv_prev[:, 1, :], fw["g_pre"])
        tail_shape = (nseq, SUBLANES, D_FF)
        tail_spec = pl.BlockSpec((nseq, SUBLANES, tf), lambda i, j: (0, 0, j))
        hrows = tm
    return pl.pallas_call(
        functools.partial(_ffn_kernel, seq_len=seq_len),
        grid=(nm, nf),
        in_specs=pre_specs + wspecs,
        out_specs=[pl.BlockSpec((tm, D_MODEL), lambda i, j: (i, 0), pipeline_mode=pl.Buffered(1)), tail_spec],
        out_shape=[jax.ShapeDtypeStruct((m, D_MODEL), F32),
                   jax.ShapeDtypeStruct(tail_shape, F32)],
        scratch_shapes=[pltpu.VMEM((hrows, D_MODEL), BF16), pltpu.VMEM((tm, D_MODEL), F32)],
        compiler_params=_cparams(("parallel", "arbitrary")),
        name="conv_ffn",
    )(*pre_args, *wargs)


def _pack_lora(x):
    z = lambda n: jnp.zeros((x.shape[0], n), x.dtype)
    xw = x[:, :DECAY_LORA]
    xa = x[:, DECAY_LORA:DECAY_LORA + ICLR_LORA]
    xg = x[:, DECAY_LORA + ICLR_LORA:]
    return jnp.concatenate([xw, z(LORA_XA - DECAY_LORA), xa, z(LORA_XG - LORA_XA - ICLR_LORA),
                            xg, z(LORA_W - LORA_XG - GATE_LORA)], axis=1)


def _unpack_lora(x):
    return jnp.concatenate([x[:, LORA_XW:LORA_XW + DECAY_LORA], x[:, LORA_XA:LORA_XA + ICLR_LORA],
                            x[:, LORA_XG:LORA_XG + GATE_LORA]], axis=1)


def _pack_cols(w):
    c = 3 * RWKV_DIM
    lora = w[:, c:RWKV_PROJ]
    q0 = RWKV_PROJ
    kv0 = q0 + Q_DIM
    g0 = kv0 + 2 * KV_DIM
    return jnp.concatenate([w[:, :2 * RWKV_DIM], w[:, 2 * RWKV_DIM:c][:, SCAN_ORDER], w[:, q0:kv0],
                            w[:, g0:g0 + 2 * D_MODEL], _pack_lora(lora), w[:, kv0:g0]], axis=1)


def _pack_rwkv_row(x):
    c = 3 * RWKV_DIM
    return jnp.concatenate([x[:, :2 * RWKV_DIM], x[:, 2 * RWKV_DIM:c][:, SCAN_ORDER], _pack_lora(x[:, c:])],
                           axis=1)


def _unpack_rwkv_row(p_row):
    return jnp.concatenate([p_row[:, :2 * RWKV_DIM], p_row[:, COL_V:COL_V + RWKV_DIM][:, NATURAL_ORDER],
                            _unpack_lora(p_row[:, COL_LORA:COL_LORA + LORA_W])], axis=1)


def _pick_tile(m, candidates):
    for c in candidates:
        if m % c == 0:
            return c
    raise ValueError(f"no row tile in {candidates} divides {m}")


def _pad_rows(w, rows):
    return jnp.concatenate([w, jnp.zeros((rows - w.shape[0], w.shape[1]), w.dtype)], axis=0)


def _trunk(x, lw, *, tm, tm_mix, tb_prep, prev_rows_fn, scan_tb, scan_nblk, s0, cos, sin, attend, ffn_tm,
           ffn_seq_len, conv_prev):
    p_all = _norm_matmul(x, lw["g_pre_mix"], lw["w_in"], tm, 1280)

    ab, wk, rr, v_scan, g, bonus = _rwkv_prep(p_all, prev_rows_fn(p_all), lw, tb_prep)
    y, s_fin = _rwkv_scan(ab, wk, rr, v_scan, s0, scan_tb, scan_nblk)

    q_roped, k_roped, v_att = _rope(p_all, cos, sin, tm)
    att = attend(q_roped, k_roped, v_att)

    mixed = _mix(y, bonus, g, att, p_all, lw["ln_g"], lw["ln_b"], lw["hsum_scan"],
                 lw["w_branch_a"], lw["w_branch_b"], tm_mix)
    x1 = _out_norm(mixed, lw["w_out"], x, lw["g_post_mix"], tm_mix)
    x2, tail = _ffn(x1, conv_prev, lw["ffn"], ffn_tm, 512, ffn_seq_len)
    return x2, p_all, k_roped, v_att, s_fin, tail


def kernel(x_prompt, x_sample, cache_meta_k, cache_meta_v, cache_win_k, cache_win_v, state_rwkv, state_shift, state_ffn_conv, meta_tokens, g_pre_mix, w_in, rwkv_mu, rwkv_w0, rwkv_w_lora_b, rwkv_a0, rwkv_a_lora_b, rwkv_g_lora_b, rwkv_k_k, rwkv_k_a, rwkv_r_k, rwkv_ln_g, rwkv_ln_b, attn_sinks, w_branch_a, w_branch_b, w_out, g_post_mix, g_pre_ffn, w_up, ffn_conv_w, ffn_conv_b, w_down, g_post_ffn):
    depth = w_in.shape[0]
    assert depth == 1
    l = 0
    seq = x_prompt.shape[1]
    nb, tn = x_sample.shape[0], x_sample.shape[1]
    row = lambda a: a.reshape(1, -1)
    lane = np.arange(LANES)
    chan = np.arange(RWKV_DIM)
    ones_bd = jnp.asarray(lane[:, None] // HEAD_DIM == lane[None, :] // HEAD_DIM, BF16)
    hsum_nat = jnp.asarray((chan // HEAD_DIM)[:, None] == (lane // V_LO)[None, :], BF16)
    hsum_scan = jnp.asarray((lane // V_LO)[:, None] == (lane // V_LO)[None, :], BF16)
    lw = dict(
        g_pre_mix=row(g_pre_mix[l]),
        w_in=_pack_cols(w_in[l].astype(BF16)),
        mu=_pack_rwkv_row(row(rwkv_mu[l])),
        w0=row(rwkv_w0[l]), a0=row(rwkv_a0[l]), k_k=row(rwkv_k_k[l]), k_a=row(rwkv_k_a[l]),
        r_k=row(rwkv_r_k[l]),
        w_lora=_pad_rows(rwkv_w_lora_b[l], LANES).astype(BF16),
        a_lora=_pad_rows(rwkv_a_lora_b[l], LANES).astype(BF16),
        g_lora=_pad_rows(rwkv_g_lora_b[l][:, SCAN_ORDER], 2 * LANES).astype(BF16),
        ones_bd=ones_bd, hsum_nat=hsum_nat, hsum_scan=hsum_scan,
        ln_g=row(rwkv_ln_g[l][SCAN_ORDER]), ln_b=row(rwkv_ln_b[l][SCAN_ORDER]),
        w_branch_a=w_branch_a[l][SCAN_ORDER, :].astype(BF16), w_branch_b=w_branch_b[l].astype(BF16),
        w_out=w_out[l].astype(BF16), g_post_mix=row(g_post_mix[l]),
        ffn=dict(g_pre=row(g_pre_ffn[l]), w_up=w_up[l].astype(BF16), w_down=w_down[l].astype(BF16),
                 conv_w=ffn_conv_w[l], conv_b=row(ffn_conv_b[l]), g_post=row(g_post_ffn[l])),
    )
    sinks = row(attn_sinks[l])

    mp = N_META + seq
    xp = jnp.concatenate([meta_tokens.astype(x_prompt.dtype), x_prompt[0]], axis=0)
    nchunk = seq // CHUNK
    tm_p = _pick_tile(mp, (912, 144, 48))
    tm_mix_p = _pick_tile(mp, (432, 144, 48))
    tb_prep_p = _pick_tile(mp, (144, 48))
    cos_p, sin_p = _rope_tables(jnp.arange(mp, dtype=jnp.int32))

    def prev_rows_prompt(p_all):
        tails = p_all[tb_prep_p - 1:mp - 1:tb_prep_p]
        tails = jnp.concatenate([tails[:, :3 * RWKV_DIM], tails[:, COL_LORA:COL_LORA + LORA_W]], axis=1)
        zero = jnp.zeros((1, RWKV_PACK), F32)
        return jnp.concatenate([zero, tails], axis=0)[:, None, :]

    def attend_prompt(q, k, v):
        mk, mv = k[:N_META], v[:N_META]
        whole = lambda a: (a, (N_META, KV_DIM), lambda c: (0, 0))
        o_m = _attention(q[:N_META], sinks, [whole(mk)], [whole(mv)], N_META, 1, False)
        kf, vf = k[N_META:], v[N_META:]
        band = lambda a: [(a, (CHUNK, KV_DIM), lambda c, d=d: (jnp.maximum(c - d, 0), 0)) for d in (2, 1, 0)]
        o_f = _attention(q[N_META:], sinks, [whole(mk)] + band(kf), [whole(mv)] + band(vf),
                         CHUNK, nchunk, True)
        return jnp.concatenate([o_m, o_f], axis=0)

    s0_p = jnp.zeros((1, V_HI, HEAD_DIM, LANES), F32)
    scan_tb_p = _pick_tile(mp, (144, 48))
    xp2, p_p, k_p, v_p, sfin_p, tail_p = _trunk(
        xp, lw, tm=tm_p, tm_mix=tm_mix_p, tb_prep=tb_prep_p, prev_rows_fn=prev_rows_prompt,
        scan_tb=scan_tb_p, scan_nblk=mp // scan_tb_p, s0=s0_p, cos=cos_p, sin=sin_p,
        attend=attend_prompt, ffn_tm=tm_p, ffn_seq_len=None, conv_prev=None)

    ms = nb * tn
    xs = x_sample.reshape(ms, D_MODEL)
    pos_s = N_META + PAST_LEN + jnp.arange(tn, dtype=jnp.int32)
    cos_s, sin_s = _rope_tables(jnp.tile(pos_s, nb))
    nwin = cache_win_k.shape[2]
    cmk = cache_meta_k[l].reshape(nb, N_META, KV_DIM)
    cmv = cache_meta_v[l].reshape(nb, N_META, KV_DIM)
    cwk = cache_win_k[l].reshape(nb, nwin, KV_DIM)
    cwv = cache_win_v[l].reshape(nb, nwin, KV_DIM)

    def prev_rows_sample(p_all):
        return _pack_rwkv_row(state_shift[l].reshape(nb, RWKV_PROJ))[:, None, :]

    def attend_sample(q, k, v):
        per_b = lambda a, n: (a, (None, n, KV_DIM), lambda b: (b, 0, 0))
        own = lambda a: (a, (tn, KV_DIM), lambda b: (b, 0))
        return _attention(q, sinks, [per_b(cmk, N_META), per_b(cwk, nwin), own(k)],
                          [per_b(cmv, N_META), per_b(cwv, nwin), own(v)], tn, nb, False)

    xs2, p_s, k_s, v_s, sfin_s, tail_s = _trunk(
        xs, lw, tm=ms, tm_mix=ms, tb_prep=tn, prev_rows_fn=prev_rows_sample, scan_tb=tn, scan_nblk=1,
        s0=_state_to_scan(state_rwkv[l]), cos=cos_s, sin=sin_s, attend=attend_sample,
        ffn_tm=ms, ffn_seq_len=tn, conv_prev=state_ffn_conv[l])

    kv5 = lambda a, n: a.reshape(1, -1, n, KV_HEADS, HEAD_DIM)
    y_prompt = xp2[N_META:][None]
    y_sample = xs2.reshape(nb, tn, D_MODEL)
    prompt_meta_k = kv5(k_p[:N_META], N_META)
    prompt_meta_v = kv5(v_p[:N_META], N_META)
    prompt_win_k = kv5(k_p[mp - WINDOW:], WINDOW)
    prompt_win_v = kv5(v_p[mp - WINDOW:], WINDOW)
    prompt_rwkv = _state_from_scan(sfin_p)[None]
    prompt_shift = _unpack_rwkv_row(p_p[mp - 1:mp]).reshape(1, 1, 1, RWKV_PROJ)
    prompt_ffn_conv = tail_p[-1, SUBLANES - (FFN_CONV - 1):].reshape(1, 1, FFN_CONV - 1, D_FF)
    sample_win_k = kv5(k_s, tn)
    sample_win_v = kv5(v_s, tn)
    sample_rwkv = _state_from_scan(sfin_s)[None]
    sample_shift = _unpack_rwkv_row(p_s[tn - 1::tn]).reshape(1, nb, 1, RWKV_PROJ)
    sample_ffn_conv = tail_s[:, SUBLANES - (FFN_CONV - 1):].reshape(1, nb, FFN_CONV - 1, D_FF)
    return (y_prompt, y_sample, prompt_meta_k, prompt_meta_v, prompt_win_k, prompt_win_v,
            prompt_rwkv, prompt_shift, prompt_ffn_conv, sample_win_k, sample_win_v,
            sample_rwkv, sample_shift, sample_ffn_conv)
```

```python
import functools
import math

import numpy as np
import jax
import jax.numpy as jnp
from jax import lax
from jax.experimental import pallas as pl
from jax.experimental.pallas import tpu as pltpu

F32 = jnp.float32
BF16 = jnp.bfloat16

D_MODEL = 2048
CHUNK = 64
N_META = 16
NORM_EPS = 1e-6
HEADS = 16
HEAD_DIM = 64
RWKV_DIM = HEADS * HEAD_DIM
DECAY_LORA = 64
ICLR_LORA = 64
GATE_LORA = 160
RWKV_PROJ = 3 * RWKV_DIM + DECAY_LORA + ICLR_LORA + GATE_LORA
RWKV_LN_EPS = 64e-5
KV_HEADS = 2
GROUP = HEADS // KV_HEADS
Q_DIM = HEADS * HEAD_DIM
KV_DIM = KV_HEADS * HEAD_DIM
WINDOW = 128
ROPE_THETA = 10000.0
D_FF = 3 * D_MODEL
FFN_CONV = 3
PAST_LEN = 4096

SUBLANES = 8
LANES = 128
V_LO = LANES // HEADS
V_HI = HEAD_DIM // V_LO
K_BLOCKS = HEAD_DIM // SUBLANES

COL_R, COL_K, COL_V, COL_Q = 0, 1024, 2048, 3072
COL_GA, COL_GB = 4096, 6144
COL_LORA = 8192
LORA_W = 512
LORA_XW, LORA_XA, LORA_XG = 0, 128, 256
COL_KV = 8704
N_PACK = 8960
RWKV_PACK = 3 * RWKV_DIM + LORA_W

VMEM_LIMIT = 56 * 1024 * 1024


def _scan_order():
    p = np.arange(RWKV_DIM)
    vh, h, vl = p // LANES, (p % LANES) // V_LO, p % V_LO
    return h * HEAD_DIM + vh * V_LO + vl


SCAN_ORDER = _scan_order()
NATURAL_ORDER = np.argsort(SCAN_ORDER)


def _cparams(sem):
    return pltpu.CompilerParams(dimension_semantics=sem, vmem_limit_bytes=VMEM_LIMIT)


def _rms(x, g):
    return x * lax.rsqrt(jnp.mean(x * x, axis=-1, keepdims=True) + NORM_EPS) * g


def _split3(x):
    hi = x.astype(BF16)
    r1 = x - hi.astype(F32)
    mid = r1.astype(BF16)
    lo = (r1 - mid.astype(F32)).astype(BF16)
    return hi, mid, lo


def _dot3(x, mat):
    hi, mid, lo = _split3(x)
    acc = jnp.dot(hi, mat, preferred_element_type=F32)
    acc = acc + jnp.dot(mid, mat, preferred_element_type=F32)
    return acc + jnp.dot(lo, mat, preferred_element_type=F32)


def _seg_sum(x, ones_bd):
    hi, mid, lo = _split3(x)
    outs = []
    for j in range(x.shape[1] // LANES):
        sl = slice(j * LANES, (j + 1) * LANES)
        acc = jnp.dot(hi[:, sl], ones_bd, preferred_element_type=F32)
        acc = acc + jnp.dot(mid[:, sl], ones_bd, preferred_element_type=F32)
        acc = acc + jnp.dot(lo[:, sl], ones_bd, preferred_element_type=F32)
        outs.append(acc)
    return jnp.concatenate(outs, axis=1)


def _tile_lanes(x, n):
    return jnp.concatenate([x] * n, axis=1)


def _norm_matmul_kernel(x_ref, g_ref, w_ref, o_ref, h_ref):
    @pl.when(pl.program_id(1) == 0)
    def _():
        h_ref[...] = _rms(x_ref[...], g_ref[...]).astype(BF16)

    o_ref[...] = lax.dot_general(h_ref[...], w_ref[...], (((1,), (1,)), ((), ())),
                                 preferred_element_type=F32)


def _norm_matmul(x, g, w_t, tm, tn):
    m, d = x.shape
    n = w_t.shape[0]
    return pl.pallas_call(
        _norm_matmul_kernel,
        grid=(m // tm, n // tn),
        in_specs=[pl.BlockSpec((tm, d), lambda i, j: (i, 0)),
                  pl.BlockSpec((1, d), lambda i, j: (0, 0)),
                  pl.BlockSpec((tn, d), lambda i, j: (j, 0))],
        out_specs=pl.BlockSpec((tm, tn), lambda i, j: (i, j)),
        out_shape=jax.ShapeDtypeStruct((m, n), F32),
        scratch_shapes=[pltpu.VMEM((tm, d), BF16)],
        compiler_params=_cparams(("parallel", "arbitrary")),
        name="norm_matmul",
    )(x, g, w_t)


def _store_head_pairs(ref, a, b):
    lane = lax.broadcasted_iota(jnp.int32, (a.shape[0], LANES), 1)
    low = lane < HEAD_DIM
    for j in range(a.shape[1] // LANES):
        sl = slice(j * LANES, (j + 1) * LANES)
        aj, bj = a[:, sl], b[:, sl]
        ref[2 * j] = jnp.where(low, aj, pltpu.roll(bj, HEAD_DIM, 1))
        ref[2 * j + 1] = jnp.where(low, pltpu.roll(aj, HEAD_DIM, 1), bj)


def _rwkv_prep_kernel(r_ref, k_ref, v_ref, l_ref, hr_ref, hk_ref, hv_ref, hl_ref, state_ref,
                      mu_ref, w0_ref, a0_ref, kk_ref, ka_ref,
                      rk_ref, ww_ref, wa_ref, wg_ref, ones_ref, hsum_ref,
                      ab_ref, wk_ref, rr_ref, vo_ref, g_ref, bonus_ref, *, nblk):
    rows = r_ref.shape[0]
    first = lax.broadcasted_iota(jnp.int32, (rows, 1), 0) == 0
    seq_start = pl.program_id(0) % nblk == 0

    def shifted(x, halo_ref, lo, hi):
        before = jnp.where(seq_start, state_ref[:, lo:hi], halo_ref[SUBLANES - 1:SUBLANES, :])
        prev = jnp.where(first, before, pltpu.roll(x, 1, 0))
        return x + (prev - x) * mu_ref[:, lo:hi]

    r = shifted(r_ref[...], hr_ref, 0, RWKV_DIM)
    k = shifted(k_ref[...], hk_ref, RWKV_DIM, 2 * RWKV_DIM)
    v = shifted(v_ref[...], hv_ref, 2 * RWKV_DIM, 3 * RWKV_DIM)
    lora = shifted(l_ref[...], hl_ref, 3 * RWKV_DIM, RWKV_PACK)
    xw = lora[:, LORA_XW:LORA_XW + LANES]
    xa = lora[:, LORA_XA:LORA_XA + LANES]
    xg = lora[:, LORA_XG:LORA_XG + 2 * LANES]

    w_pre = w0_ref[...] + jnp.dot(jnp.tanh(xw).astype(BF16), ww_ref[...], preferred_element_type=F32)
    decay = jnp.exp(-math.exp(-0.5) * jax.nn.sigmoid(w_pre))
    a = jax.nn.sigmoid(a0_ref[...] + jnp.dot(xa.astype(BF16), wa_ref[...], preferred_element_type=F32))
    g = jnp.dot(jax.nn.sigmoid(xg).astype(BF16), wg_ref[...], preferred_element_type=F32)

    kk = k * kk_ref[...]
    kk = kk * lax.rsqrt(jnp.maximum(_seg_sum(kk * kk, ones_ref[...]), 1e-24))
    k_mod = k * (1.0 + (a - 1.0) * ka_ref[...])
    rk_sum = _dot3(r * k_mod * rk_ref[...], hsum_ref[...])

    _store_head_pairs(ab_ref, -kk, kk * a)
    _store_head_pairs(wk_ref, decay, k_mod)
    _store_head_pairs(rr_ref, r, r)
    for vh in range(V_HI):
        vo_ref[vh] = v[:, vh * LANES:(vh + 1) * LANES]
    g_ref[...] = g
    bonus_ref[...] = _tile_lanes(rk_sum, V_HI) * v


def _rwkv_prep(p_all, state_rows, pw, tb, nblk):
    m = p_all.shape[0]
    nt = m // tb
    assert tb % SUBLANES == 0 and nt == state_rows.shape[0] * nblk
    row = lambda c: pl.BlockSpec((tb, RWKV_DIM), lambda i, c=c: (i, c))
    before = lambda i: jnp.maximum(i * (tb // SUBLANES) - 1, 0)
    halo = lambda c: pl.BlockSpec((SUBLANES, RWKV_DIM), lambda i, c=c: (before(i), c))
    full = lambda a: pl.BlockSpec(a.shape, lambda i: (0,) * a.ndim)
    params = (pw["mu"], pw["w0"], pw["a0"], pw["k_k"], pw["k_a"], pw["r_k"],
              pw["w_lora"], pw["a_lora"], pw["g_lora"], pw["ones_bd"], pw["hsum_nat"])
    per_head = jax.ShapeDtypeStruct((HEADS, m, LANES), F32)
    per_vhi = jax.ShapeDtypeStruct((V_HI, m, LANES), F32)
    narrow = jax.ShapeDtypeStruct((m, RWKV_DIM), F32)
    return pl.pallas_call(
        functools.partial(_rwkv_prep_kernel, nblk=nblk),
        grid=(nt,),
        in_specs=[row(COL_R // RWKV_DIM), row(COL_K // RWKV_DIM), row(COL_V // RWKV_DIM),
                  pl.BlockSpec((tb, LORA_W), lambda i: (i, COL_LORA // LORA_W)),
                  halo(COL_R // RWKV_DIM), halo(COL_K // RWKV_DIM), halo(COL_V // RWKV_DIM),
                  pl.BlockSpec((SUBLANES, LORA_W), lambda i: (before(i), COL_LORA // LORA_W)),
                  pl.BlockSpec((None, 1, RWKV_PACK), lambda i: (i // nblk, 0, 0))]
                 + [full(a) for a in params],
        out_specs=[pl.BlockSpec((HEADS, tb, LANES), lambda i: (0, i, 0))] * 3
                  + [pl.BlockSpec((V_HI, tb, LANES), lambda i: (0, i, 0))]
                  + [pl.BlockSpec((tb, RWKV_DIM), lambda i: (i, 0))] * 2,
        out_shape=[per_head, per_head, per_head, per_vhi, narrow, narrow],
        compiler_params=_cparams(("parallel",)),
        name="rwkv_prep",
    )(p_all, p_all, p_all, p_all, p_all, p_all, p_all, p_all, state_rows, *params)


def _sublane_allsum(x):
    x = x + pltpu.roll(x, 4, 0)
    x = x + pltpu.roll(x, 2, 0)
    return x + pltpu.roll(x, 1, 0)


def _scan_kernel(ab_ref, wk_ref, rr_ref, v_ref, s0_ref, y_ref, st_ref, ops_ref, *, tb, nblk):
    @pl.when(pl.program_id(0) % nblk == 0)
    def _():
        st_ref[...] = s0_ref[...]

    def col_tiles(ref, t):
        rep = jnp.concatenate(
            [jnp.broadcast_to(ref[h, pl.ds(t, 1), :], (V_LO, LANES)) for h in range(HEADS)], axis=0)
        return rep.T

    def stage(t, slot):
        ab = col_tiles(ab_ref, t)
        wk = col_tiles(wk_ref, t)
        rr = col_tiles(rr_ref, t)
        ops_ref[slot, 0] = ab[:HEAD_DIM]
        ops_ref[slot, 1] = ab[HEAD_DIM:]
        ops_ref[slot, 2] = wk[:HEAD_DIM]
        ops_ref[slot, 3] = wk[HEAD_DIM:]
        ops_ref[slot, 4] = rr[:HEAD_DIM]

    def op_tile(slot, which, kb):
        return ops_ref[slot, which, kb * SUBLANES:(kb + 1) * SUBLANES, :]

    def step(t, slot):
        ksl = lambda kb: slice(kb * SUBLANES, (kb + 1) * SUBLANES)
        acc = [None] * V_HI
        for kb in range(K_BLOCKS):
            alpha = op_tile(slot, 0, kb)
            for vh in range(V_HI):
                p = st_ref[vh, ksl(kb), :] * alpha
                acc[vh] = p if acc[vh] is None else acc[vh] + p
        u = [_sublane_allsum(acc[vh]) for vh in range(V_HI)]
        vb = [jnp.broadcast_to(v_ref[vh, pl.ds(t, 1), :], (SUBLANES, LANES)) for vh in range(V_HI)]
        yacc = [None] * V_HI
        for kb in range(K_BLOCKS):
            beta = op_tile(slot, 1, kb)
            decay = op_tile(slot, 2, kb)
            kmod = op_tile(slot, 3, kb)
            rr = op_tile(slot, 4, kb)
            for vh in range(V_HI):
                s_new = st_ref[vh, ksl(kb), :] * decay + beta * u[vh] + kmod * vb[vh]
                st_ref[vh, ksl(kb), :] = s_new
                p = s_new * rr
                yacc[vh] = p if yacc[vh] is None else yacc[vh] + p
        for vh in range(V_HI):
            y_ref[vh, pl.ds(t, 1), :] = _sublane_allsum(yacc[vh])[0:1, :]

    stage(0, 0)

    def body(i, carry):
        base = pl.multiple_of(i * SUBLANES, SUBLANES)
        following = pl.multiple_of(jnp.minimum(base + SUBLANES, tb - SUBLANES), SUBLANES)
        for j in range(SUBLANES):
            nxt = base + (j + 1) if j + 1 < SUBLANES else following
            stage(nxt, (j + 1) % 2)
            step(base + j, j % 2)
        return carry

    lax.fori_loop(0, tb // SUBLANES, body, 0)


def _rwkv_scan(ab, wk, rr, v, s0, tb, nblk):
    t_total = v.shape[1]
    nseq = s0.shape[0]
    grid = t_total // tb
    assert grid == nseq * nblk and tb % SUBLANES == 0
    st_shape = (V_HI, HEAD_DIM, LANES)
    per_head = pl.BlockSpec((HEADS, tb, LANES), lambda i: (0, i, 0))
    per_vhi = pl.BlockSpec((V_HI, tb, LANES), lambda i: (0, i, 0))
    state = pl.BlockSpec((None,) + st_shape, lambda i: (i // nblk, 0, 0, 0))
    return pl.pallas_call(
        functools.partial(_scan_kernel, tb=tb, nblk=nblk),
        grid=(grid,),
        in_specs=[per_head, per_head, per_head, per_vhi, state],
        out_specs=[per_vhi, state],
        out_shape=[jax.ShapeDtypeStruct((V_HI, t_total, LANES), F32),
                   jax.ShapeDtypeStruct((nseq,) + st_shape, F32)],
        scratch_shapes=[pltpu.VMEM((2, 5, HEAD_DIM, LANES), F32)],
        compiler_params=_cparams(("arbitrary",)),
        name="rwkv_scan",
    )(ab, wk, rr, v, s0)


def _state_to_scan(s):
    b = s.shape[0]
    s = s.reshape(b, HEADS, V_HI, V_LO, HEAD_DIM)
    return s.transpose(0, 2, 4, 1, 3).reshape(b, V_HI, HEAD_DIM, LANES)


def _state_from_scan(st):
    b = st.shape[0]
    st = st.reshape(b, V_HI, HEAD_DIM, HEADS, V_LO)
    return st.transpose(0, 3, 1, 4, 2).reshape(b, HEADS, HEAD_DIM, HEAD_DIM)


def _rope_tile(x, cos, sin_signed):
    lane = lax.broadcasted_iota(jnp.int32, x.shape, 1)
    first_half = (lane % HEAD_DIM) < (HEAD_DIM // 2)
    rot = jnp.where(first_half, pltpu.roll(x, LANES - HEAD_DIM // 2, 1), pltpu.roll(x, HEAD_DIM // 2, 1))
    return x * cos + rot * sin_signed


def _rope_kernel(q_ref, kv_ref, cos_ref, sin_ref, qo_ref, ko_ref, vo_ref):
    cos = cos_ref[...]
    sin = sin_ref[...]
    for j in range(Q_DIM // LANES):
        sl = slice(j * LANES, (j + 1) * LANES)
        qo_ref[:, sl] = _rope_tile(q_ref[:, sl], cos, sin).astype(BF16)
    ko_ref[...] = _rope_tile(kv_ref[:, :KV_DIM], cos, sin)
    vo_ref[...] = kv_ref[:, KV_DIM:]


def _rope(p_all, cos, sin, tb):
    m = p_all.shape[0]
    return pl.pallas_call(
        _rope_kernel,
        grid=(m // tb,),
        in_specs=[pl.BlockSpec((tb, Q_DIM), lambda i: (i, COL_Q // Q_DIM)),
                  pl.BlockSpec((tb, 2 * KV_DIM), lambda i: (i, COL_KV // (2 * KV_DIM))),
                  pl.BlockSpec((tb, LANES), lambda i: (i, 0)),
                  pl.BlockSpec((tb, LANES), lambda i: (i, 0))],
        out_specs=[pl.BlockSpec((tb, Q_DIM), lambda i: (i, 0)),
                   pl.BlockSpec((tb, KV_DIM), lambda i: (i, 0)),
                   pl.BlockSpec((tb, KV_DIM), lambda i: (i, 0))],
        out_shape=[jax.ShapeDtypeStruct((m, Q_DIM), BF16),
                   jax.ShapeDtypeStruct((m, KV_DIM), F32),
                   jax.ShapeDtypeStruct((m, KV_DIM), F32)],
        compiler_params=_cparams(("parallel",)),
        name="rope",
    )(p_all, p_all, cos, sin)


def _rope_tables(pos):
    half = HEAD_DIM // 2
    inv_freq = ROPE_THETA ** (-jnp.arange(half, dtype=F32) / half)
    ang = pos.astype(F32)[:, None] * inv_freq[None, :]
    cos = jnp.cos(ang)
    sin = jnp.sin(ang)
    cos = jnp.concatenate([cos, cos, cos, cos], axis=1)
    sin = jnp.concatenate([-sin, sin, -sin, sin], axis=1)
    return cos, sin


PAIRS = GROUP // 2


def _attn_kernel(*refs, nseg, band_mask):
    q_ref, sink_ref = refs[0], refs[1]
    k_refs = refs[2:2 + nseg]
    v_refs = refs[2 + nseg:2 + 2 * nseg]
    o_ref = refs[2 + 2 * nseg]
    tq = q_ref.shape[0]

    k_all = jnp.concatenate([r[...] for r in k_refs], axis=0)
    v_all = jnp.concatenate([r[...] for r in v_refs], axis=0)
    nkeys = k_all.shape[0]
    low = lax.broadcasted_iota(jnp.int32, (nkeys, LANES), 1) < HEAD_DIM
    k_swap = pltpu.roll(k_all, HEAD_DIM, 1)
    v_swap = pltpu.roll(v_all, HEAD_DIM, 1)
    if band_mask:
        c = pl.program_id(0)
        col = lax.broadcasted_iota(jnp.int32, (1, nkeys), 1)
        first_valid = N_META + jnp.maximum(2 - c, 0) * CHUNK
        visible = (col < N_META) | (col >= first_valid)
        bias = jnp.where(visible, 0.0, -jnp.inf).astype(F32)
    scale = HEAD_DIM ** -0.5
    ones_kv = jnp.ones((nkeys, LANES), BF16)
    groups = []
    for kvh in range(KV_HEADS):
        k_src, k_oth = (k_all, k_swap) if kvh == 0 else (k_swap, k_all)
        v_src, v_oth = (v_all, v_swap) if kvh == 0 else (v_swap, v_all)
        halves = ((jnp.where(low, k_src, 0.0).astype(BF16), jnp.where(low, v_src, 0.0).astype(BF16)),
                  (jnp.where(low, 0.0, k_oth).astype(BF16), jnp.where(low, 0.0, v_oth).astype(BF16)))
        qp = jnp.concatenate(
            [q_ref[:, (kvh * PAIRS + p) * LANES:(kvh * PAIRS + p + 1) * LANES] for p in range(PAIRS)], axis=0)
        scores = []
        for kx, _ in halves:
            s = lax.dot_general(qp, kx, (((1,), (1,)), ((), ())), preferred_element_type=F32) * scale
            scores.append(s + bias if band_mask else s)
        total = None
        for parity, (_, vx) in enumerate(halves):
            sink = jnp.concatenate(
                [jnp.broadcast_to(sink_ref[:, kvh * GROUP + 2 * p + parity:kvh * GROUP + 2 * p + parity + 1],
                                  (tq, 1)) for p in range(PAIRS)], axis=0)
            s = scores[parity]
            m = jnp.maximum(jnp.max(s, axis=1, keepdims=True), sink)
            e = jnp.exp(s - m).astype(BF16)
            den = jnp.dot(e, ones_kv, preferred_element_type=F32) + jnp.exp(sink - m)
            o = jnp.dot(e, vx, preferred_element_type=F32) / den
            total = o if total is None else total + o
        groups.extend(total[p * tq:(p + 1) * tq] for p in range(PAIRS))
    o_ref[...] = jnp.concatenate(groups, axis=1).astype(BF16)


def _attention(q, sinks, k_segs, v_segs, tq, grid, band_mask):
    nseg = len(k_segs)
    seg_specs = [pl.BlockSpec(bs, im) for (_, bs, im) in k_segs + v_segs]
    return pl.pallas_call(
        functools.partial(_attn_kernel, nseg=nseg, band_mask=band_mask),
        grid=(grid,),
        in_specs=[pl.BlockSpec((tq, Q_DIM), lambda i: (i, 0)),
                  pl.BlockSpec((1, HEADS), lambda i: (0, 0))] + seg_specs,
        out_specs=pl.BlockSpec((tq, Q_DIM), lambda i: (i, 0)),
        out_shape=jax.ShapeDtypeStruct((grid * tq, Q_DIM), BF16),
        compiler_params=_cparams(("parallel",)),
        name="attention",
    )(q, sinks, *[a for (a, _, _) in k_segs + v_segs])


def _mix_kernel(y_ref, bonus_ref, g_ref, yb_ref, ga_ref, gb_ref, lng_ref, lnb_ref, hsum_ref,
                wa_ref, wb_ref, o_ref, ya_ref):
    @pl.when(pl.program_id(1) == 0)
    def _():
        hsum = hsum_ref[...]
        mean = _dot3(sum(y_ref[vh] for vh in range(V_HI)), hsum) * (1.0 / HEAD_DIM)
        d = [y_ref[vh] - mean for vh in range(V_HI)]
        var = _dot3(sum(dv * dv for dv in d), hsum) * (1.0 / HEAD_DIM)
        inv = lax.rsqrt(var + RWKV_LN_EPS)
        for vh in range(V_HI):
            sl = slice(vh * LANES, (vh + 1) * LANES)
            yn = d[vh] * inv * lng_ref[:, sl] + lnb_ref[:, sl]
            ya_ref[:, sl] = ((yn + bonus_ref[:, sl]) * g_ref[:, sl]).astype(BF16)

    pa = jnp.dot(ya_ref[...], wa_ref[...], preferred_element_type=F32)
    pb = jnp.dot(yb_ref[...], wb_ref[...], preferred_element_type=F32)
    o_ref[...] = (jax.nn.sigmoid(ga_ref[...]) * pa + jax.nn.sigmoid(gb_ref[...]) * pb).astype(BF16)


def _mix(y, bonus, g, yb, p_all, ln_g, ln_b, hsum, wa, wb, tm):
    m = bonus.shape[0]
    tn = 1024
    row = pl.BlockSpec((tm, RWKV_DIM), lambda i, j: (i, 0))
    one = lambda a: pl.BlockSpec(a.shape, lambda i, j: (0,) * a.ndim)
    return pl.pallas_call(
        _mix_kernel,
        grid=(m // tm, D_MODEL // tn),
        in_specs=[pl.BlockSpec((V_HI, tm, LANES), lambda i, j: (0, i, 0)), row, row, row,
                  pl.BlockSpec((tm, tn), lambda i, j: (i, COL_GA // tn + j)),
                  pl.BlockSpec((tm, tn), lambda i, j: (i, COL_GB // tn + j)),
                  one(ln_g), one(ln_b), one(hsum),
                  pl.BlockSpec((RWKV_DIM, tn), lambda i, j: (0, j)),
                  pl.BlockSpec((Q_DIM, tn), lambda i, j: (0, j))],
        out_specs=pl.BlockSpec((tm, tn), lambda i, j: (i, j)),
        out_shape=jax.ShapeDtypeStruct((m, D_MODEL), BF16),
        scratch_shapes=[pltpu.VMEM((tm, RWKV_DIM), BF16)],
        compiler_params=_cparams(("parallel", "arbitrary")),
        name="branch_mix",
    )(y, bonus, g, yb, p_all, p_all, ln_g, ln_b, hsum, wa, wb)


def _out_norm_kernel(mix_ref, w_ref, x_ref, g_ref, o_ref):
    z = jnp.dot(mix_ref[...], w_ref[...], preferred_element_type=F32)
    o_ref[...] = x_ref[...] + _rms(z, g_ref[...])


def _out_norm(mixed, w, x, g, tm):
    m = x.shape[0]
    return pl.pallas_call(
        _out_norm_kernel,
        grid=(m // tm,),
        in_specs=[pl.BlockSpec((tm, D_MODEL), lambda i: (i, 0)),
                  pl.BlockSpec((D_MODEL, D_MODEL), lambda i: (0, 0)),
                  pl.BlockSpec((tm, D_MODEL), lambda i: (i, 0)),
                  pl.BlockSpec((1, D_MODEL), lambda i: (0, 0))],
        out_specs=pl.BlockSpec((tm, D_MODEL), lambda i: (i, 0)),
        out_shape=jax.ShapeDtypeStruct((m, D_MODEL), F32),
        compiler_params=_cparams(("parallel",)),
        name="out_norm",
    )(mixed, w, x, g)


HALO = 16


def _gelu_tanh(x):
    return 0.5 * x * (1.0 + jnp.tanh(0.7978845608028654 * (x + 0.044715 * x * x * x)))


def _ffn_kernel(*refs, seq_len):
    if seq_len is None:
        (x_ref, halo_ref, gpre_ref, wg_ref, wv_ref, wd_ref, cw_ref, cb_ref, gpost_ref,
         o_ref, tail_ref, h_ref) = refs
    else:
        (x_ref, sa_ref, sb_ref, gpre_ref, wg_ref, wv_ref, wd_ref, cw_ref, cb_ref, gpost_ref,
         o_ref, tail_ref, h_ref) = refs
    acc_ref = o_ref
    tm = x_ref.shape[0]
    j = pl.program_id(1)
    top = HALO if seq_len is None else 0

    @pl.when(j == 0)
    def _():
        if seq_len is None:
            halo = _rms(halo_ref[...], gpre_ref[...])
            halo = jnp.where(pl.program_id(0) == 0, 0.0, halo)
            h_ref[:HALO] = halo.astype(BF16)
        h_ref[top:] = _rms(x_ref[...], gpre_ref[...]).astype(BF16)
        acc_ref[...] = jnp.zeros_like(acc_ref)

    h_all = h_ref[...]
    gate_all = jnp.dot(h_all, wg_ref[...], preferred_element_type=F32)
    val = jnp.dot(h_all[top:], wv_ref[...], preferred_element_type=F32)
    g0 = gate_all[top:]
    g1 = pltpu.roll(gate_all, 1, 0)[top:]
    g2 = pltpu.roll(gate_all, 2, 0)[top:]
    if seq_len is not None:
        nseq = tm // seq_len
        tf = g0.shape[1]
        tau = lax.broadcasted_iota(jnp.int32, (tm, 1), 0) % seq_len
        expand = lambda s: jnp.concatenate(
            [jnp.broadcast_to(s[b:b + 1, :], (seq_len, tf)) for b in range(nseq)], axis=0)
        st_a = expand(sa_ref[...])
        st_b = expand(sb_ref[...])
        g1 = jnp.where(tau == 0, st_b, g1)
        g2 = jnp.where(tau == 0, st_a, jnp.where(tau == 1, st_b, g2))
    conv = cb_ref[...] + g2 * cw_ref[0:1, :] + g1 * cw_ref[1:2, :] + g0 * cw_ref[2:3, :]
    act = (_gelu_tanh(conv) * val).astype(BF16)
    acc_ref[...] += jnp.dot(act, wd_ref[...], preferred_element_type=F32)

    if seq_len is None:
        tail_ref[...] = g0[tm - SUBLANES:]
    else:
        for b in range(tm // seq_len):
            tail_ref[b] = g0[(b + 1) * seq_len - SUBLANES:(b + 1) * seq_len]

    @pl.when(j == pl.num_programs(1) - 1)
    def _():
        o_ref[...] = x_ref[...] + _rms(acc_ref[...], gpost_ref[...])


def _ffn(x1, conv_prev, fw, tm, tf, seq_len):
    m = x1.shape[0]
    nm, nf = m // tm, D_FF // tf
    one = lambda a: pl.BlockSpec(a.shape, lambda i, j: (0,) * a.ndim)
    xspec = pl.BlockSpec((tm, D_MODEL), lambda i, j: (i, 0))
    wspecs = [pl.BlockSpec((D_MODEL, tf), lambda i, j: (0, j)),
              pl.BlockSpec((D_MODEL, tf), lambda i, j: (0, D_FF // tf + j)),
              pl.BlockSpec((tf, D_MODEL), lambda i, j: (j, 0)),
              pl.BlockSpec((FFN_CONV, tf), lambda i, j: (0, j)),
              pl.BlockSpec((1, tf), lambda i, j: (0, j)),
              one(fw["g_post"])]
    wargs = (fw["w_up"], fw["w_up"], fw["w_down"], fw["conv_w"], fw["conv_b"], fw["g_post"])
    if seq_len is None:
        assert tm % HALO == 0
        pre_specs = [xspec,
                     pl.BlockSpec((HALO, D_MODEL), lambda i, j: (jnp.maximum(i * (tm // HALO) - 1, 0), 0)),
                     one(fw["g_pre"])]
        pre_args = (x1, x1, fw["g_pre"])
        tail_shape = (nm, SUBLANES, D_FF)
        tail_spec = pl.BlockSpec((None, SUBLANES, tf), lambda i, j: (i, 0, j))
        hrows = tm + HALO
    else:
        nseq = tm // seq_len
        assert nm == 1
        pre_specs = [xspec,
                     pl.BlockSpec((nseq, tf), lambda i, j: (0, j)),
                     pl.BlockSpec((nseq, tf), lambda i, j: (0, j)),
                     one(fw["g_pre"])]
        pre_args = (x1, conv_prev[:, 0, :], conv_prev[:, 1, :], fw["g_pre"])
        tail_shape = (nseq, SUBLANES, D_FF)
        tail_spec = pl.BlockSpec((nseq, SUBLANES, tf), lambda i, j: (0, 0, j))
        hrows = tm
    return pl.pallas_call(
        functools.partial(_ffn_kernel, seq_len=seq_len),
        grid=(nm, nf),
        in_specs=pre_specs + wspecs,
        out_specs=[pl.BlockSpec((tm, D_MODEL), lambda i, j: (i, 0), pipeline_mode=pl.Buffered(1)), tail_spec],
        out_shape=[jax.ShapeDtypeStruct((m, D_MODEL), F32),
                   jax.ShapeDtypeStruct(tail_shape, F32)],
        scratch_shapes=[pltpu.VMEM((hrows, D_MODEL), BF16)],
        compiler_params=_cparams(("parallel", "arbitrary")),
        name="conv_ffn",
    )(*pre_args, *wargs)


def _pack_lora(x, axis=1):
    cut = lambda a, b: lax.slice_in_dim(x, a, b, axis=axis)

    def z(n):
        shape = list(x.shape)
        shape[axis] = n
        return jnp.zeros(shape, x.dtype)

    xw = cut(0, DECAY_LORA)
    xa = cut(DECAY_LORA, DECAY_LORA + ICLR_LORA)
    xg = cut(DECAY_LORA + ICLR_LORA, DECAY_LORA + ICLR_LORA + GATE_LORA)
    return jnp.concatenate([xw, z(LORA_XA - DECAY_LORA), xa, z(LORA_XG - LORA_XA - ICLR_LORA),
                            xg, z(LORA_W - LORA_XG - GATE_LORA)], axis=axis)


def _unpack_lora(x):
    return jnp.concatenate([x[:, LORA_XW:LORA_XW + DECAY_LORA], x[:, LORA_XA:LORA_XA + ICLR_LORA],
                            x[:, LORA_XG:LORA_XG + GATE_LORA]], axis=1)


def _pack_in_proj_t(wt):
    c = 3 * RWKV_DIM
    q0 = RWKV_PROJ
    kv0 = q0 + Q_DIM
    g0 = kv0 + 2 * KV_DIM
    return jnp.concatenate([wt[:2 * RWKV_DIM], wt[2 * RWKV_DIM:c][SCAN_ORDER], wt[q0:kv0],
                            wt[g0:g0 + 2 * D_MODEL], _pack_lora(wt[c:RWKV_PROJ], axis=0), wt[kv0:g0]], axis=0)


def _pack_rwkv_row(x):
    c = 3 * RWKV_DIM
    return jnp.concatenate([x[:, :2 * RWKV_DIM], x[:, 2 * RWKV_DIM:c][:, SCAN_ORDER], _pack_lora(x[:, c:])],
                           axis=1)


def _unpack_rwkv_row(p_row):
    return jnp.concatenate([p_row[:, :2 * RWKV_DIM], p_row[:, COL_V:COL_V + RWKV_DIM][:, NATURAL_ORDER],
                            _unpack_lora(p_row[:, COL_LORA:COL_LORA + LORA_W])], axis=1)


def _pick_tile(m, candidates):
    for c in candidates:
        if m % c == 0:
            return c
    raise ValueError(f"no row tile in {candidates} divides {m}")


def _pad_rows(w, rows):
    return jnp.concatenate([w, jnp.zeros((rows - w.shape[0], w.shape[1]), w.dtype)], axis=0)


def _trunk(x, lw, *, tm, tm_mix, tb_prep, shift_rows, scan_tb, s0, cos, sin, attend, ffn_tm,
           ffn_seq_len, conv_prev):
    nseq = s0.shape[0]
    seq_rows = x.shape[0] // nseq
    scan_nblk = seq_rows // scan_tb
    p_all = _norm_matmul(x, lw["g_pre_mix"], lw["w_in_t"], tm, 1280)

    ab, wk, rr, v_scan, g, bonus = _rwkv_prep(p_all, shift_rows, lw, tb_prep, seq_rows // tb_prep)
    y, s_fin = _rwkv_scan(ab, wk, rr, v_scan, s0, scan_tb, scan_nblk)

    q_roped, k_roped, v_att = _rope(p_all, cos, sin, tm)
    att = attend(q_roped, k_roped, v_att)

    mixed = _mix(y, bonus, g, att, p_all, lw["ln_g"], lw["ln_b"], lw["hsum_scan"],
                 lw["w_branch_a"], lw["w_branch_b"], tm_mix)
    x1 = _out_norm(mixed, lw["w_out"], x, lw["g_post_mix"], tm_mix)
    x2, tail = _ffn(x1, conv_prev, lw["ffn"], ffn_tm, 512, ffn_seq_len)
    return x2, p_all, k_roped, v_att, s_fin, tail


def kernel(x_prompt, x_sample, cache_meta_k, cache_meta_v, cache_win_k, cache_win_v, state_rwkv, state_shift, state_ffn_conv, meta_tokens, g_pre_mix, w_in, rwkv_mu, rwkv_w0, rwkv_w_lora_b, rwkv_a0, rwkv_a_lora_b, rwkv_g_lora_b, rwkv_k_k, rwkv_k_a, rwkv_r_k, rwkv_ln_g, rwkv_ln_b, attn_sinks, w_branch_a, w_branch_b, w_out, g_post_mix, g_pre_ffn, w_up, ffn_conv_w, ffn_conv_b, w_down, g_post_ffn):
    depth = w_in.shape[0]
    assert depth == 1
    l = 0
    seq = x_prompt.shape[1]
    nb, tn = x_sample.shape[0], x_sample.shape[1]
    row = lambda a: a.reshape(1, -1)
    lane = np.arange(LANES)
    chan = np.arange(RWKV_DIM)
    ones_bd = jnp.asarray(lane[:, None] // HEAD_DIM == lane[None, :] // HEAD_DIM, BF16)
    hsum_nat = jnp.asarray((chan // HEAD_DIM)[:, None] == (lane // V_LO)[None, :], BF16)
    hsum_scan = jnp.asarray((lane // V_LO)[:, None] == (lane // V_LO)[None, :], BF16)
    lw = dict(
        g_pre_mix=row(g_pre_mix[l]),
        w_in_t=_pack_in_proj_t(jnp.swapaxes(w_in[l], 0, 1)).astype(BF16),
        mu=_pack_rwkv_row(row(rwkv_mu[l])),
        w0=row(rwkv_w0[l]), a0=row(rwkv_a0[l]), k_k=row(rwkv_k_k[l]), k_a=row(rwkv_k_a[l]),
        r_k=row(rwkv_r_k[l]),
        w_lora=_pad_rows(rwkv_w_lora_b[l], LANES).astype(BF16),
        a_lora=_pad_rows(rwkv_a_lora_b[l], LANES).astype(BF16),
        g_lora=_pad_rows(rwkv_g_lora_b[l][:, SCAN_ORDER], 2 * LANES).astype(BF16),
        ones_bd=ones_bd, hsum_nat=hsum_nat, hsum_scan=hsum_scan,
        ln_g=row(rwkv_ln_g[l][SCAN_ORDER]), ln_b=row(rwkv_ln_b[l][SCAN_ORDER]),
        w_branch_a=w_branch_a[l][SCAN_ORDER, :].astype(BF16), w_branch_b=w_branch_b[l].astype(BF16),
        w_out=w_out[l].astype(BF16), g_post_mix=row(g_post_mix[l]),
        ffn=dict(g_pre=row(g_pre_ffn[l]), w_up=w_up[l].astype(BF16), w_down=w_down[l].astype(BF16),
                 conv_w=ffn_conv_w[l], conv_b=row(ffn_conv_b[l]), g_post=row(g_post_ffn[l])),
    )
    sinks = row(attn_sinks[l])

    mp = N_META + seq
    xp = jnp.concatenate([meta_tokens.astype(x_prompt.dtype), x_prompt[0]], axis=0)
    nchunk = seq // CHUNK
    tm_p = _pick_tile(mp, (912, 144, 48))
    tm_mix_p = _pick_tile(mp, (432, 144, 48))
    tb_prep_p = _pick_tile(mp, (144, 48))
    cos_p, sin_p = _rope_tables(jnp.arange(mp, dtype=jnp.int32))

    def attend_prompt(q, k, v):
        mk, mv = k[:N_META], v[:N_META]
        whole = lambda a: (a, (N_META, KV_DIM), lambda c: (0, 0))
        o_m = _attention(q[:N_META], sinks, [whole(mk)], [whole(mv)], N_META, 1, False)
        kf, vf = k[N_META:], v[N_META:]
        band = lambda a: [(a, (CHUNK, KV_DIM), lambda c, d=d: (jnp.maximum(c - d, 0), 0)) for d in (2, 1, 0)]
        o_f = _attention(q[N_META:], sinks, [whole(mk)] + band(kf), [whole(mv)] + band(vf),
                         CHUNK, nchunk, True)
        return jnp.concatenate([o_m, o_f], axis=0)

    s0_p = jnp.zeros((1, V_HI, HEAD_DIM, LANES), F32)
    scan_tb_p = _pick_tile(mp, (144, 48))
    xp2, p_p, k_p, v_p, sfin_p, tail_p = _trunk(
        xp, lw, tm=tm_p, tm_mix=tm_mix_p, tb_prep=tb_prep_p, shift_rows=jnp.zeros((1, 1, RWKV_PACK), F32),
        scan_tb=scan_tb_p, s0=s0_p, cos=cos_p, sin=sin_p,
        attend=attend_prompt, ffn_tm=tm_p, ffn_seq_len=None, conv_prev=None)

    ms = nb * tn
    xs = x_sample.reshape(ms, D_MODEL)
    pos_s = N_META + PAST_LEN + jnp.arange(tn, dtype=jnp.int32)
    cos_s, sin_s = _rope_tables(jnp.tile(pos_s, nb))
    nwin = cache_win_k.shape[2]
    cmk = cache_meta_k[l].reshape(nb, N_META, KV_DIM)
    cmv = cache_meta_v[l].reshape(nb, N_META, KV_DIM)
    cwk = cache_win_k[l].reshape(nb, nwin, KV_DIM)
    cwv = cache_win_v[l].reshape(nb, nwin, KV_DIM)

    shift_s = _pack_rwkv_row(state_shift[l].reshape(nb, RWKV_PROJ))[:, None, :]

    def attend_sample(q, k, v):
        per_b = lambda a, n: (a, (None, n, KV_DIM), lambda b: (b, 0, 0))
        own = lambda a: (a, (tn, KV_DIM), lambda b: (b, 0))
        return _attention(q, sinks, [per_b(cmk, N_META), per_b(cwk, nwin), own(k)],
                          [per_b(cmv, N_META), per_b(cwv, nwin), own(v)], tn, nb, False)

    xs2, p_s, k_s, v_s, sfin_s, tail_s = _trunk(
        xs, lw, tm=ms, tm_mix=ms, tb_prep=tn, shift_rows=shift_s, scan_tb=tn,
        s0=_state_to_scan(state_rwkv[l]), cos=cos_s, sin=sin_s, attend=attend_sample,
        ffn_tm=ms, ffn_seq_len=tn, conv_prev=state_ffn_conv[l])

    kv5 = lambda a, n: a.reshape(1, -1, n, KV_HEADS, HEAD_DIM)
    y_prompt = xp2[N_META:][None]
    y_sample = xs2.reshape(nb, tn, D_MODEL)
    prompt_meta_k = kv5(k_p[:N_META], N_META)
    prompt_meta_v = kv5(v_p[:N_META], N_META)
    prompt_win_k = kv5(k_p[mp - WINDOW:], WINDOW)
    prompt_win_v = kv5(v_p[mp - WINDOW:], WINDOW)
    prompt_rwkv = _state_from_scan(sfin_p)[None]
    prompt_shift = _unpack_rwkv_row(p_p[mp - 1:mp]).reshape(1, 1, 1, RWKV_PROJ)
    prompt_ffn_conv = tail_p[-1, SUBLANES - (FFN_CONV - 1):].reshape(1, 1, FFN_CONV - 1, D_FF)
    sample_win_k = kv5(k_s, tn)
    sample_win_v = kv5(v_s, tn)
    sample_rwkv = _state_from_scan(sfin_s)[None]
    sample_shift = _unpack_rwkv_row(p_s[tn - 1::tn]).reshape(1, nb, 1, RWKV_PROJ)
    sample_ffn_conv = tail_s[:, SUBLANES - (FFN_CONV - 1):].reshape(1, nb, FFN_CONV - 1, D_FF)
    return (y_prompt, y_sample, prompt_meta_k, prompt_meta_v, prompt_win_k, prompt_win_v,
            prompt_rwkv, prompt_shift, prompt_ffn_conv, sample_win_k, sample_win_v,
            sample_rwkv, sample_shift, sample_ffn_conv)
```

```python
import functools
import math

import numpy as np
import jax
import jax.numpy as jnp
from jax import lax
from jax.experimental import pallas as pl
from jax.experimental.pallas import tpu as pltpu

F32 = jnp.float32
BF16 = jnp.bfloat16

D_MODEL = 2048
CHUNK = 64
N_META = 16
NORM_EPS = 1e-6
HEADS = 16
HEAD_DIM = 64
RWKV_DIM = HEADS * HEAD_DIM
DECAY_LORA = 64
ICLR_LORA = 64
GATE_LORA = 160
RWKV_PROJ = 3 * RWKV_DIM + DECAY_LORA + ICLR_LORA + GATE_LORA
RWKV_LN_EPS = 64e-5
KV_HEADS = 2
GROUP = HEADS // KV_HEADS
Q_DIM = HEADS * HEAD_DIM
KV_DIM = KV_HEADS * HEAD_DIM
WINDOW = 128
ROPE_THETA = 10000.0
D_FF = 3 * D_MODEL
FFN_CONV = 3
PAST_LEN = 4096

SUBLANES = 8
LANES = 128
V_LO = LANES // HEADS
V_HI = HEAD_DIM // V_LO
K_BLOCKS = HEAD_DIM // SUBLANES

COL_R, COL_K, COL_V, COL_Q = 0, 1024, 2048, 3072
COL_GA, COL_GB = 4096, 6144
COL_LORA = 8192
LORA_W = 512
LORA_XW, LORA_XA, LORA_XG = 0, 128, 256
COL_KV = 8704
N_PACK = 8960
RWKV_PACK = 3 * RWKV_DIM + LORA_W

VMEM_LIMIT = 56 * 1024 * 1024


def _scan_order():
    p = np.arange(RWKV_DIM)
    vh, h, vl = p // LANES, (p % LANES) // V_LO, p % V_LO
    return h * HEAD_DIM + vh * V_LO + vl


SCAN_ORDER = _scan_order()
NATURAL_ORDER = np.argsort(SCAN_ORDER)


def _cparams(sem):
    return pltpu.CompilerParams(dimension_semantics=sem, vmem_limit_bytes=VMEM_LIMIT)


def _rms(x, g):
    return x * lax.rsqrt(jnp.mean(x * x, axis=-1, keepdims=True) + NORM_EPS) * g


def _split3(x):
    hi = x.astype(BF16)
    r1 = x - hi.astype(F32)
    mid = r1.astype(BF16)
    lo = (r1 - mid.astype(F32)).astype(BF16)
    return hi, mid, lo


def _dot3(x, mat):
    hi, mid, lo = _split3(x)
    acc = jnp.dot(hi, mat, preferred_element_type=F32)
    acc = acc + jnp.dot(mid, mat, preferred_element_type=F32)
    return acc + jnp.dot(lo, mat, preferred_element_type=F32)


def _seg_sum(x, ones_bd):
    hi, mid, lo = _split3(x)
    outs = []
    for j in range(x.shape[1] // LANES):
        sl = slice(j * LANES, (j + 1) * LANES)
        acc = jnp.dot(hi[:, sl], ones_bd, preferred_element_type=F32)
        acc = acc + jnp.dot(mid[:, sl], ones_bd, preferred_element_type=F32)
        acc = acc + jnp.dot(lo[:, sl], ones_bd, preferred_element_type=F32)
        outs.append(acc)
    return jnp.concatenate(outs, axis=1)


def _tile_lanes(x, n):
    return jnp.concatenate([x] * n, axis=1)


def _norm_matmul_kernel(x_ref, g_ref, w_ref, o_ref, h_ref):
    @pl.when(pl.program_id(1) == 0)
    def _():
        h_ref[...] = _rms(x_ref[...], g_ref[...]).astype(BF16)

    o_ref[...] = lax.dot_general(h_ref[...], w_ref[...], (((1,), (1,)), ((), ())),
                                 preferred_element_type=F32)


def _norm_matmul(x, g, w_t, tm, tn):
    m, d = x.shape
    n = w_t.shape[0]
    return pl.pallas_call(
        _norm_matmul_kernel,
        grid=(m // tm, n // tn),
        in_specs=[pl.BlockSpec((tm, d), lambda i, j: (i, 0)),
                  pl.BlockSpec((1, d), lambda i, j: (0, 0)),
                  pl.BlockSpec((tn, d), lambda i, j: (j, 0))],
        out_specs=pl.BlockSpec((tm, tn), lambda i, j: (i, j)),
        out_shape=jax.ShapeDtypeStruct((m, n), F32),
        scratch_shapes=[pltpu.VMEM((tm, d), BF16)],
        compiler_params=_cparams(("parallel", "arbitrary")),
        name="norm_matmul",
    )(x, g, w_t)


def _store_head_pairs(ref, a, b):
    lane = lax.broadcasted_iota(jnp.int32, (a.shape[0], LANES), 1)
    low = lane < HEAD_DIM
    for j in range(a.shape[1] // LANES):
        sl = slice(j * LANES, (j + 1) * LANES)
        aj, bj = a[:, sl], b[:, sl]
        ref[2 * j] = jnp.where(low, aj, pltpu.roll(bj, HEAD_DIM, 1))
        ref[2 * j + 1] = jnp.where(low, pltpu.roll(aj, HEAD_DIM, 1), bj)


def _rwkv_prep_kernel(r_ref, k_ref, v_ref, l_ref, hr_ref, hk_ref, hv_ref, hl_ref, state_ref,
                      mu_ref, w0_ref, a0_ref, kk_ref, ka_ref,
                      rk_ref, ww_ref, wa_ref, wg_ref, ones_ref, hsum_ref,
                      ab_ref, wk_ref, rr_ref, vo_ref, g_ref, bonus_ref, *, nblk):
    rows = r_ref.shape[0]
    first = lax.broadcasted_iota(jnp.int32, (rows, 1), 0) == 0
    seq_start = pl.program_id(0) % nblk == 0

    def shifted(x, halo_ref, lo, hi):
        before = jnp.where(seq_start, state_ref[:, lo:hi], halo_ref[SUBLANES - 1:SUBLANES, :])
        prev = jnp.where(first, before, pltpu.roll(x, 1, 0))
        return x + (prev - x) * mu_ref[:, lo:hi]

    r = shifted(r_ref[...], hr_ref, 0, RWKV_DIM)
    k = shifted(k_ref[...], hk_ref, RWKV_DIM, 2 * RWKV_DIM)
    v = shifted(v_ref[...], hv_ref, 2 * RWKV_DIM, 3 * RWKV_DIM)
    lora = shifted(l_ref[...], hl_ref, 3 * RWKV_DIM, RWKV_PACK)
    xw = lora[:, LORA_XW:LORA_XW + LANES]
    xa = lora[:, LORA_XA:LORA_XA + LANES]
    xg = lora[:, LORA_XG:LORA_XG + 2 * LANES]

    w_pre = w0_ref[...] + jnp.dot(jnp.tanh(xw).astype(BF16), ww_ref[...], preferred_element_type=F32)
    decay = jnp.exp(-math.exp(-0.5) * jax.nn.sigmoid(w_pre))
    a = jax.nn.sigmoid(a0_ref[...] + jnp.dot(xa.astype(BF16), wa_ref[...], preferred_element_type=F32))
    g = jnp.dot(jax.nn.sigmoid(xg).astype(BF16), wg_ref[...], preferred_element_type=F32)

    kk = k * kk_ref[...]
    kk = kk * lax.rsqrt(jnp.maximum(_seg_sum(kk * kk, ones_ref[...]), 1e-24))
    k_mod = k * (1.0 + (a - 1.0) * ka_ref[...])
    rk_sum = _dot3(r * k_mod * rk_ref[...], hsum_ref[...])

    _store_head_pairs(ab_ref, -kk, kk * a)
    _store_head_pairs(wk_ref, decay, k_mod)
    _store_head_pairs(rr_ref, r, r)
    for vh in range(V_HI):
        vo_ref[vh] = v[:, vh * LANES:(vh + 1) * LANES]
    g_ref[...] = g
    bonus_ref[...] = _tile_lanes(rk_sum, V_HI) * v


def _rwkv_prep(p_all, state_rows, pw, tb, nblk):
    m = p_all.shape[0]
    nt = m // tb
    assert tb % SUBLANES == 0 and nt == state_rows.shape[0] * nblk
    row = lambda c: pl.BlockSpec((tb, RWKV_DIM), lambda i, c=c: (i, c))
    before = lambda i: jnp.maximum(i * (tb // SUBLANES) - 1, 0)
    halo = lambda c: pl.BlockSpec((SUBLANES, RWKV_DIM), lambda i, c=c: (before(i), c))
    full = lambda a: pl.BlockSpec(a.shape, lambda i: (0,) * a.ndim)
    params = (pw["mu"], pw["w0"], pw["a0"], pw["k_k"], pw["k_a"], pw["r_k"],
              pw["w_lora"], pw["a_lora"], pw["g_lora"], pw["ones_bd"], pw["hsum_nat"])
    per_head = jax.ShapeDtypeStruct((HEADS, m, LANES), F32)
    per_vhi = jax.ShapeDtypeStruct((V_HI, m, LANES), F32)
    narrow = jax.ShapeDtypeStruct((m, RWKV_DIM), F32)
    return pl.pallas_call(
        functools.partial(_rwkv_prep_kernel, nblk=nblk),
        grid=(nt,),
        in_specs=[row(COL_R // RWKV_DIM), row(COL_K // RWKV_DIM), row(COL_V // RWKV_DIM),
                  pl.BlockSpec((tb, LORA_W), lambda i: (i, COL_LORA // LORA_W)),
                  halo(COL_R // RWKV_DIM), halo(COL_K // RWKV_DIM), halo(COL_V // RWKV_DIM),
                  pl.BlockSpec((SUBLANES, LORA_W), lambda i: (before(i), COL_LORA // LORA_W)),
                  pl.BlockSpec((None, 1, RWKV_PACK), lambda i: (i // nblk, 0, 0))]
                 + [full(a) for a in params],
        out_specs=[pl.BlockSpec((HEADS, tb, LANES), lambda i: (0, i, 0))] * 3
                  + [pl.BlockSpec((V_HI, tb, LANES), lambda i: (0, i, 0))]
                  + [pl.BlockSpec((tb, RWKV_DIM), lambda i: (i, 0))] * 2,
        out_shape=[per_head, per_head, per_head, per_vhi, narrow, narrow],
        compiler_params=_cparams(("parallel",)),
        name="rwkv_prep",
    )(p_all, p_all, p_all, p_all, p_all, p_all, p_all, p_all, state_rows, *params)


def _sublane_allsum(x):
    x = x + pltpu.roll(x, 4, 0)
    x = x + pltpu.roll(x, 2, 0)
    return x + pltpu.roll(x, 1, 0)


def _scan_kernel(ab_ref, wk_ref, rr_ref, v_ref, s0_ref, y_ref, st_ref, ops_ref, *, tb, nblk):
    @pl.when(pl.program_id(0) % nblk == 0)
    def _():
        st_ref[...] = s0_ref[...]

    def col_tiles(ref, t):
        rep = jnp.concatenate(
            [jnp.broadcast_to(ref[h, pl.ds(t, 1), :], (V_LO, LANES)) for h in range(HEADS)], axis=0)
        return rep.T

    def stage(t, slot):
        ab = col_tiles(ab_ref, t)
        wk = col_tiles(wk_ref, t)
        rr = col_tiles(rr_ref, t)
        ops_ref[slot, 0] = ab[:HEAD_DIM]
        ops_ref[slot, 1] = ab[HEAD_DIM:]
        ops_ref[slot, 2] = wk[:HEAD_DIM]
        ops_ref[slot, 3] = wk[HEAD_DIM:]
        ops_ref[slot, 4] = rr[:HEAD_DIM]

    def op_tile(slot, which, kb):
        return ops_ref[slot, which, kb * SUBLANES:(kb + 1) * SUBLANES, :]

    def step(t, slot):
        ksl = lambda kb: slice(kb * SUBLANES, (kb + 1) * SUBLANES)
        acc = [None] * V_HI
        for kb in range(K_BLOCKS):
            alpha = op_tile(slot, 0, kb)
            for vh in range(V_HI):
                p = st_ref[vh, ksl(kb), :] * alpha
                acc[vh] = p if acc[vh] is None else acc[vh] + p
        u = [_sublane_allsum(acc[vh]) for vh in range(V_HI)]
        vb = [jnp.broadcast_to(v_ref[vh, pl.ds(t, 1), :], (SUBLANES, LANES)) for vh in range(V_HI)]
        yacc = [None] * V_HI
        for kb in range(K_BLOCKS):
            beta = op_tile(slot, 1, kb)
            decay = op_tile(slot, 2, kb)
            kmod = op_tile(slot, 3, kb)
            rr = op_tile(slot, 4, kb)
            for vh in range(V_HI):
                s_new = st_ref[vh, ksl(kb), :] * decay + beta * u[vh] + kmod * vb[vh]
                st_ref[vh, ksl(kb), :] = s_new
                p = s_new * rr
                yacc[vh] = p if yacc[vh] is None else yacc[vh] + p
        for vh in range(V_HI):
            y_ref[vh, pl.ds(t, 1), :] = _sublane_allsum(yacc[vh])[0:1, :]

    stage(0, 0)

    def body(i, carry):
        base = pl.multiple_of(i * SUBLANES, SUBLANES)
        following = pl.multiple_of(jnp.minimum(base + SUBLANES, tb - SUBLANES), SUBLANES)
        for j in range(SUBLANES):
            nxt = base + (j + 1) if j + 1 < SUBLANES else following
            stage(nxt, (j + 1) % 2)
            step(base + j, j % 2)
        return carry

    lax.fori_loop(0, tb // SUBLANES, body, 0)


def _rwkv_scan(ab, wk, rr, v, s0, tb, nblk):
    t_total = v.shape[1]
    nseq = s0.shape[0]
    grid = t_total // tb
    assert grid == nseq * nblk and tb % SUBLANES == 0
    st_shape = (V_HI, HEAD_DIM, LANES)
    per_head = pl.BlockSpec((HEADS, tb, LANES), lambda i: (0, i, 0))
    per_vhi = pl.BlockSpec((V_HI, tb, LANES), lambda i: (0, i, 0))
    state = pl.BlockSpec((None,) + st_shape, lambda i: (i // nblk, 0, 0, 0))
    return pl.pallas_call(
        functools.partial(_scan_kernel, tb=tb, nblk=nblk),
        grid=(grid,),
        in_specs=[per_head, per_head, per_head, per_vhi, state],
        out_specs=[per_vhi, state],
        out_shape=[jax.ShapeDtypeStruct((V_HI, t_total, LANES), F32),
                   jax.ShapeDtypeStruct((nseq,) + st_shape, F32)],
        scratch_shapes=[pltpu.VMEM((2, 5, HEAD_DIM, LANES), F32)],
        compiler_params=_cparams(("arbitrary",)),
        name="rwkv_scan",
    )(ab, wk, rr, v, s0)


def _state_to_scan(s):
    b = s.shape[0]
    s = s.reshape(b, HEADS, V_HI, V_LO, HEAD_DIM)
    return s.transpose(0, 2, 4, 1, 3).reshape(b, V_HI, HEAD_DIM, LANES)


def _state_from_scan(st):
    b = st.shape[0]
    st = st.reshape(b, V_HI, HEAD_DIM, HEADS, V_LO)
    return st.transpose(0, 3, 1, 4, 2).reshape(b, HEADS, HEAD_DIM, HEAD_DIM)


def _rope_tile(x, cos, sin_signed):
    lane = lax.broadcasted_iota(jnp.int32, x.shape, 1)
    first_half = (lane % HEAD_DIM) < (HEAD_DIM // 2)
    rot = jnp.where(first_half, pltpu.roll(x, LANES - HEAD_DIM // 2, 1), pltpu.roll(x, HEAD_DIM // 2, 1))
    return x * cos + rot * sin_signed


def _rope_kernel(q_ref, kv_ref, cos_ref, sin_ref, qo_ref, ko_ref, vo_ref):
    cos = cos_ref[...]
    sin = sin_ref[...]
    for j in range(Q_DIM // LANES):
        sl = slice(j * LANES, (j + 1) * LANES)
        qo_ref[:, sl] = _rope_tile(q_ref[:, sl], cos, sin).astype(BF16)
    ko_ref[...] = _rope_tile(kv_ref[:, :KV_DIM], cos, sin)
    vo_ref[...] = kv_ref[:, KV_DIM:]


def _rope(p_all, cos, sin, tb):
    m = p_all.shape[0]
    return pl.pallas_call(
        _rope_kernel,
        grid=(m // tb,),
        in_specs=[pl.BlockSpec((tb, Q_DIM), lambda i: (i, COL_Q // Q_DIM)),
                  pl.BlockSpec((tb, 2 * KV_DIM), lambda i: (i, COL_KV // (2 * KV_DIM))),
                  pl.BlockSpec((tb, LANES), lambda i: (i, 0)),
                  pl.BlockSpec((tb, LANES), lambda i: (i, 0))],
        out_specs=[pl.BlockSpec((tb, Q_DIM), lambda i: (i, 0)),
                   pl.BlockSpec((tb, KV_DIM), lambda i: (i, 0)),
                   pl.BlockSpec((tb, KV_DIM), lambda i: (i, 0))],
        out_shape=[jax.ShapeDtypeStruct((m, Q_DIM), BF16),
                   jax.ShapeDtypeStruct((m, KV_DIM), F32),
                   jax.ShapeDtypeStruct((m, KV_DIM), F32)],
        compiler_params=_cparams(("parallel",)),
        name="rope",
    )(p_all, p_all, cos, sin)


def _rope_tables(pos):
    half = HEAD_DIM // 2
    inv_freq = ROPE_THETA ** (-jnp.arange(half, dtype=F32) / half)
    ang = pos.astype(F32)[:, None] * inv_freq[None, :]
    cos = jnp.cos(ang)
    sin = jnp.sin(ang)
    cos = jnp.concatenate([cos, cos, cos, cos], axis=1)
    sin = jnp.concatenate([-sin, sin, -sin, sin], axis=1)
    return cos, sin


PAIRS = GROUP // 2


def _attn_kernel(*refs, nseg, band_mask):
    q_ref, sink_ref = refs[0], refs[1]
    k_refs = refs[2:2 + nseg]
    v_refs = refs[2 + nseg:2 + 2 * nseg]
    o_ref = refs[2 + 2 * nseg]
    tq = q_ref.shape[0]

    k_all = jnp.concatenate([r[...] for r in k_refs], axis=0)
    v_all = jnp.concatenate([r[...] for r in v_refs], axis=0)
    nkeys = k_all.shape[0]
    low = lax.broadcasted_iota(jnp.int32, (nkeys, LANES), 1) < HEAD_DIM
    k_swap = pltpu.roll(k_all, HEAD_DIM, 1)
    v_swap = pltpu.roll(v_all, HEAD_DIM, 1)
    if band_mask:
        c = pl.program_id(0)
        col = lax.broadcasted_iota(jnp.int32, (1, nkeys), 1)
        first_valid = N_META + jnp.maximum(2 - c, 0) * CHUNK
        visible = (col < N_META) | (col >= first_valid)
        bias = jnp.where(visible, 0.0, -jnp.inf).astype(F32)
    scale = HEAD_DIM ** -0.5
    ones_kv = jnp.ones((nkeys, LANES), BF16)
    groups = []
    for kvh in range(KV_HEADS):
        k_src, k_oth = (k_all, k_swap) if kvh == 0 else (k_swap, k_all)
        v_src, v_oth = (v_all, v_swap) if kvh == 0 else (v_swap, v_all)
        halves = ((jnp.where(low, k_src, 0.0).astype(BF16), jnp.where(low, v_src, 0.0).astype(BF16)),
                  (jnp.where(low, 0.0, k_oth).astype(BF16), jnp.where(low, 0.0, v_oth).astype(BF16)))
        qp = jnp.concatenate(
            [q_ref[:, (kvh * PAIRS + p) * LANES:(kvh * PAIRS + p + 1) * LANES] for p in range(PAIRS)], axis=0)
        scores = []
        for kx, _ in halves:
            s = lax.dot_general(qp, kx, (((1,), (1,)), ((), ())), preferred_element_type=F32) * scale
            scores.append(s + bias if band_mask else s)
        total = None
        for parity, (_, vx) in enumerate(halves):
            sink = jnp.concatenate(
                [jnp.broadcast_to(sink_ref[:, kvh * GROUP + 2 * p + parity:kvh * GROUP + 2 * p + parity + 1],
                                  (tq, 1)) for p in range(PAIRS)], axis=0)
            s = scores[parity]
            m = jnp.maximum(jnp.max(s, axis=1, keepdims=True), sink)
            e = jnp.exp(s - m).astype(BF16)
            den = jnp.dot(e, ones_kv, preferred_element_type=F32) + jnp.exp(sink - m)
            o = jnp.dot(e, vx, preferred_element_type=F32) / den
            total = o if total is None else total + o
        groups.extend(total[p * tq:(p + 1) * tq] for p in range(PAIRS))
    o_ref[...] = jnp.concatenate(groups, axis=1).astype(BF16)


def _attention(q, sinks, k_segs, v_segs, tq, grid, band_mask):
    nseg = len(k_segs)
    seg_specs = [pl.BlockSpec(bs, im) for (_, bs, im) in k_segs + v_segs]
    return pl.pallas_call(
        functools.partial(_attn_kernel, nseg=nseg, band_mask=band_mask),
        grid=(grid,),
        in_specs=[pl.BlockSpec((tq, Q_DIM), lambda i: (i, 0)),
                  pl.BlockSpec((1, HEADS), lambda i: (0, 0))] + seg_specs,
        out_specs=pl.BlockSpec((tq, Q_DIM), lambda i: (i, 0)),
        out_shape=jax.ShapeDtypeStruct((grid * tq, Q_DIM), BF16),
        compiler_params=_cparams(("parallel",)),
        name="attention",
    )(q, sinks, *[a for (a, _, _) in k_segs + v_segs])


def _mix_kernel(y_ref, bonus_ref, g_ref, yb_ref, ga_ref, gb_ref, lng_ref, lnb_ref, hsum_ref,
                wa_ref, wb_ref, o_ref, ya_ref):
    hsum = hsum_ref[...]
    mean = _dot3(sum(y_ref[vh] for vh in range(V_HI)), hsum) * (1.0 / HEAD_DIM)
    d = [y_ref[vh] - mean for vh in range(V_HI)]
    var = _dot3(sum(dv * dv for dv in d), hsum) * (1.0 / HEAD_DIM)
    inv = lax.rsqrt(var + RWKV_LN_EPS)
    for vh in range(V_HI):
        sl = slice(vh * LANES, (vh + 1) * LANES)
        yn = d[vh] * inv * lng_ref[:, sl] + lnb_ref[:, sl]
        ya_ref[:, sl] = ((yn + bonus_ref[:, sl]) * g_ref[:, sl]).astype(BF16)

    pa = jnp.dot(ya_ref[...], wa_ref[...], preferred_element_type=F32)
    pb = jnp.dot(yb_ref[...], wb_ref[...], preferred_element_type=F32)
    o_ref[...] = (jax.nn.sigmoid(ga_ref[...]) * pa + jax.nn.sigmoid(gb_ref[...]) * pb).astype(BF16)


def _mix(y, bonus, g, yb, p_all, ln_g, ln_b, hsum, wa, wb, tm):
    m = bonus.shape[0]
    row = pl.BlockSpec((tm, RWKV_DIM), lambda i: (i, 0))
    one = lambda a: pl.BlockSpec(a.shape, lambda i: (0,) * a.ndim)
    resident = lambda a: pl.BlockSpec(a.shape, lambda i: (0, 0), pipeline_mode=pl.Buffered(1))
    return pl.pallas_call(
        _mix_kernel,
        grid=(m // tm,),
        in_specs=[pl.BlockSpec((V_HI, tm, LANES), lambda i: (0, i, 0)), row, row, row,
                  pl.BlockSpec((tm, D_MODEL), lambda i: (i, COL_GA // D_MODEL)),
                  pl.BlockSpec((tm, D_MODEL), lambda i: (i, COL_GB // D_MODEL)),
                  one(ln_g), one(ln_b), one(hsum), resident(wa), resident(wb)],
        out_specs=pl.BlockSpec((tm, D_MODEL), lambda i: (i, 0)),
        out_shape=jax.ShapeDtypeStruct((m, D_MODEL), BF16),
        scratch_shapes=[pltpu.VMEM((tm, RWKV_DIM), BF16)],
        compiler_params=_cparams(("parallel",)),
        name="branch_mix",
    )(y, bonus, g, yb, p_all, p_all, ln_g, ln_b, hsum, wa, wb)


def _out_norm_kernel(mix_ref, w_ref, x_ref, g_ref, o_ref):
    z = jnp.dot(mix_ref[...], w_ref[...], preferred_element_type=F32)
    o_ref[...] = x_ref[...] + _rms(z, g_ref[...])


def _out_norm(mixed, w, x, g, tm):
    m = x.shape[0]
    return pl.pallas_call(
        _out_norm_kernel,
        grid=(m // tm,),
        in_specs=[pl.BlockSpec((tm, D_MODEL), lambda i: (i, 0)),
                  pl.BlockSpec((D_MODEL, D_MODEL), lambda i: (0, 0)),
                  pl.BlockSpec((tm, D_MODEL), lambda i: (i, 0)),
                  pl.BlockSpec((1, D_MODEL), lambda i: (0, 0))],
        out_specs=pl.BlockSpec((tm, D_MODEL), lambda i: (i, 0)),
        out_shape=jax.ShapeDtypeStruct((m, D_MODEL), F32),
        compiler_params=_cparams(("parallel",)),
        name="out_norm",
    )(mixed, w, x, g)


HALO = 16


def _gelu_tanh(x):
    return 0.5 * x * (1.0 + jnp.tanh(0.7978845608028654 * (x + 0.044715 * x * x * x)))


def _ffn_kernel(*refs, seq_len):
    if seq_len is None:
        (x_ref, halo_ref, gpre_ref, wg_ref, wv_ref, wd_ref, cw_ref, cb_ref, gpost_ref,
         o_ref, tail_ref, h_ref) = refs
    else:
        (x_ref, sa_ref, sb_ref, gpre_ref, wg_ref, wv_ref, wd_ref, cw_ref, cb_ref, gpost_ref,
         o_ref, tail_ref, h_ref) = refs
    acc_ref = o_ref
    tm = x_ref.shape[0]
    j = pl.program_id(1)
    top = HALO if seq_len is None else 0

    @pl.when(j == 0)
    def _():
        if seq_len is None:
            halo = _rms(halo_ref[...], gpre_ref[...])
            halo = jnp.where(pl.program_id(0) == 0, 0.0, halo)
            h_ref[:HALO] = halo.astype(BF16)
        h_ref[top:] = _rms(x_ref[...], gpre_ref[...]).astype(BF16)
        acc_ref[...] = jnp.zeros_like(acc_ref)

    h_all = h_ref[...]
    gate_all = jnp.dot(h_all, wg_ref[...], preferred_element_type=F32)
    val = jnp.dot(h_all[top:], wv_ref[...], preferred_element_type=F32)
    g0 = gate_all[top:]
    g1 = pltpu.roll(gate_all, 1, 0)[top:]
    g2 = pltpu.roll(gate_all, 2, 0)[top:]
    if seq_len is not None:
        nseq = tm // seq_len
        tf = g0.shape[1]
        tau = lax.broadcasted_iota(jnp.int32, (tm, 1), 0) % seq_len
        expand = lambda s: jnp.concatenate(
            [jnp.broadcast_to(s[b:b + 1, :], (seq_len, tf)) for b in range(nseq)], axis=0)
        st_a = expand(sa_ref[...])
        st_b = expand(sb_ref[...])
        g1 = jnp.where(tau == 0, st_b, g1)
        g2 = jnp.where(tau == 0, st_a, jnp.where(tau == 1, st_b, g2))
    conv = cb_ref[...] + g2 * cw_ref[0:1, :] + g1 * cw_ref[1:2, :] + g0 * cw_ref[2:3, :]
    act = (_gelu_tanh(conv) * val).astype(BF16)
    acc_ref[...] += jnp.dot(act, wd_ref[...], preferred_element_type=F32)

    if seq_len is None:
        tail_ref[...] = g0[tm - SUBLANES:]
    else:
        for b in range(tm // seq_len):
            tail_ref[b] = g0[(b + 1) * seq_len - SUBLANES:(b + 1) * seq_len]

    @pl.when(j == pl.num_programs(1) - 1)
    def _():
        o_ref[...] = x_ref[...] + _rms(acc_ref[...], gpost_ref[...])


def _ffn(x1, conv_prev, fw, tm, tf, seq_len):
    m = x1.shape[0]
    nm, nf = m // tm, D_FF // tf
    one = lambda a: pl.BlockSpec(a.shape, lambda i, j: (0,) * a.ndim)
    xspec = pl.BlockSpec((tm, D_MODEL), lambda i, j: (i, 0))
    wspecs = [pl.BlockSpec((D_MODEL, tf), lambda i, j: (0, j)),
              pl.BlockSpec((D_MODEL, tf), lambda i, j: (0, D_FF // tf + j)),
              pl.BlockSpec((tf, D_MODEL), lambda i, j: (j, 0)),
              pl.BlockSpec((FFN_CONV, tf), lambda i, j: (0, j)),
              pl.BlockSpec((1, tf), lambda i, j: (0, j)),
              one(fw["g_post"])]
    wargs = (fw["w_up"], fw["w_up"], fw["w_down"], fw["conv_w"], fw["conv_b"], fw["g_post"])
    if seq_len is None:
        assert tm % HALO == 0
        pre_specs = [xspec,
                     pl.BlockSpec((HALO, D_MODEL), lambda i, j: (jnp.maximum(i * (tm // HALO) - 1, 0), 0)),
                     one(fw["g_pre"])]
        pre_args = (x1, x1, fw["g_pre"])
        tail_shape = (nm, SUBLANES, D_FF)
        tail_spec = pl.BlockSpec((None, SUBLANES, tf), lambda i, j: (i, 0, j))
        hrows = tm + HALO
    else:
        nseq = tm // seq_len
        assert nm == 1
        pre_specs = [xspec,
                     pl.BlockSpec((nseq, tf), lambda i, j: (0, j)),
                     pl.BlockSpec((nseq, tf), lambda i, j: (0, j)),
                     one(fw["g_pre"])]
        pre_args = (x1, conv_prev[:, 0, :], conv_prev[:, 1, :], fw["g_pre"])
        tail_shape = (nseq, SUBLANES, D_FF)
        tail_spec = pl.BlockSpec((nseq, SUBLANES, tf), lambda i, j: (0, 0, j))
        hrows = tm
    return pl.pallas_call(
        functools.partial(_ffn_kernel, seq_len=seq_len),
        grid=(nm, nf),
        in_specs=pre_specs + wspecs,
        out_specs=[pl.BlockSpec((tm, D_MODEL), lambda i, j: (i, 0), pipeline_mode=pl.Buffered(1)), tail_spec],
        out_shape=[jax.ShapeDtypeStruct((m, D_MODEL), F32),
                   jax.ShapeDtypeStruct(tail_shape, F32)],
        scratch_shapes=[pltpu.VMEM((hrows, D_MODEL), BF16)],
        compiler_params=_cparams(("parallel", "arbitrary")),
        name="conv_ffn",
    )(*pre_args, *wargs)


def _pack_lora(x, axis=1):
    cut = lambda a, b: lax.slice_in_dim(x, a, b, axis=axis)

    def z(n):
        shape = list(x.shape)
        shape[axis] = n
        return jnp.zeros(shape, x.dtype)

    xw = cut(0, DECAY_LORA)
    xa = cut(DECAY_LORA, DECAY_LORA + ICLR_LORA)
    xg = cut(DECAY_LORA + ICLR_LORA, DECAY_LORA + ICLR_LORA + GATE_LORA)
    return jnp.concatenate([xw, z(LORA_XA - DECAY_LORA), xa, z(LORA_XG - LORA_XA - ICLR_LORA),
                            xg, z(LORA_W - LORA_XG - GATE_LORA)], axis=axis)


def _unpack_lora(x):
    return jnp.concatenate([x[:, LORA_XW:LORA_XW + DECAY_LORA], x[:, LORA_XA:LORA_XA + ICLR_LORA],
                            x[:, LORA_XG:LORA_XG + GATE_LORA]], axis=1)


def _pack_in_proj_t(wt):
    c = 3 * RWKV_DIM
    q0 = RWKV_PROJ
    kv0 = q0 + Q_DIM
    g0 = kv0 + 2 * KV_DIM
    return jnp.concatenate([wt[:2 * RWKV_DIM], wt[2 * RWKV_DIM:c][SCAN_ORDER], wt[q0:kv0],
                            wt[g0:g0 + 2 * D_MODEL], _pack_lora(wt[c:RWKV_PROJ], axis=0), wt[kv0:g0]], axis=0)


def _pack_rwkv_row(x):
    c = 3 * RWKV_DIM
    return jnp.concatenate([x[:, :2 * RWKV_DIM], x[:, 2 * RWKV_DIM:c][:, SCAN_ORDER], _pack_lora(x[:, c:])],
                           axis=1)


def _unpack_rwkv_row(p_row):
    return jnp.concatenate([p_row[:, :2 * RWKV_DIM], p_row[:, COL_V:COL_V + RWKV_DIM][:, NATURAL_ORDER],
                            _unpack_lora(p_row[:, COL_LORA:COL_LORA + LORA_W])], axis=1)


def _pick_tile(m, candidates):
    for c in candidates:
        if m % c == 0:
            return c
    raise ValueError(f"no row tile in {candidates} divides {m}")


def _pad_rows(w, rows):
    return jnp.concatenate([w, jnp.zeros((rows - w.shape[0], w.shape[1]), w.dtype)], axis=0)


def _trunk(x, lw, *, tm, tm_mix, tb_prep, shift_rows, scan_tb, s0, cos, sin, attend, ffn_tm,
           ffn_seq_len, conv_prev):
    nseq = s0.shape[0]
    seq_rows = x.shape[0] // nseq
    scan_nblk = seq_rows // scan_tb
    p_all = _norm_matmul(x, lw["g_pre_mix"], lw["w_in_t"], tm, 1280)

    ab, wk, rr, v_scan, g, bonus = _rwkv_prep(p_all, shift_rows, lw, tb_prep, seq_rows // tb_prep)
    y, s_fin = _rwkv_scan(ab, wk, rr, v_scan, s0, scan_tb, scan_nblk)

    q_roped, k_roped, v_att = _rope(p_all, cos, sin, tm)
    att = attend(q_roped, k_roped, v_att)

    mixed = _mix(y, bonus, g, att, p_all, lw["ln_g"], lw["ln_b"], lw["hsum_scan"],
                 lw["w_branch_a"], lw["w_branch_b"], tm_mix)
    x1 = _out_norm(mixed, lw["w_out"], x, lw["g_post_mix"], tm_mix)
    x2, tail = _ffn(x1, conv_prev, lw["ffn"], ffn_tm, 512, ffn_seq_len)
    return x2, p_all, k_roped, v_att, s_fin, tail


def kernel(x_prompt, x_sample, cache_meta_k, cache_meta_v, cache_win_k, cache_win_v, state_rwkv, state_shift, state_ffn_conv, meta_tokens, g_pre_mix, w_in, rwkv_mu, rwkv_w0, rwkv_w_lora_b, rwkv_a0, rwkv_a_lora_b, rwkv_g_lora_b, rwkv_k_k, rwkv_k_a, rwkv_r_k, rwkv_ln_g, rwkv_ln_b, attn_sinks, w_branch_a, w_branch_b, w_out, g_post_mix, g_pre_ffn, w_up, ffn_conv_w, ffn_conv_b, w_down, g_post_ffn):
    depth = w_in.shape[0]
    assert depth == 1
    l = 0
    seq = x_prompt.shape[1]
    nb, tn = x_sample.shape[0], x_sample.shape[1]
    row = lambda a: a.reshape(1, -1)
    lane = np.arange(LANES)
    chan = np.arange(RWKV_DIM)
    ones_bd = jnp.asarray(lane[:, None] // HEAD_DIM == lane[None, :] // HEAD_DIM, BF16)
    hsum_nat = jnp.asarray((chan // HEAD_DIM)[:, None] == (lane // V_LO)[None, :], BF16)
    hsum_scan = jnp.asarray((lane // V_LO)[:, None] == (lane // V_LO)[None, :], BF16)
    lw = dict(
        g_pre_mix=row(g_pre_mix[l]),
        w_in_t=_pack_in_proj_t(jnp.swapaxes(w_in[l], 0, 1)).astype(BF16),
        mu=_pack_rwkv_row(row(rwkv_mu[l])),
        w0=row(rwkv_w0[l]), a0=row(rwkv_a0[l]), k_k=row(rwkv_k_k[l]), k_a=row(rwkv_k_a[l]),
        r_k=row(rwkv_r_k[l]),
        w_lora=_pad_rows(rwkv_w_lora_b[l], LANES).astype(BF16),
        a_lora=_pad_rows(rwkv_a_lora_b[l], LANES).astype(BF16),
        g_lora=_pad_rows(rwkv_g_lora_b[l][:, SCAN_ORDER], 2 * LANES).astype(BF16),
        ones_bd=ones_bd, hsum_nat=hsum_nat, hsum_scan=hsum_scan,
        ln_g=row(rwkv_ln_g[l][SCAN_ORDER]), ln_b=row(rwkv_ln_b[l][SCAN_ORDER]),
        w_branch_a=w_branch_a[l][SCAN_ORDER, :].astype(BF16), w_branch_b=w_branch_b[l].astype(BF16),
        w_out=w_out[l].astype(BF16), g_post_mix=row(g_post_mix[l]),
        ffn=dict(g_pre=row(g_pre_ffn[l]), w_up=w_up[l].astype(BF16), w_down=w_down[l].astype(BF16),
                 conv_w=ffn_conv_w[l], conv_b=row(ffn_conv_b[l]), g_post=row(g_post_ffn[l])),
    )
    sinks = row(attn_sinks[l])

    mp = N_META + seq
    xp = jnp.concatenate([meta_tokens.astype(x_prompt.dtype), x_prompt[0]], axis=0)
    nchunk = seq // CHUNK
    tm_p = _pick_tile(mp, (912, 144, 48))
    tm_mix_p = _pick_tile(mp, (432, 144, 48))
    tb_prep_p = _pick_tile(mp, (144, 48))
    cos_p, sin_p = _rope_tables(jnp.arange(mp, dtype=jnp.int32))

    def attend_prompt(q, k, v):
        mk, mv = k[:N_META], v[:N_META]
        whole = lambda a: (a, (N_META, KV_DIM), lambda c: (0, 0))
        o_m = _attention(q[:N_META], sinks, [whole(mk)], [whole(mv)], N_META, 1, False)
        kf, vf = k[N_META:], v[N_META:]
        band = lambda a: [(a, (CHUNK, KV_DIM), lambda c, d=d: (jnp.maximum(c - d, 0), 0)) for d in (2, 1, 0)]
        o_f = _attention(q[N_META:], sinks, [whole(mk)] + band(kf), [whole(mv)] + band(vf),
                         CHUNK, nchunk, True)
        return jnp.concatenate([o_m, o_f], axis=0)

    s0_p = jnp.zeros((1, V_HI, HEAD_DIM, LANES), F32)
    scan_tb_p = _pick_tile(mp, (432, 144, 48))
    xp2, p_p, k_p, v_p, sfin_p, tail_p = _trunk(
        xp, lw, tm=tm_p, tm_mix=tm_mix_p, tb_prep=tb_prep_p, shift_rows=jnp.zeros((1, 1, RWKV_PACK), F32),
        scan_tb=scan_tb_p, s0=s0_p, cos=cos_p, sin=sin_p,
        attend=attend_prompt, ffn_tm=tm_p, ffn_seq_len=None, conv_prev=None)

    ms = nb * tn
    xs = x_sample.reshape(ms, D_MODEL)
    pos_s = N_META + PAST_LEN + jnp.arange(tn, dtype=jnp.int32)
    cos_s, sin_s = _rope_tables(jnp.tile(pos_s, nb))
    nwin = cache_win_k.shape[2]
    cmk = cache_meta_k[l].reshape(nb, N_META, KV_DIM)
    cmv = cache_meta_v[l].reshape(nb, N_META, KV_DIM)
    cwk = cache_win_k[l].reshape(nb, nwin, KV_DIM)
    cwv = cache_win_v[l].reshape(nb, nwin, KV_DIM)

    shift_s = _pack_rwkv_row(state_shift[l].reshape(nb, RWKV_PROJ))[:, None, :]

    def attend_sample(q, k, v):
        per_b = lambda a, n: (a, (None, n, KV_DIM), lambda b: (b, 0, 0))
        own = lambda a: (a, (tn, KV_DIM), lambda b: (b, 0))
        return _attention(q, sinks, [per_b(cmk, N_META), per_b(cwk, nwin), own(k)],
                          [per_b(cmv, N_META), per_b(cwv, nwin), own(v)], tn, nb, False)

    xs2, p_s, k_s, v_s, sfin_s, tail_s = _trunk(
        xs, lw, tm=ms, tm_mix=ms, tb_prep=tn, shift_rows=shift_s, scan_tb=tn,
        s0=_state_to_scan(state_rwkv[l]), cos=cos_s, sin=sin_s, attend=attend_sample,
        ffn_tm=ms, ffn_seq_len=tn, conv_prev=state_ffn_conv[l])

    kv5 = lambda a, n: a.reshape(1, -1, n, KV_HEADS, HEAD_DIM)
    y_prompt = xp2[N_META:][None]
    y_sample = xs2.reshape(nb, tn, D_MODEL)
    prompt_meta_k = kv5(k_p[:N_META], N_META)
    prompt_meta_v = kv5(v_p[:N_META], N_META)
    prompt_win_k = kv5(k_p[mp - WINDOW:], WINDOW)
    prompt_win_v = kv5(v_p[mp - WINDOW:], WINDOW)
    prompt_rwkv = _state_from_scan(sfin_p)[None]
    prompt_shift = _unpack_rwkv_row(p_p[mp - 1:mp]).reshape(1, 1, 1, RWKV_PROJ)
    prompt_ffn_conv = tail_p[-1, SUBLANES - (FFN_CONV - 1):].reshape(1, 1, FFN_CONV - 1, D_FF)
    sample_win_k = kv5(k_s, tn)
    sample_win_v = kv5(v_s, tn)
    sample_rwkv = _state_from_scan(sfin_s)[None]
    sample_shift = _unpack_rwkv_row(p_s[tn - 1::tn]).reshape(1, nb, 1, RWKV_PROJ)
    sample_ffn_conv = tail_s[:, SUBLANES - (FFN_CONV - 1):].reshape(1, nb, FFN_CONV - 1, D_FF)
    return (y_prompt, y_sample, prompt_meta_k, prompt_meta_v, prompt_win_k, prompt_win_v,
            prompt_rwkv, prompt_shift, prompt_ffn_conv, sample_win_k, sample_win_v,
            sample_rwkv, sample_shift, sample_ffn_conv)
```

```python
import functools
import math

import numpy as np
import jax
import jax.numpy as jnp
from jax import lax
from jax.experimental import pallas as pl
from jax.experimental.pallas import tpu as pltpu

F32 = jnp.float32
BF16 = jnp.bfloat16

D_MODEL = 2048
CHUNK = 64
N_META = 16
NORM_EPS = 1e-6
HEADS = 16
HEAD_DIM = 64
RWKV_DIM = HEADS * HEAD_DIM
DECAY_LORA = 64
ICLR_LORA = 64
GATE_LORA = 160
RWKV_PROJ = 3 * RWKV_DIM + DECAY_LORA + ICLR_LORA + GATE_LORA
RWKV_LN_EPS = 64e-5
KV_HEADS = 2
GROUP = HEADS // KV_HEADS
Q_DIM = HEADS * HEAD_DIM
KV_DIM = KV_HEADS * HEAD_DIM
WINDOW = 128
ROPE_THETA = 10000.0
D_FF = 3 * D_MODEL
FFN_CONV = 3
PAST_LEN = 4096

SUBLANES = 8
LANES = 128
V_LO = LANES // HEADS
V_HI = HEAD_DIM // V_LO
K_BLOCKS = HEAD_DIM // SUBLANES
SCAN_BLOCK = SUBLANES

COL_R, COL_K, COL_V, COL_Q = 0, 1024, 2048, 3072
COL_GA, COL_GB = 4096, 6144
COL_LORA = 8192
LORA_W = 512
LORA_XW, LORA_XA, LORA_XG = 0, 128, 256
COL_KV = 8704
N_PACK = 8960
RWKV_PACK = 3 * RWKV_DIM + LORA_W

VMEM_LIMIT = 56 * 1024 * 1024


def _scan_order():
    p = np.arange(RWKV_DIM)
    vh, h, vl = p // LANES, (p % LANES) // V_LO, p % V_LO
    return h * HEAD_DIM + vh * V_LO + vl


SCAN_ORDER = _scan_order()
NATURAL_ORDER = np.argsort(SCAN_ORDER)


def _cparams(sem):
    return pltpu.CompilerParams(dimension_semantics=sem, vmem_limit_bytes=VMEM_LIMIT)


def _rms(x, g):
    return x * lax.rsqrt(jnp.mean(x * x, axis=-1, keepdims=True) + NORM_EPS) * g


def _split3(x):
    hi = x.astype(BF16)
    r1 = x - hi.astype(F32)
    mid = r1.astype(BF16)
    lo = (r1 - mid.astype(F32)).astype(BF16)
    return hi, mid, lo


def _dot3(x, mat):
    hi, mid, lo = _split3(x)
    acc = jnp.dot(hi, mat, preferred_element_type=F32)
    acc = acc + jnp.dot(mid, mat, preferred_element_type=F32)
    return acc + jnp.dot(lo, mat, preferred_element_type=F32)


def _seg_sum(x, ones_bd):
    hi, mid, lo = _split3(x)
    outs = []
    for j in range(x.shape[1] // LANES):
        sl = slice(j * LANES, (j + 1) * LANES)
        acc = jnp.dot(hi[:, sl], ones_bd, preferred_element_type=F32)
        acc = acc + jnp.dot(mid[:, sl], ones_bd, preferred_element_type=F32)
        acc = acc + jnp.dot(lo[:, sl], ones_bd, preferred_element_type=F32)
        outs.append(acc)
    return jnp.concatenate(outs, axis=1)


def _tile_lanes(x, n):
    return jnp.concatenate([x] * n, axis=1)


def _norm_matmul_kernel(x_ref, g_ref, w_ref, o_ref, h_ref):
    @pl.when(pl.program_id(1) == 0)
    def _():
        h_ref[...] = _rms(x_ref[...], g_ref[...]).astype(BF16)

    o_ref[...] = lax.dot_general(h_ref[...], w_ref[...], (((1,), (1,)), ((), ())),
                                 preferred_element_type=F32)


def _norm_matmul(x, g, w_t, tm, tn):
    m, d = x.shape
    n = w_t.shape[0]
    return pl.pallas_call(
        _norm_matmul_kernel,
        grid=(m // tm, n // tn),
        in_specs=[pl.BlockSpec((tm, d), lambda i, j: (i, 0)),
                  pl.BlockSpec((1, d), lambda i, j: (0, 0)),
                  pl.BlockSpec((tn, d), lambda i, j: (j, 0))],
        out_specs=pl.BlockSpec((tm, tn), lambda i, j: (i, j)),
        out_shape=jax.ShapeDtypeStruct((m, n), F32),
        scratch_shapes=[pltpu.VMEM((tm, d), BF16)],
        compiler_params=_cparams(("parallel", "arbitrary")),
        name="norm_matmul",
    )(x, g, w_t)


def _store_head_pairs(ref, a, b):
    lane = lax.broadcasted_iota(jnp.int32, (a.shape[0], LANES), 1)
    low = lane < HEAD_DIM
    for j in range(a.shape[1] // LANES):
        sl = slice(j * LANES, (j + 1) * LANES)
        aj, bj = a[:, sl], b[:, sl]
        ref[2 * j] = jnp.where(low, aj, pltpu.roll(bj, HEAD_DIM, 1))
        ref[2 * j + 1] = jnp.where(low, pltpu.roll(aj, HEAD_DIM, 1), bj)


def _rwkv_prep_kernel(r_ref, k_ref, v_ref, l_ref, hr_ref, hk_ref, hv_ref, hl_ref, state_ref,
                      mu_ref, w0_ref, a0_ref, kk_ref, ka_ref,
                      rk_ref, ww_ref, wa_ref, wg_ref, ones_ref, hsum_ref,
                      ab_ref, kr_ref, cum_ref, vo_ref, g_ref, bonus_ref, *, nblk):
    rows = r_ref.shape[0]
    first = lax.broadcasted_iota(jnp.int32, (rows, 1), 0) == 0
    seq_start = pl.program_id(0) % nblk == 0

    def shifted(x, halo_ref, lo, hi):
        before = jnp.where(seq_start, state_ref[:, lo:hi], halo_ref[SUBLANES - 1:SUBLANES, :])
        prev = jnp.where(first, before, pltpu.roll(x, 1, 0))
        return x + (prev - x) * mu_ref[:, lo:hi]

    r = shifted(r_ref[...], hr_ref, 0, RWKV_DIM)
    k = shifted(k_ref[...], hk_ref, RWKV_DIM, 2 * RWKV_DIM)
    v = shifted(v_ref[...], hv_ref, 2 * RWKV_DIM, 3 * RWKV_DIM)
    lora = shifted(l_ref[...], hl_ref, 3 * RWKV_DIM, RWKV_PACK)
    xw = lora[:, LORA_XW:LORA_XW + LANES]
    xa = lora[:, LORA_XA:LORA_XA + LANES]
    xg = lora[:, LORA_XG:LORA_XG + 2 * LANES]

    w_pre = w0_ref[...] + jnp.dot(jnp.tanh(xw).astype(BF16), ww_ref[...], preferred_element_type=F32)
    log_decay = -math.exp(-0.5) * jax.nn.sigmoid(w_pre)
    step_in_block = lax.broadcasted_iota(jnp.int32, (rows, 1), 0) % SCAN_BLOCK
    log_cum = log_decay
    shift = 1
    while shift < SCAN_BLOCK:
        log_cum = log_cum + jnp.where(step_in_block >= shift, pltpu.roll(log_cum, shift, 0), 0.0)
        shift *= 2
    cum = jnp.exp(log_cum)
    cum_before = jnp.exp(log_cum - log_decay)
    inv_cum = jnp.exp(-log_cum)
    a = jax.nn.sigmoid(a0_ref[...] + jnp.dot(xa.astype(BF16), wa_ref[...], preferred_element_type=F32))
    g = jnp.dot(jax.nn.sigmoid(xg).astype(BF16), wg_ref[...], preferred_element_type=F32)

    kk = k * kk_ref[...]
    kk = kk * lax.rsqrt(jnp.maximum(_seg_sum(kk * kk, ones_ref[...]), 1e-24))
    k_mod = k * (1.0 + (a - 1.0) * ka_ref[...])
    rk_sum = _dot3(r * k_mod * rk_ref[...], hsum_ref[...])

    _store_head_pairs(ab_ref, -kk * cum_before, kk * a * inv_cum)
    _store_head_pairs(kr_ref, k_mod * inv_cum, r * cum)
    _store_head_pairs(cum_ref, cum, cum)
    for vh in range(V_HI):
        vo_ref[vh] = v[:, vh * LANES:(vh + 1) * LANES]
    g_ref[...] = g
    bonus_ref[...] = _tile_lanes(rk_sum, V_HI) * v


def _rwkv_prep(p_all, state_rows, pw, tb, nblk):
    m = p_all.shape[0]
    nt = m // tb
    assert tb % SUBLANES == 0 and nt == state_rows.shape[0] * nblk
    row = lambda c: pl.BlockSpec((tb, RWKV_DIM), lambda i, c=c: (i, c))
    before = lambda i: jnp.maximum(i * (tb // SUBLANES) - 1, 0)
    halo = lambda c: pl.BlockSpec((SUBLANES, RWKV_DIM), lambda i, c=c: (before(i), c))
    full = lambda a: pl.BlockSpec(a.shape, lambda i: (0,) * a.ndim)
    params = (pw["mu"], pw["w0"], pw["a0"], pw["k_k"], pw["k_a"], pw["r_k"],
              pw["w_lora"], pw["a_lora"], pw["g_lora"], pw["ones_bd"], pw["hsum_nat"])
    per_head = jax.ShapeDtypeStruct((HEADS, m, LANES), F32)
    per_vhi = jax.ShapeDtypeStruct((V_HI, m, LANES), F32)
    narrow = jax.ShapeDtypeStruct((m, RWKV_DIM), F32)
    return pl.pallas_call(
        functools.partial(_rwkv_prep_kernel, nblk=nblk),
        grid=(nt,),
        in_specs=[row(COL_R // RWKV_DIM), row(COL_K // RWKV_DIM), row(COL_V // RWKV_DIM),
                  pl.BlockSpec((tb, LORA_W), lambda i: (i, COL_LORA // LORA_W)),
                  halo(COL_R // RWKV_DIM), halo(COL_K // RWKV_DIM), halo(COL_V // RWKV_DIM),
                  pl.BlockSpec((SUBLANES, LORA_W), lambda i: (before(i), COL_LORA // LORA_W)),
                  pl.BlockSpec((None, 1, RWKV_PACK), lambda i: (i // nblk, 0, 0))]
                 + [full(a) for a in params],
        out_specs=[pl.BlockSpec((HEADS, tb, LANES), lambda i: (0, i, 0))] * 3
                  + [pl.BlockSpec((V_HI, tb, LANES), lambda i: (0, i, 0))]
                  + [pl.BlockSpec((tb, RWKV_DIM), lambda i: (i, 0))] * 2,
        out_shape=[per_head, per_head, per_head, per_vhi, narrow, narrow],
        compiler_params=_cparams(("parallel",)),
        name="rwkv_prep",
    )(p_all, p_all, p_all, p_all, p_all, p_all, p_all, p_all, state_rows, *params)


def _sublane_allsum(x):
    x = x + pltpu.roll(x, 4, 0)
    x = x + pltpu.roll(x, 2, 0)
    return x + pltpu.roll(x, 1, 0)


def _scan_kernel(ab_ref, kr_ref, cum_ref, v_ref, s0_ref, y_ref, st_ref, ops_ref, blk_ref, *, tb, nblk):
    @pl.when(pl.program_id(0) % nblk == 0)
    def _():
        st_ref[...] = s0_ref[...]

    def col_tiles(ref, t):
        rep = jnp.concatenate(
            [jnp.broadcast_to(ref[h, pl.ds(t, 1), :], (V_LO, LANES)) for h in range(HEADS)], axis=0)
        return rep.T

    def stage(t, slot):
        ab = col_tiles(ab_ref, t)
        kr = col_tiles(kr_ref, t)
        ops_ref[slot, 0] = ab[:HEAD_DIM]
        ops_ref[slot, 1] = ab[HEAD_DIM:]
        ops_ref[slot, 2] = kr[:HEAD_DIM]
        ops_ref[slot, 3] = kr[HEAD_DIM:]

    def op_tile(slot, which, kb):
        return ops_ref[slot, which, kb * SUBLANES:(kb + 1) * SUBLANES, :]

    def step(t, slot, block_end):
        ksl = lambda kb: slice(kb * SUBLANES, (kb + 1) * SUBLANES)
        acc = [None] * V_HI
        for kb in range(K_BLOCKS):
            alpha = op_tile(slot, 0, kb)
            for vh in range(V_HI):
                p = st_ref[vh, ksl(kb), :] * alpha
                acc[vh] = p if acc[vh] is None else acc[vh] + p
        u = [_sublane_allsum(acc[vh]) for vh in range(V_HI)]
        vb = [jnp.broadcast_to(v_ref[vh, pl.ds(t, 1), :], (SUBLANES, LANES)) for vh in range(V_HI)]
        yacc = [None] * V_HI
        for kb in range(K_BLOCKS):
            beta = op_tile(slot, 1, kb)
            kmod = op_tile(slot, 2, kb)
            rr = op_tile(slot, 3, kb)
            for vh in range(V_HI):
                s_new = st_ref[vh, ksl(kb), :] + beta * u[vh] + kmod * vb[vh]
                p = s_new * rr
                yacc[vh] = p if yacc[vh] is None else yacc[vh] + p
                if block_end:
                    s_new = s_new * blk_ref[ksl(kb), :]
                st_ref[vh, ksl(kb), :] = s_new
        for vh in range(V_HI):
            y_ref[vh, pl.ds(t, 1), :] = _sublane_allsum(yacc[vh])[0:1, :]

    stage(0, 0)

    def body(i, carry):
        base = pl.multiple_of(i * SCAN_BLOCK, SCAN_BLOCK)
        following = pl.multiple_of(jnp.minimum(base + SCAN_BLOCK, tb - SCAN_BLOCK), SCAN_BLOCK)
        blk_ref[...] = col_tiles(cum_ref, base + (SCAN_BLOCK - 1))[:HEAD_DIM]
        for j in range(SCAN_BLOCK):
            nxt = base + (j + 1) if j + 1 < SCAN_BLOCK else following
            stage(nxt, (j + 1) % 2)
            step(base + j, j % 2, j + 1 == SCAN_BLOCK)
        return carry

    lax.fori_loop(0, tb // SCAN_BLOCK, body, 0)


def _rwkv_scan(ab, kr, cum, v, s0, tb, nblk):
    t_total = v.shape[1]
    nseq = s0.shape[0]
    grid = t_total // tb
    assert grid == nseq * nblk and tb % SCAN_BLOCK == 0 and SCAN_BLOCK % SUBLANES == 0
    st_shape = (V_HI, HEAD_DIM, LANES)
    per_head = pl.BlockSpec((HEADS, tb, LANES), lambda i: (0, i, 0))
    per_vhi = pl.BlockSpec((V_HI, tb, LANES), lambda i: (0, i, 0))
    state = pl.BlockSpec((None,) + st_shape, lambda i: (i // nblk, 0, 0, 0))
    return pl.pallas_call(
        functools.partial(_scan_kernel, tb=tb, nblk=nblk),
        grid=(grid,),
        in_specs=[per_head, per_head, per_head, per_vhi, state],
        out_specs=[per_vhi, state],
        out_shape=[jax.ShapeDtypeStruct((V_HI, t_total, LANES), F32),
                   jax.ShapeDtypeStruct((nseq,) + st_shape, F32)],
        scratch_shapes=[pltpu.VMEM((2, 4, HEAD_DIM, LANES), F32), pltpu.VMEM((HEAD_DIM, LANES), F32)],
        compiler_params=_cparams(("arbitrary",)),
        name="rwkv_scan",
    )(ab, kr, cum, v, s0)


def _state_to_scan(s):
    b = s.shape[0]
    s = s.reshape(b, HEADS, V_HI, V_LO, HEAD_DIM)
    return s.transpose(0, 2, 4, 1, 3).reshape(b, V_HI, HEAD_DIM, LANES)


def _state_from_scan(st):
    b = st.shape[0]
    st = st.reshape(b, V_HI, HEAD_DIM, HEADS, V_LO)
    return st.transpose(0, 3, 1, 4, 2).reshape(b, HEADS, HEAD_DIM, HEAD_DIM)


def _rope_tile(x, cos, sin_signed):
    lane = lax.broadcasted_iota(jnp.int32, x.shape, 1)
    first_half = (lane % HEAD_DIM) < (HEAD_DIM // 2)
    rot = jnp.where(first_half, pltpu.roll(x, LANES - HEAD_DIM // 2, 1), pltpu.roll(x, HEAD_DIM // 2, 1))
    return x * cos + rot * sin_signed


def _rope_kernel(q_ref, kv_ref, cos_ref, sin_ref, qo_ref, ko_ref, vo_ref):
    cos = cos_ref[...]
    sin = sin_ref[...]
    for j in range(Q_DIM // LANES):
        sl = slice(j * LANES, (j + 1) * LANES)
        qo_ref[:, sl] = _rope_tile(q_ref[:, sl], cos, sin).astype(BF16)
    ko_ref[...] = _rope_tile(kv_ref[:, :KV_DIM], cos, sin)
    vo_ref[...] = kv_ref[:, KV_DIM:]


def _rope(p_all, cos, sin, tb):
    m = p_all.shape[0]
    return pl.pallas_call(
        _rope_kernel,
        grid=(m // tb,),
        in_specs=[pl.BlockSpec((tb, Q_DIM), lambda i: (i, COL_Q // Q_DIM)),
                  pl.BlockSpec((tb, 2 * KV_DIM), lambda i: (i, COL_KV // (2 * KV_DIM))),
                  pl.BlockSpec((tb, LANES), lambda i: (i, 0)),
                  pl.BlockSpec((tb, LANES), lambda i: (i, 0))],
        out_specs=[pl.BlockSpec((tb, Q_DIM), lambda i: (i, 0)),
                   pl.BlockSpec((tb, KV_DIM), lambda i: (i, 0)),
                   pl.BlockSpec((tb, KV_DIM), lambda i: (i, 0))],
        out_shape=[jax.ShapeDtypeStruct((m, Q_DIM), BF16),
                   jax.ShapeDtypeStruct((m, KV_DIM), F32),
                   jax.ShapeDtypeStruct((m, KV_DIM), F32)],
        compiler_params=_cparams(("parallel",)),
        name="rope",
    )(p_all, p_all, cos, sin)


def _rope_tables(pos):
    half = HEAD_DIM // 2
    inv_freq = ROPE_THETA ** (-jnp.arange(half, dtype=F32) / half)
    ang = pos.astype(F32)[:, None] * inv_freq[None, :]
    cos = jnp.cos(ang)
    sin = jnp.sin(ang)
    cos = jnp.concatenate([cos, cos, cos, cos], axis=1)
    sin = jnp.concatenate([-sin, sin, -sin, sin], axis=1)
    return cos, sin


PAIRS = GROUP // 2


def _attn_kernel(*refs, nseg, band_mask):
    q_ref, sink_ref = refs[0], refs[1]
    k_refs = refs[2:2 + nseg]
    v_refs = refs[2 + nseg:2 + 2 * nseg]
    o_ref = refs[2 + 2 * nseg]
    tq = q_ref.shape[0]

    k_all = jnp.concatenate([r[...] for r in k_refs], axis=0)
    v_all = jnp.concatenate([r[...] for r in v_refs], axis=0)
    nkeys = k_all.shape[0]
    low = lax.broadcasted_iota(jnp.int32, (nkeys, LANES), 1) < HEAD_DIM
    k_swap = pltpu.roll(k_all, HEAD_DIM, 1)
    v_swap = pltpu.roll(v_all, HEAD_DIM, 1)
    if band_mask:
        c = pl.program_id(0)
        col = lax.broadcasted_iota(jnp.int32, (1, nkeys), 1)
        first_valid = N_META + jnp.maximum(2 - c, 0) * CHUNK
        visible = (col < N_META) | (col >= first_valid)
        bias = jnp.where(visible, 0.0, -jnp.inf).astype(F32)
    scale = HEAD_DIM ** -0.5
    ones_kv = jnp.ones((nkeys, LANES), BF16)
    groups = []
    for kvh in range(KV_HEADS):
        k_src, k_oth = (k_all, k_swap) if kvh == 0 else (k_swap, k_all)
        v_src, v_oth = (v_all, v_swap) if kvh == 0 else (v_swap, v_all)
        halves = ((jnp.where(low, k_src, 0.0).astype(BF16), jnp.where(low, v_src, 0.0).astype(BF16)),
                  (jnp.where(low, 0.0, k_oth).astype(BF16), jnp.where(low, 0.0, v_oth).astype(BF16)))
        qp = jnp.concatenate(
            [q_ref[:, (kvh * PAIRS + p) * LANES:(kvh * PAIRS + p + 1) * LANES] for p in range(PAIRS)], axis=0)
        scores = []
        for kx, _ in halves:
            s = lax.dot_general(qp, kx, (((1,), (1,)), ((), ())), preferred_element_type=F32) * scale
            scores.append(s + bias if band_mask else s)
        total = None
        for parity, (_, vx) in enumerate(halves):
            sink = jnp.concatenate(
                [jnp.broadcast_to(sink_ref[:, kvh * GROUP + 2 * p + parity:kvh * GROUP + 2 * p + parity + 1],
                                  (tq, 1)) for p in range(PAIRS)], axis=0)
            s = scores[parity]
            m = jnp.maximum(jnp.max(s, axis=1, keepdims=True), sink)
            e = jnp.exp(s - m).astype(BF16)
            den = jnp.dot(e, ones_kv, preferred_element_type=F32) + jnp.exp(sink - m)
            o = jnp.dot(e, vx, preferred_element_type=F32) / den
            total = o if total is None else total + o
        groups.extend(total[p * tq:(p + 1) * tq] for p in range(PAIRS))
    o_ref[...] = jnp.concatenate(groups, axis=1).astype(BF16)


def _attention(q, sinks, k_segs, v_segs, tq, grid, band_mask):
    nseg = len(k_segs)
    seg_specs = [pl.BlockSpec(bs, im) for (_, bs, im) in k_segs + v_segs]
    return pl.pallas_call(
        functools.partial(_attn_kernel, nseg=nseg, band_mask=band_mask),
        grid=(grid,),
        in_specs=[pl.BlockSpec((tq, Q_DIM), lambda i: (i, 0)),
                  pl.BlockSpec((1, HEADS), lambda i: (0, 0))] + seg_specs,
        out_specs=pl.BlockSpec((tq, Q_DIM), lambda i: (i, 0)),
        out_shape=jax.ShapeDtypeStruct((grid * tq, Q_DIM), BF16),
        compiler_params=_cparams(("parallel",)),
        name="attention",
    )(q, sinks, *[a for (a, _, _) in k_segs + v_segs])


def _mix_kernel(y_ref, bonus_ref, g_ref, yb_ref, ga_ref, gb_ref, lng_ref, lnb_ref, hsum_ref,
                wa_ref, wb_ref, o_ref, ya_ref):
    hsum = hsum_ref[...]
    mean = _dot3(sum(y_ref[vh] for vh in range(V_HI)), hsum) * (1.0 / HEAD_DIM)
    d = [y_ref[vh] - mean for vh in range(V_HI)]
    var = _dot3(sum(dv * dv for dv in d), hsum) * (1.0 / HEAD_DIM)
    inv = lax.rsqrt(var + RWKV_LN_EPS)
    for vh in range(V_HI):
        sl = slice(vh * LANES, (vh + 1) * LANES)
        yn = d[vh] * inv * lng_ref[:, sl] + lnb_ref[:, sl]
        ya_ref[:, sl] = ((yn + bonus_ref[:, sl]) * g_ref[:, sl]).astype(BF16)

    pa = jnp.dot(ya_ref[...], wa_ref[...], preferred_element_type=F32)
    pb = jnp.dot(yb_ref[...], wb_ref[...], preferred_element_type=F32)
    o_ref[...] = (jax.nn.sigmoid(ga_ref[...]) * pa + jax.nn.sigmoid(gb_ref[...]) * pb).astype(BF16)


def _mix(y, bonus, g, yb, p_all, ln_g, ln_b, hsum, wa, wb, tm):
    m = bonus.shape[0]
    row = pl.BlockSpec((tm, RWKV_DIM), lambda i: (i, 0))
    one = lambda a: pl.BlockSpec(a.shape, lambda i: (0,) * a.ndim)
    resident = lambda a: pl.BlockSpec(a.shape, lambda i: (0, 0), pipeline_mode=pl.Buffered(1))
    return pl.pallas_call(
        _mix_kernel,
        grid=(m // tm,),
        in_specs=[pl.BlockSpec((V_HI, tm, LANES), lambda i: (0, i, 0)), row, row, row,
                  pl.BlockSpec((tm, D_MODEL), lambda i: (i, COL_GA // D_MODEL)),
                  pl.BlockSpec((tm, D_MODEL), lambda i: (i, COL_GB // D_MODEL)),
                  one(ln_g), one(ln_b), one(hsum), resident(wa), resident(wb)],
        out_specs=pl.BlockSpec((tm, D_MODEL), lambda i: (i, 0)),
        out_shape=jax.ShapeDtypeStruct((m, D_MODEL), BF16),
        scratch_shapes=[pltpu.VMEM((tm, RWKV_DIM), BF16)],
        compiler_params=_cparams(("parallel",)),
        name="branch_mix",
    )(y, bonus, g, yb, p_all, p_all, ln_g, ln_b, hsum, wa, wb)


def _out_norm_kernel(mix_ref, w_ref, x_ref, g_ref, o_ref):
    z = jnp.dot(mix_ref[...], w_ref[...], preferred_element_type=F32)
    o_ref[...] = x_ref[...] + _rms(z, g_ref[...])


def _out_norm(mixed, w, x, g, tm):
    m = x.shape[0]
    return pl.pallas_call(
        _out_norm_kernel,
        grid=(m // tm,),
        in_specs=[pl.BlockSpec((tm, D_MODEL), lambda i: (i, 0)),
                  pl.BlockSpec((D_MODEL, D_MODEL), lambda i: (0, 0)),
                  pl.BlockSpec((tm, D_MODEL), lambda i: (i, 0)),
                  pl.BlockSpec((1, D_MODEL), lambda i: (0, 0))],
        out_specs=pl.BlockSpec((tm, D_MODEL), lambda i: (i, 0)),
        out_shape=jax.ShapeDtypeStruct((m, D_MODEL), F32),
        compiler_params=_cparams(("parallel",)),
        name="out_norm",
    )(mixed, w, x, g)


HALO = 16


def _gelu_tanh(x):
    return 0.5 * x * (1.0 + jnp.tanh(0.7978845608028654 * (x + 0.044715 * x * x * x)))


def _ffn_kernel(*refs, seq_len):
    if seq_len is None:
        (x_ref, halo_ref, gpre_ref, wg_ref, wv_ref, wd_ref, cw_ref, cb_ref, gpost_ref,
         o_ref, tail_ref, h_ref) = refs
    else:
        (x_ref, sa_ref, sb_ref, gpre_ref, wg_ref, wv_ref, wd_ref, cw_ref, cb_ref, gpost_ref,
         o_ref, tail_ref, h_ref) = refs
    acc_ref = o_ref
    tm = x_ref.shape[0]
    j = pl.program_id(1)
    top = HALO if seq_len is None else 0

    @pl.when(j == 0)
    def _():
        if seq_len is None:
            h_ref[:HALO] = _rms(halo_ref[...], gpre_ref[...]).astype(BF16)
        h_ref[top:] = _rms(x_ref[...], gpre_ref[...]).astype(BF16)
        acc_ref[...] = jnp.zeros_like(acc_ref)

    h_all = h_ref[...]
    gate_all = jnp.dot(h_all, wg_ref[...], preferred_element_type=F32)
    val = jnp.dot(h_all[top:], wv_ref[...], preferred_element_type=F32)
    g0 = gate_all[top:]
    g1 = pltpu.roll(gate_all, 1, 0)[top:]
    g2 = pltpu.roll(gate_all, 2, 0)[top:]
    if seq_len is not None:
        nseq = tm // seq_len
        tf = g0.shape[1]
        tau = lax.broadcasted_iota(jnp.int32, (tm, 1), 0) % seq_len
        expand = lambda s: jnp.concatenate(
            [jnp.broadcast_to(s[b:b + 1, :], (seq_len, tf)) for b in range(nseq)], axis=0)
        st_a = expand(sa_ref[...])
        st_b = expand(sb_ref[...])
        g1 = jnp.where(tau == 0, st_b, g1)
        g2 = jnp.where(tau == 0, st_a, jnp.where(tau == 1, st_b, g2))
    conv = cb_ref[...] + g2 * cw_ref[0:1, :] + g1 * cw_ref[1:2, :] + g0 * cw_ref[2:3, :]
    act = (_gelu_tanh(conv) * val).astype(BF16)
    acc_ref[...] += jnp.dot(act, wd_ref[...], preferred_element_type=F32)

    if seq_len is None:
        tail_ref[...] = g0[tm - SUBLANES:]
    else:
        for b in range(tm // seq_len):
            tail_ref[b] = g0[(b + 1) * seq_len - SUBLANES:(b + 1) * seq_len]

    @pl.when(j == pl.num_programs(1) - 1)
    def _():
        o_ref[...] = x_ref[...] + _rms(acc_ref[...], gpost_ref[...])


def _ffn(x1, conv_prev, fw, tm, tf, seq_len):
    m = x1.shape[0] - (HALO if seq_len is None else 0)
    nm, nf = m // tm, D_FF // tf
    one = lambda a: pl.BlockSpec(a.shape, lambda i, j: (0,) * a.ndim)
    wspecs = [pl.BlockSpec((D_MODEL, tf), lambda i, j: (0, j)),
              pl.BlockSpec((D_MODEL, tf), lambda i, j: (0, D_FF // tf + j)),
              pl.BlockSpec((tf, D_MODEL), lambda i, j: (j, 0)),
              pl.BlockSpec((FFN_CONV, tf), lambda i, j: (0, j)),
              pl.BlockSpec((1, tf), lambda i, j: (0, j)),
              one(fw["g_post"])]
    wargs = (fw["w_up"], fw["w_up"], fw["w_down"], fw["conv_w"], fw["conv_b"], fw["g_post"])
    if seq_len is None:
        assert tm % HALO == 0
        pre_specs = [pl.BlockSpec((pl.Element(tm), pl.Element(D_MODEL)),
                                  lambda i, j: (pl.multiple_of(HALO + i * tm, HALO), 0)),
                     pl.BlockSpec((HALO, D_MODEL), lambda i, j: (i * (tm // HALO), 0)),
                     one(fw["g_pre"])]
        pre_args = (x1, x1, fw["g_pre"])
        tail_shape = (nm, SUBLANES, D_FF)
        tail_spec = pl.BlockSpec((None, SUBLANES, tf), lambda i, j: (i, 0, j))
        hrows = tm + HALO
    else:
        nseq = tm // seq_len
        assert nm == 1
        pre_specs = [pl.BlockSpec((tm, D_MODEL), lambda i, j: (i, 0)),
                     pl.BlockSpec((nseq, tf), lambda i, j: (0, j)),
                     pl.BlockSpec((nseq, tf), lambda i, j: (0, j)),
                     one(fw["g_pre"])]
        pre_args = (x1, conv_prev[:, 0, :], conv_prev[:, 1, :], fw["g_pre"])
        tail_shape = (nseq, SUBLANES, D_FF)
        tail_spec = pl.BlockSpec((nseq, SUBLANES, tf), lambda i, j: (0, 0, j))
        hrows = tm
    return pl.pallas_call(
        functools.partial(_ffn_kernel, seq_len=seq_len),
        grid=(nm, nf),
        in_specs=pre_specs + wspecs,
        out_specs=[pl.BlockSpec((tm, D_MODEL), lambda i, j: (i, 0), pipeline_mode=pl.Buffered(1)), tail_spec],
        out_shape=[jax.ShapeDtypeStruct((m, D_MODEL), F32),
                   jax.ShapeDtypeStruct(tail_shape, F32)],
        scratch_shapes=[pltpu.VMEM((hrows, D_MODEL), BF16)],
        compiler_params=_cparams(("parallel", "arbitrary")),
        name="conv_ffn",
    )(*pre_args, *wargs)


def _pack_lora(x, axis=1):
    cut = lambda a, b: lax.slice_in_dim(x, a, b, axis=axis)

    def z(n):
        shape = list(x.shape)
        shape[axis] = n
        return jnp.zeros(shape, x.dtype)

    xw = cut(0, DECAY_LORA)
    xa = cut(DECAY_LORA, DECAY_LORA + ICLR_LORA)
    xg = cut(DECAY_LORA + ICLR_LORA, DECAY_LORA + ICLR_LORA + GATE_LORA)
    return jnp.concatenate([xw, z(LORA_XA - DECAY_LORA), xa, z(LORA_XG - LORA_XA - ICLR_LORA),
                            xg, z(LORA_W - LORA_XG - GATE_LORA)], axis=axis)


def _unpack_lora(x):
    return jnp.concatenate([x[:, LORA_XW:LORA_XW + DECAY_LORA], x[:, LORA_XA:LORA_XA + ICLR_LORA],
                            x[:, LORA_XG:LORA_XG + GATE_LORA]], axis=1)


def _pack_in_proj_t(wt):
    c = 3 * RWKV_DIM
    q0 = RWKV_PROJ
    kv0 = q0 + Q_DIM
    g0 = kv0 + 2 * KV_DIM
    return jnp.concatenate([wt[:2 * RWKV_DIM], wt[2 * RWKV_DIM:c][SCAN_ORDER], wt[q0:kv0],
                            wt[g0:g0 + 2 * D_MODEL], _pack_lora(wt[c:RWKV_PROJ], axis=0), wt[kv0:g0]], axis=0)


def _pack_rwkv_row(x):
    c = 3 * RWKV_DIM
    return jnp.concatenate([x[:, :2 * RWKV_DIM], x[:, 2 * RWKV_DIM:c][:, SCAN_ORDER], _pack_lora(x[:, c:])],
                           axis=1)


def _unpack_rwkv_row(p_row):
    return jnp.concatenate([p_row[:, :2 * RWKV_DIM], p_row[:, COL_V:COL_V + RWKV_DIM][:, NATURAL_ORDER],
                            _unpack_lora(p_row[:, COL_LORA:COL_LORA + LORA_W])], axis=1)


def _pick_tile(m, candidates):
    for c in candidates:
        if m % c == 0:
            return c
    raise ValueError(f"no row tile in {candidates} divides {m}")


def _pad_rows(w, rows):
    return jnp.concatenate([w, jnp.zeros((rows - w.shape[0], w.shape[1]), w.dtype)], axis=0)


def _trunk(x, lw, *, tm, tm_mix, tb_prep, shift_rows, scan_tb, s0, cos, sin, attend, ffn_tm,
           ffn_seq_len, conv_prev):
    nseq = s0.shape[0]
    seq_rows = x.shape[0] // nseq
    scan_nblk = seq_rows // scan_tb
    p_all = _norm_matmul(x, lw["g_pre_mix"], lw["w_in_t"], tm, 1280)

    ab, kr, cum, v_scan, g, bonus = _rwkv_prep(p_all, shift_rows, lw, tb_prep, seq_rows // tb_prep)
    y, s_fin = _rwkv_scan(ab, kr, cum, v_scan, s0, scan_tb, scan_nblk)

    q_roped, k_roped, v_att = _rope(p_all, cos, sin, tm)
    att = attend(q_roped, k_roped, v_att)

    mixed = _mix(y, bonus, g, att, p_all, lw["ln_g"], lw["ln_b"], lw["hsum_scan"],
                 lw["w_branch_a"], lw["w_branch_b"], tm_mix)
    x1 = _out_norm(mixed, lw["w_out"], x, lw["g_post_mix"], tm_mix)
    x2, tail = _ffn(x1, conv_prev, lw["ffn"], ffn_tm, 512, ffn_seq_len)
    return x2, p_all, k_roped, v_att, s_fin, tail


def kernel(x_prompt, x_sample, cache_meta_k, cache_meta_v, cache_win_k, cache_win_v, state_rwkv, state_shift, state_ffn_conv, meta_tokens, g_pre_mix, w_in, rwkv_mu, rwkv_w0, rwkv_w_lora_b, rwkv_a0, rwkv_a_lora_b, rwkv_g_lora_b, rwkv_k_k, rwkv_k_a, rwkv_r_k, rwkv_ln_g, rwkv_ln_b, attn_sinks, w_branch_a, w_branch_b, w_out, g_post_mix, g_pre_ffn, w_up, ffn_conv_w, ffn_conv_b, w_down, g_post_ffn):
    depth = w_in.shape[0]
    assert depth == 1
    l = 0
    seq = x_prompt.shape[1]
    nb, tn = x_sample.shape[0], x_sample.shape[1]
    row = lambda a: a.reshape(1, -1)
    lane = np.arange(LANES)
    chan = np.arange(RWKV_DIM)
    ones_bd = jnp.asarray(lane[:, None] // HEAD_DIM == lane[None, :] // HEAD_DIM, BF16)
    hsum_nat = jnp.asarray((chan // HEAD_DIM)[:, None] == (lane // V_LO)[None, :], BF16)
    hsum_scan = jnp.asarray((lane // V_LO)[:, None] == (lane // V_LO)[None, :], BF16)
    lw = dict(
        g_pre_mix=row(g_pre_mix[l]),
        w_in_t=_pack_in_proj_t(jnp.swapaxes(w_in[l], 0, 1)).astype(BF16),
        mu=_pack_rwkv_row(row(rwkv_mu[l])),
        w0=row(rwkv_w0[l]), a0=row(rwkv_a0[l]), k_k=row(rwkv_k_k[l]), k_a=row(rwkv_k_a[l]),
        r_k=row(rwkv_r_k[l]),
        w_lora=_pad_rows(rwkv_w_lora_b[l], LANES).astype(BF16),
        a_lora=_pad_rows(rwkv_a_lora_b[l], LANES).astype(BF16),
        g_lora=_pad_rows(rwkv_g_lora_b[l][:, SCAN_ORDER], 2 * LANES).astype(BF16),
        ones_bd=ones_bd, hsum_nat=hsum_nat, hsum_scan=hsum_scan,
        ln_g=row(rwkv_ln_g[l][SCAN_ORDER]), ln_b=row(rwkv_ln_b[l][SCAN_ORDER]),
        w_branch_a=w_branch_a[l][SCAN_ORDER, :].astype(BF16), w_branch_b=w_branch_b[l].astype(BF16),
        w_out=w_out[l].astype(BF16), g_post_mix=row(g_post_mix[l]),
        ffn=dict(g_pre=row(g_pre_ffn[l]), w_up=w_up[l].astype(BF16), w_down=w_down[l].astype(BF16),
                 conv_w=ffn_conv_w[l], conv_b=row(ffn_conv_b[l]), g_post=row(g_post_ffn[l])),
    )
    sinks = row(attn_sinks[l])

    mp = N_META + seq
    xp = jnp.concatenate([meta_tokens.astype(x_prompt.dtype), x_prompt[0]], axis=0)
    nchunk = seq // CHUNK
    tm_p = _pick_tile(mp, (912, 144, 48))
    tm_mix_p = _pick_tile(mp, (432, 144, 48))
    tb_prep_p = _pick_tile(mp, (144, 48))
    cos_p, sin_p = _rope_tables(jnp.arange(mp, dtype=jnp.int32))

    def attend_prompt(q, k, v):
        mk, mv = k[:N_META], v[:N_META]
        whole = lambda a: (a, (N_META, KV_DIM), lambda c: (0, 0))
        o_m = _attention(q[:N_META], sinks, [whole(mk)], [whole(mv)], N_META, 1, False)
        kf, vf = k[N_META:], v[N_META:]
        band = lambda a: [(a, (CHUNK, KV_DIM), lambda c, d=d: (jnp.maximum(c - d, 0), 0)) for d in (2, 1, 0)]
        o_f = _attention(q[N_META:], sinks, [whole(mk)] + band(kf), [whole(mv)] + band(vf),
                         CHUNK, nchunk, True)
        return jnp.concatenate([o_m, o_f], axis=0)

    s0_p = jnp.zeros((1, V_HI, HEAD_DIM, LANES), F32)
    scan_tb_p = _pick_tile(mp, (432, 144, 48))
    assert N_META == HALO
    yp, p_p, k_p, v_p, sfin_p, tail_p = _trunk(
        xp, lw, tm=tm_p, tm_mix=tm_mix_p, tb_prep=tb_prep_p, shift_rows=jnp.zeros((1, 1, RWKV_PACK), F32),
        scan_tb=scan_tb_p, s0=s0_p, cos=cos_p, sin=sin_p,
        attend=attend_prompt, ffn_tm=_pick_tile(seq, (1024, 64)), ffn_seq_len=None, conv_prev=None)

    ms = nb * tn
    xs = x_sample.reshape(ms, D_MODEL)
    pos_s = N_META + PAST_LEN + jnp.arange(tn, dtype=jnp.int32)
    cos_s, sin_s = _rope_tables(jnp.tile(pos_s, nb))
    nwin = cache_win_k.shape[2]
    cmk = cache_meta_k[l].reshape(nb, N_META, KV_DIM)
    cmv = cache_meta_v[l].reshape(nb, N_META, KV_DIM)
    cwk = cache_win_k[l].reshape(nb, nwin, KV_DIM)
    cwv = cache_win_v[l].reshape(nb, nwin, KV_DIM)

    shift_s = _pack_rwkv_row(state_shift[l].reshape(nb, RWKV_PROJ))[:, None, :]

    def attend_sample(q, k, v):
        per_b = lambda a, n: (a, (None, n, KV_DIM), lambda b: (b, 0, 0))
        own = lambda a: (a, (tn, KV_DIM), lambda b: (b, 0))
        return _attention(q, sinks, [per_b(cmk, N_META), per_b(cwk, nwin), own(k)],
                          [per_b(cmv, N_META), per_b(cwv, nwin), own(v)], tn, nb, False)

    xs2, p_s, k_s, v_s, sfin_s, tail_s = _trunk(
        xs, lw, tm=ms, tm_mix=ms, tb_prep=tn, shift_rows=shift_s, scan_tb=tn,
        s0=_state_to_scan(state_rwkv[l]), cos=cos_s, sin=sin_s, attend=attend_sample,
        ffn_tm=ms, ffn_seq_len=tn, conv_prev=state_ffn_conv[l])

    kv5 = lambda a, n: a.reshape(1, -1, n, KV_HEADS, HEAD_DIM)
    y_prompt = yp[None]
    y_sample = xs2.reshape(nb, tn, D_MODEL)
    prompt_meta_k = kv5(k_p[:N_META], N_META)
    prompt_meta_v = kv5(v_p[:N_META], N_META)
    prompt_win_k = kv5(k_p[mp - WINDOW:], WINDOW)
    prompt_win_v = kv5(v_p[mp - WINDOW:], WINDOW)
    prompt_rwkv = _state_from_scan(sfin_p)[None]
    prompt_shift = _unpack_rwkv_row(p_p[mp - 1:mp]).reshape(1, 1, 1, RWKV_PROJ)
    prompt_ffn_conv = tail_p[-1, SUBLANES - (FFN_CONV - 1):].reshape(1, 1, FFN_CONV - 1, D_FF)
    sample_win_k = kv5(k_s, tn)
    sample_win_v = kv5(v_s, tn)
    sample_rwkv = _state_from_scan(sfin_s)[None]
    sample_shift = _unpack_rwkv_row(p_s[tn - 1::tn]).reshape(1, nb, 1, RWKV_PROJ)
    sample_ffn_conv = tail_s[:, SUBLANES - (FFN_CONV - 1):].reshape(1, nb, FFN_CONV - 1, D_FF)
    return (y_prompt, y_sample, prompt_meta_k, prompt_meta_v, prompt_win_k, prompt_win_v,
            prompt_rwkv, prompt_shift, prompt_ffn_conv, sample_win_k, sample_win_v,
            sample_rwkv, sample_shift, sample_ffn_conv)
```

```python
import functools
import math

import numpy as np
import jax
import jax.numpy as jnp
from jax import lax
from jax.experimental import pallas as pl
from jax.experimental.pallas import tpu as pltpu

F32 = jnp.float32
BF16 = jnp.bfloat16

D_MODEL = 2048
CHUNK = 64
N_META = 16
NORM_EPS = 1e-6
HEADS = 16
HEAD_DIM = 64
RWKV_DIM = HEADS * HEAD_DIM
DECAY_LORA = 64
ICLR_LORA = 64
GATE_LORA = 160
RWKV_PROJ = 3 * RWKV_DIM + DECAY_LORA + ICLR_LORA + GATE_LORA
RWKV_LN_EPS = 64e-5
KV_HEADS = 2
GROUP = HEADS // KV_HEADS
Q_DIM = HEADS * HEAD_DIM
KV_DIM = KV_HEADS * HEAD_DIM
WINDOW = 128
ROPE_THETA = 10000.0
D_FF = 3 * D_MODEL
FFN_CONV = 3
PAST_LEN = 4096

SUBLANES = 8
LANES = 128
V_LO = LANES // HEADS
V_HI = HEAD_DIM // V_LO
K_BLOCKS = HEAD_DIM // SUBLANES
SCAN_BLOCK = SUBLANES

COL_R, COL_K, COL_V, COL_Q = 0, 1024, 2048, 3072
COL_GA, COL_GB = 4096, 6144
COL_LORA = 8192
LORA_W = 512
LORA_XW, LORA_XA, LORA_XG = 0, 128, 256
COL_KV = 8704
N_PACK = 8960
RWKV_PACK = 3 * RWKV_DIM + LORA_W

VMEM_LIMIT = 56 * 1024 * 1024


def _scan_order():
    p = np.arange(RWKV_DIM)
    vh, h, vl = p // LANES, (p % LANES) // V_LO, p % V_LO
    return h * HEAD_DIM + vh * V_LO + vl


SCAN_ORDER = _scan_order()
NATURAL_ORDER = np.argsort(SCAN_ORDER)


def _cparams(sem):
    return pltpu.CompilerParams(dimension_semantics=sem, vmem_limit_bytes=VMEM_LIMIT)


def _rms(x, g):
    return x * lax.rsqrt(jnp.mean(x * x, axis=-1, keepdims=True) + NORM_EPS) * g


def _split3(x):
    hi = x.astype(BF16)
    r1 = x - hi.astype(F32)
    mid = r1.astype(BF16)
    lo = (r1 - mid.astype(F32)).astype(BF16)
    return hi, mid, lo


def _dot3(x, mat):
    hi, mid, lo = _split3(x)
    acc = jnp.dot(hi, mat, preferred_element_type=F32)
    acc = acc + jnp.dot(mid, mat, preferred_element_type=F32)
    return acc + jnp.dot(lo, mat, preferred_element_type=F32)


def _seg_sum(x, ones_bd):
    hi, mid, lo = _split3(x)
    outs = []
    for j in range(x.shape[1] // LANES):
        sl = slice(j * LANES, (j + 1) * LANES)
        acc = jnp.dot(hi[:, sl], ones_bd, preferred_element_type=F32)
        acc = acc + jnp.dot(mid[:, sl], ones_bd, preferred_element_type=F32)
        acc = acc + jnp.dot(lo[:, sl], ones_bd, preferred_element_type=F32)
        outs.append(acc)
    return jnp.concatenate(outs, axis=1)


def _tile_lanes(x, n):
    return jnp.concatenate([x] * n, axis=1)


def _norm_matmul_kernel(x_ref, g_ref, w_ref, o_ref, h_ref):
    @pl.when(pl.program_id(1) == 0)
    def _():
        h_ref[...] = _rms(x_ref[...], g_ref[...]).astype(BF16)

    o_ref[...] = lax.dot_general(h_ref[...], w_ref[...], (((1,), (1,)), ((), ())),
                                 preferred_element_type=F32)


def _norm_matmul(x, g, w_t, tm, tn):
    m, d = x.shape
    n = w_t.shape[0]
    return pl.pallas_call(
        _norm_matmul_kernel,
        grid=(m // tm, n // tn),
        in_specs=[pl.BlockSpec((tm, d), lambda i, j: (i, 0)),
                  pl.BlockSpec((1, d), lambda i, j: (0, 0)),
                  pl.BlockSpec((tn, d), lambda i, j: (j, 0))],
        out_specs=pl.BlockSpec((tm, tn), lambda i, j: (i, j)),
        out_shape=jax.ShapeDtypeStruct((m, n), F32),
        scratch_shapes=[pltpu.VMEM((tm, d), BF16)],
        compiler_params=_cparams(("parallel", "arbitrary")),
        name="norm_matmul",
    )(x, g, w_t)


def _store_head_pairs(ref, a, b):
    lane = lax.broadcasted_iota(jnp.int32, (a.shape[0], LANES), 1)
    low = lane < HEAD_DIM
    for j in range(a.shape[1] // LANES):
        sl = slice(j * LANES, (j + 1) * LANES)
        aj, bj = a[:, sl], b[:, sl]
        ref[2 * j] = jnp.where(low, aj, pltpu.roll(bj, HEAD_DIM, 1))
        ref[2 * j + 1] = jnp.where(low, pltpu.roll(aj, HEAD_DIM, 1), bj)


def _rwkv_prep_kernel(r_ref, k_ref, v_ref, l_ref, hr_ref, hk_ref, hv_ref, hl_ref, state_ref,
                      mu_ref, w0_ref, a0_ref, kk_ref, ka_ref,
                      rk_ref, ww_ref, wa_ref, wg_ref, ones_ref, hsum_ref,
                      ab_ref, kr_ref, cum_ref, vo_ref, g_ref, bonus_ref, *, nblk):
    rows = r_ref.shape[0]
    first = lax.broadcasted_iota(jnp.int32, (rows, 1), 0) == 0
    seq_start = pl.program_id(0) % nblk == 0

    def shifted(x, halo_ref, lo, hi):
        before = jnp.where(seq_start, state_ref[:, lo:hi], halo_ref[SUBLANES - 1:SUBLANES, :])
        prev = jnp.where(first, before, pltpu.roll(x, 1, 0))
        return x + (prev - x) * mu_ref[:, lo:hi]

    r = shifted(r_ref[...], hr_ref, 0, RWKV_DIM)
    k = shifted(k_ref[...], hk_ref, RWKV_DIM, 2 * RWKV_DIM)
    v = shifted(v_ref[...], hv_ref, 2 * RWKV_DIM, 3 * RWKV_DIM)
    lora = shifted(l_ref[...], hl_ref, 3 * RWKV_DIM, RWKV_PACK)
    xw = lora[:, LORA_XW:LORA_XW + LANES]
    xa = lora[:, LORA_XA:LORA_XA + LANES]
    xg = lora[:, LORA_XG:LORA_XG + 2 * LANES]

    w_pre = w0_ref[...] + jnp.dot(jnp.tanh(xw).astype(BF16), ww_ref[...], preferred_element_type=F32)
    log_decay = -math.exp(-0.5) * jax.nn.sigmoid(w_pre)
    step_in_block = lax.broadcasted_iota(jnp.int32, (rows, 1), 0) % SCAN_BLOCK
    log_cum = log_decay
    shift = 1
    while shift < SCAN_BLOCK:
        log_cum = log_cum + jnp.where(step_in_block >= shift, pltpu.roll(log_cum, shift, 0), 0.0)
        shift *= 2
    cum = jnp.exp(log_cum)
    cum_before = jnp.exp(log_cum - log_decay)
    inv_cum = jnp.exp(-log_cum)
    a = jax.nn.sigmoid(a0_ref[...] + jnp.dot(xa.astype(BF16), wa_ref[...], preferred_element_type=F32))
    g = jnp.dot(jax.nn.sigmoid(xg).astype(BF16), wg_ref[...], preferred_element_type=F32)

    kk = k * kk_ref[...]
    kk = kk * lax.rsqrt(jnp.maximum(_seg_sum(kk * kk, ones_ref[...]), 1e-24))
    k_mod = k * (1.0 + (a - 1.0) * ka_ref[...])
    rk_sum = _dot3(r * k_mod * rk_ref[...], hsum_ref[...])

    _store_head_pairs(ab_ref, -kk * cum_before, kk * a * inv_cum)
    _store_head_pairs(kr_ref, k_mod * inv_cum, r * cum)
    _store_head_pairs(cum_ref, cum, cum)
    for vh in range(V_HI):
        vo_ref[vh] = v[:, vh * LANES:(vh + 1) * LANES]
    g_ref[...] = g
    bonus_ref[...] = _tile_lanes(rk_sum, V_HI) * v


def _rwkv_prep(p_all, state_rows, pw, tb, nblk):
    m = p_all.shape[0]
    nt = m // tb
    assert tb % SUBLANES == 0 and nt == state_rows.shape[0] * nblk
    row = lambda c: pl.BlockSpec((tb, RWKV_DIM), lambda i, c=c: (i, c))
    before = lambda i: jnp.maximum(i * (tb // SUBLANES) - 1, 0)
    halo = lambda c: pl.BlockSpec((SUBLANES, RWKV_DIM), lambda i, c=c: (before(i), c))
    full = lambda a: pl.BlockSpec(a.shape, lambda i: (0,) * a.ndim)
    params = (pw["mu"], pw["w0"], pw["a0"], pw["k_k"], pw["k_a"], pw["r_k"],
              pw["w_lora"], pw["a_lora"], pw["g_lora"], pw["ones_bd"], pw["hsum_nat"])
    per_head = jax.ShapeDtypeStruct((HEADS, m, LANES), F32)
    per_vhi = jax.ShapeDtypeStruct((V_HI, m, LANES), F32)
    narrow = jax.ShapeDtypeStruct((m, RWKV_DIM), F32)
    return pl.pallas_call(
        functools.partial(_rwkv_prep_kernel, nblk=nblk),
        grid=(nt,),
        in_specs=[row(COL_R // RWKV_DIM), row(COL_K // RWKV_DIM), row(COL_V // RWKV_DIM),
                  pl.BlockSpec((tb, LORA_W), lambda i: (i, COL_LORA // LORA_W)),
                  halo(COL_R // RWKV_DIM), halo(COL_K // RWKV_DIM), halo(COL_V // RWKV_DIM),
                  pl.BlockSpec((SUBLANES, LORA_W), lambda i: (before(i), COL_LORA // LORA_W)),
                  pl.BlockSpec((None, 1, RWKV_PACK), lambda i: (i // nblk, 0, 0))]
                 + [full(a) for a in params],
        out_specs=[pl.BlockSpec((HEADS, tb, LANES), lambda i: (0, i, 0))] * 3
                  + [pl.BlockSpec((V_HI, tb, LANES), lambda i: (0, i, 0))]
                  + [pl.BlockSpec((tb, RWKV_DIM), lambda i: (i, 0))] * 2,
        out_shape=[per_head, per_head, per_head, per_vhi, narrow, narrow],
        compiler_params=_cparams(("parallel",)),
        name="rwkv_prep",
    )(p_all, p_all, p_all, p_all, p_all, p_all, p_all, p_all, state_rows, *params)


def _sublane_allsum(x):
    x = x + pltpu.roll(x, 4, 0)
    x = x + pltpu.roll(x, 2, 0)
    return x + pltpu.roll(x, 1, 0)


def _scan_kernel(ab_ref, kr_ref, cum_ref, v_ref, s0_ref, y_ref, st_ref, ops_ref, blk_ref, *, tb, nblk):
    @pl.when(pl.program_id(0) % nblk == 0)
    def _():
        st_ref[...] = s0_ref[...]

    def col_tiles(ref, t):
        rep = jnp.concatenate(
            [jnp.broadcast_to(ref[h, pl.ds(t, 1), :], (V_LO, LANES)) for h in range(HEADS)], axis=0)
        return rep.T

    def stage(t, slot):
        ab = col_tiles(ab_ref, t)
        kr = col_tiles(kr_ref, t)
        ops_ref[slot, 0] = ab[:HEAD_DIM]
        ops_ref[slot, 1] = ab[HEAD_DIM:]
        ops_ref[slot, 2] = kr[:HEAD_DIM]
        ops_ref[slot, 3] = kr[HEAD_DIM:]

    def op_tile(slot, which, kb):
        return ops_ref[slot, which, kb * SUBLANES:(kb + 1) * SUBLANES, :]

    def step(t, slot, block_end):
        ksl = lambda kb: slice(kb * SUBLANES, (kb + 1) * SUBLANES)
        acc = [None] * V_HI
        for kb in range(K_BLOCKS):
            alpha = op_tile(slot, 0, kb)
            for vh in range(V_HI):
                p = st_ref[vh, ksl(kb), :] * alpha
                acc[vh] = p if acc[vh] is None else acc[vh] + p
        u = [_sublane_allsum(acc[vh]) for vh in range(V_HI)]
        vb = [jnp.broadcast_to(v_ref[vh, pl.ds(t, 1), :], (SUBLANES, LANES)) for vh in range(V_HI)]
        yacc = [None] * V_HI
        for kb in range(K_BLOCKS):
            beta = op_tile(slot, 1, kb)
            kmod = op_tile(slot, 2, kb)
            rr = op_tile(slot, 3, kb)
            for vh in range(V_HI):
                s_new = st_ref[vh, ksl(kb), :] + beta * u[vh] + kmod * vb[vh]
                p = s_new * rr
                yacc[vh] = p if yacc[vh] is None else yacc[vh] + p
                if block_end:
                    s_new = s_new * blk_ref[ksl(kb), :]
                st_ref[vh, ksl(kb), :] = s_new
        for vh in range(V_HI):
            y_ref[vh, pl.ds(t, 1), :] = _sublane_allsum(yacc[vh])[0:1, :]

    stage(0, 0)

    def body(i, carry):
        base = pl.multiple_of(i * SCAN_BLOCK, SCAN_BLOCK)
        following = pl.multiple_of(jnp.minimum(base + SCAN_BLOCK, tb - SCAN_BLOCK), SCAN_BLOCK)
        blk_ref[...] = col_tiles(cum_ref, base + (SCAN_BLOCK - 1))[:HEAD_DIM]
        for j in range(SCAN_BLOCK):
            nxt = base + (j + 1) if j + 1 < SCAN_BLOCK else following
            stage(nxt, (j + 1) % 2)
            step(base + j, j % 2, j + 1 == SCAN_BLOCK)
        return carry

    lax.fori_loop(0, tb // SCAN_BLOCK, body, 0)


def _rwkv_scan(ab, kr, cum, v, s0, tb, nblk):
    t_total = v.shape[1]
    nseq = s0.shape[0]
    grid = t_total // tb
    assert grid == nseq * nblk and tb % SCAN_BLOCK == 0 and SCAN_BLOCK % SUBLANES == 0
    st_shape = (V_HI, HEAD_DIM, LANES)
    per_head = pl.BlockSpec((HEADS, tb, LANES), lambda i: (0, i, 0))
    per_vhi = pl.BlockSpec((V_HI, tb, LANES), lambda i: (0, i, 0))
    state = pl.BlockSpec((None,) + st_shape, lambda i: (i // nblk, 0, 0, 0))
    return pl.pallas_call(
        functools.partial(_scan_kernel, tb=tb, nblk=nblk),
        grid=(grid,),
        in_specs=[per_head, per_head, per_head, per_vhi, state],
        out_specs=[per_vhi, state],
        out_shape=[jax.ShapeDtypeStruct((V_HI, t_total, LANES), F32),
                   jax.ShapeDtypeStruct((nseq,) + st_shape, F32)],
        scratch_shapes=[pltpu.VMEM((2, 4, HEAD_DIM, LANES), F32), pltpu.VMEM((HEAD_DIM, LANES), F32)],
        compiler_params=_cparams(("arbitrary",)),
        name="rwkv_scan",
    )(ab, kr, cum, v, s0)


def _state_to_scan(s):
    b = s.shape[0]
    s = s.reshape(b, HEADS, V_HI, V_LO, HEAD_DIM)
    return s.transpose(0, 2, 4, 1, 3).reshape(b, V_HI, HEAD_DIM, LANES)


def _state_from_scan(st):
    b = st.shape[0]
    st = st.reshape(b, V_HI, HEAD_DIM, HEADS, V_LO)
    return st.transpose(0, 3, 1, 4, 2).reshape(b, HEADS, HEAD_DIM, HEAD_DIM)


def _rope_tile(x, cos, sin_signed):
    lane = lax.broadcasted_iota(jnp.int32, x.shape, 1)
    first_half = (lane % HEAD_DIM) < (HEAD_DIM // 2)
    rot = jnp.where(first_half, pltpu.roll(x, LANES - HEAD_DIM // 2, 1), pltpu.roll(x, HEAD_DIM // 2, 1))
    return x * cos + rot * sin_signed


def _rope_kernel(q_ref, kv_ref, cos_ref, sin_ref, qo_ref, ko_ref, vo_ref):
    cos = cos_ref[...]
    sin = sin_ref[...]
    scale = HEAD_DIM ** -0.5
    assert math.log2(scale).is_integer()
    for j in range(Q_DIM // LANES):
        sl = slice(j * LANES, (j + 1) * LANES)
        qo_ref[:, sl] = (_rope_tile(q_ref[:, sl], cos, sin) * scale).astype(BF16)
    ko_ref[...] = _rope_tile(kv_ref[:, :KV_DIM], cos, sin)
    vo_ref[...] = kv_ref[:, KV_DIM:]


def _rope(p_all, cos, sin, tb):
    m = p_all.shape[0]
    return pl.pallas_call(
        _rope_kernel,
        grid=(m // tb,),
        in_specs=[pl.BlockSpec((tb, Q_DIM), lambda i: (i, COL_Q // Q_DIM)),
                  pl.BlockSpec((tb, 2 * KV_DIM), lambda i: (i, COL_KV // (2 * KV_DIM))),
                  pl.BlockSpec((tb, LANES), lambda i: (i, 0)),
                  pl.BlockSpec((tb, LANES), lambda i: (i, 0))],
        out_specs=[pl.BlockSpec((tb, Q_DIM), lambda i: (i, 0)),
                   pl.BlockSpec((tb, KV_DIM), lambda i: (i, 0)),
                   pl.BlockSpec((tb, KV_DIM), lambda i: (i, 0))],
        out_shape=[jax.ShapeDtypeStruct((m, Q_DIM), BF16),
                   jax.ShapeDtypeStruct((m, KV_DIM), F32),
                   jax.ShapeDtypeStruct((m, KV_DIM), F32)],
        compiler_params=_cparams(("parallel",)),
        name="rope",
    )(p_all, p_all, cos, sin)


def _rope_tables(pos):
    half = HEAD_DIM // 2
    inv_freq = ROPE_THETA ** (-jnp.arange(half, dtype=F32) / half)
    ang = pos.astype(F32)[:, None] * inv_freq[None, :]
    cos = jnp.cos(ang)
    sin = jnp.sin(ang)
    cos = jnp.concatenate([cos, cos, cos, cos], axis=1)
    sin = jnp.concatenate([-sin, sin, -sin, sin], axis=1)
    return cos, sin


PAIRS = GROUP // 2


def _attn_kernel(*refs, nseg, band_mask):
    q_ref, sink_ref = refs[0], refs[1]
    k_refs = refs[2:2 + nseg]
    v_refs = refs[2 + nseg:2 + 2 * nseg]
    o_ref = refs[2 + 2 * nseg]
    tq = q_ref.shape[0]

    nkeys = sum(r.shape[0] for r in k_refs)
    nk_pad = pl.cdiv(nkeys + 1, LANES) * LANES
    pad = jnp.zeros((nk_pad - nkeys, LANES), F32)
    k_all = jnp.concatenate([r[...] for r in k_refs] + [pad], axis=0)
    v_all = jnp.concatenate([r[...] for r in v_refs] + [pad], axis=0)
    low = lax.broadcasted_iota(jnp.int32, (nk_pad, LANES), 1) < HEAD_DIM
    k_swap = pltpu.roll(k_all, HEAD_DIM, 1)
    v_swap = pltpu.roll(v_all, HEAD_DIM, 1)
    col = lax.broadcasted_iota(jnp.int32, (1, nk_pad), 1)
    visible = col < nkeys
    if band_mask:
        first_valid = N_META + jnp.maximum(2 - pl.program_id(0), 0) * CHUNK
        visible = visible & ((col < N_META) | (col >= first_valid))
    bias = jnp.where(visible, 0.0, -jnp.inf).astype(F32)
    is_sink = col == nkeys
    ones_kv = (lax.broadcasted_iota(jnp.int32, (nk_pad, LANES), 0) <= nkeys).astype(BF16)
    rows = PAIRS * tq
    scores, values = [], []
    for kvh in range(KV_HEADS):
        k_src, k_oth = (k_all, k_swap) if kvh == 0 else (k_swap, k_all)
        v_src, v_oth = (v_all, v_swap) if kvh == 0 else (v_swap, v_all)
        halves = ((jnp.where(low, k_src, 0.0), jnp.where(low, v_src, 0.0)),
                  (jnp.where(low, 0.0, k_oth), jnp.where(low, 0.0, v_oth)))
        qp = jnp.concatenate(
            [q_ref[:, (kvh * PAIRS + p) * LANES:(kvh * PAIRS + p + 1) * LANES] for p in range(PAIRS)], axis=0)
        for parity, (kx, vx) in enumerate(halves):
            s = lax.dot_general(qp, kx.astype(BF16), (((1,), (1,)), ((), ())), preferred_element_type=F32) + bias
            for p in range(PAIRS):
                h = kvh * GROUP + 2 * p + parity
                scores.append(jnp.where(is_sink, sink_ref[:, h:h + 1], s[p * tq:(p + 1) * tq]))
            values.append(jnp.concatenate([vx.astype(BF16), ones_kv], axis=1))
    s_all = jnp.concatenate(scores, axis=0)
    e = jnp.exp(s_all - jnp.max(s_all, axis=1, keepdims=True)).astype(BF16)
    groups = []
    for kvh in range(KV_HEADS):
        total = None
        for parity in range(2):
            idx = kvh * 2 + parity
            ov = jnp.dot(e[idx * rows:(idx + 1) * rows], values[idx], preferred_element_type=F32)
            o = ov[:, :LANES] / ov[:, LANES:]
            total = o if total is None else total + o
        groups.extend(total[p * tq:(p + 1) * tq] for p in range(PAIRS))
    o_ref[...] = jnp.concatenate(groups, axis=1).astype(BF16)


def _attention(q, sinks, k_segs, v_segs, tq, grid, band_mask):
    nseg = len(k_segs)
    seg_specs = [pl.BlockSpec(bs, im) for (_, bs, im) in k_segs + v_segs]
    return pl.pallas_call(
        functools.partial(_attn_kernel, nseg=nseg, band_mask=band_mask),
        grid=(grid,),
        in_specs=[pl.BlockSpec((tq, Q_DIM), lambda i: (i, 0)),
                  pl.BlockSpec((1, HEADS), lambda i: (0, 0))] + seg_specs,
        out_specs=pl.BlockSpec((tq, Q_DIM), lambda i: (i, 0)),
        out_shape=jax.ShapeDtypeStruct((grid * tq, Q_DIM), BF16),
        compiler_params=_cparams(("parallel",)),
        name="attention",
    )(q, sinks, *[a for (a, _, _) in k_segs + v_segs])


def _mix_kernel(y_ref, bonus_ref, g_ref, yb_ref, ga_ref, gb_ref, lng_ref, lnb_ref, hsum_ref,
                wa_ref, wb_ref, o_ref, ya_ref):
    hsum = hsum_ref[...]
    mean = _dot3(sum(y_ref[vh] for vh in range(V_HI)), hsum) * (1.0 / HEAD_DIM)
    d = [y_ref[vh] - mean for vh in range(V_HI)]
    var = _dot3(sum(dv * dv for dv in d), hsum) * (1.0 / HEAD_DIM)
    inv = lax.rsqrt(var + RWKV_LN_EPS)
    for vh in range(V_HI):
        sl = slice(vh * LANES, (vh + 1) * LANES)
        yn = d[vh] * inv * lng_ref[:, sl] + lnb_ref[:, sl]
        ya_ref[:, sl] = ((yn + bonus_ref[:, sl]) * g_ref[:, sl]).astype(BF16)

    pa = jnp.dot(ya_ref[...], wa_ref[...], preferred_element_type=F32)
    pb = jnp.dot(yb_ref[...], wb_ref[...], preferred_element_type=F32)
    o_ref[...] = (jax.nn.sigmoid(ga_ref[...]) * pa + jax.nn.sigmoid(gb_ref[...]) * pb).astype(BF16)


def _mix(y, bonus, g, yb, p_all, ln_g, ln_b, hsum, wa, wb, tm):
    m = bonus.shape[0]
    row = pl.BlockSpec((tm, RWKV_DIM), lambda i: (i, 0))
    one = lambda a: pl.BlockSpec(a.shape, lambda i: (0,) * a.ndim)
    resident = lambda a: pl.BlockSpec(a.shape, lambda i: (0, 0), pipeline_mode=pl.Buffered(1))
    return pl.pallas_call(
        _mix_kernel,
        grid=(m // tm,),
        in_specs=[pl.BlockSpec((V_HI, tm, LANES), lambda i: (0, i, 0)), row, row, row,
                  pl.BlockSpec((tm, D_MODEL), lambda i: (i, COL_GA // D_MODEL)),
                  pl.BlockSpec((tm, D_MODEL), lambda i: (i, COL_GB // D_MODEL)),
                  one(ln_g), one(ln_b), one(hsum), resident(wa), resident(wb)],
        out_specs=pl.BlockSpec((tm, D_MODEL), lambda i: (i, 0)),
        out_shape=jax.ShapeDtypeStruct((m, D_MODEL), BF16),
        scratch_shapes=[pltpu.VMEM((tm, RWKV_DIM), BF16)],
        compiler_params=_cparams(("parallel",)),
        name="branch_mix",
    )(y, bonus, g, yb, p_all, p_all, ln_g, ln_b, hsum, wa, wb)


def _out_norm_kernel(mix_ref, w_ref, x_ref, g_ref, o_ref):
    z = jnp.dot(mix_ref[...], w_ref[...], preferred_element_type=F32)
    o_ref[...] = x_ref[...] + _rms(z, g_ref[...])


def _out_norm(mixed, w, x, g, tm):
    m = x.shape[0]
    return pl.pallas_call(
        _out_norm_kernel,
        grid=(m // tm,),
        in_specs=[pl.BlockSpec((tm, D_MODEL), lambda i: (i, 0)),
                  pl.BlockSpec((D_MODEL, D_MODEL), lambda i: (0, 0)),
                  pl.BlockSpec((tm, D_MODEL), lambda i: (i, 0)),
                  pl.BlockSpec((1, D_MODEL), lambda i: (0, 0))],
        out_specs=pl.BlockSpec((tm, D_MODEL), lambda i: (i, 0)),
        out_shape=jax.ShapeDtypeStruct((m, D_MODEL), F32),
        compiler_params=_cparams(("parallel",)),
        name="out_norm",
    )(mixed, w, x, g)


HALO = 16


def _gelu_tanh(x):
    return 0.5 * x * (1.0 + jnp.tanh(0.7978845608028654 * (x + 0.044715 * x * x * x)))


def _ffn_kernel(*refs, seq_len):
    if seq_len is None:
        (x_ref, halo_ref, gpre_ref, wg_ref, wv_ref, wd_ref, cw_ref, cb_ref, gpost_ref,
         o_ref, tail_ref, h_ref) = refs
    else:
        (x_ref, sa_ref, sb_ref, gpre_ref, wg_ref, wv_ref, wd_ref, cw_ref, cb_ref, gpost_ref,
         o_ref, tail_ref, h_ref) = refs
    acc_ref = o_ref
    tm = x_ref.shape[0]
    j = pl.program_id(1)
    top = HALO if seq_len is None else 0

    @pl.when(j == 0)
    def _():
        if seq_len is None:
            h_ref[:HALO] = _rms(halo_ref[...], gpre_ref[...]).astype(BF16)
        h_ref[top:] = _rms(x_ref[...], gpre_ref[...]).astype(BF16)
        acc_ref[...] = jnp.zeros_like(acc_ref)

    h_all = h_ref[...]
    gate_all = jnp.dot(h_all, wg_ref[...], preferred_element_type=F32)
    val = jnp.dot(h_all[top:], wv_ref[...], preferred_element_type=F32)
    g0 = gate_all[top:]
    g1 = pltpu.roll(gate_all, 1, 0)[top:]
    g2 = pltpu.roll(gate_all, 2, 0)[top:]
    if seq_len is not None:
        nseq = tm // seq_len
        tf = g0.shape[1]
        tau = lax.broadcasted_iota(jnp.int32, (tm, 1), 0) % seq_len
        expand = lambda s: jnp.concatenate(
            [jnp.broadcast_to(s[b:b + 1, :], (seq_len, tf)) for b in range(nseq)], axis=0)
        st_a = expand(sa_ref[...])
        st_b = expand(sb_ref[...])
        g1 = jnp.where(tau == 0, st_b, g1)
        g2 = jnp.where(tau == 0, st_a, jnp.where(tau == 1, st_b, g2))
    conv = cb_ref[...] + g2 * cw_ref[0:1, :] + g1 * cw_ref[1:2, :] + g0 * cw_ref[2:3, :]
    act = (_gelu_tanh(conv) * val).astype(BF16)
    acc_ref[...] += jnp.dot(act, wd_ref[...], preferred_element_type=F32)

    if seq_len is None:
        tail_ref[...] = g0[tm - SUBLANES:]
    else:
        for b in range(tm // seq_len):
            tail_ref[b] = g0[(b + 1) * seq_len - SUBLANES:(b + 1) * seq_len]

    @pl.when(j == pl.num_programs(1) - 1)
    def _():
        o_ref[...] = x_ref[...] + _rms(acc_ref[...], gpost_ref[...])


def _ffn(x1, conv_prev, fw, tm, tf, seq_len):
    m = x1.shape[0] - (HALO if seq_len is None else 0)
    nm, nf = m // tm, D_FF // tf
    one = lambda a: pl.BlockSpec(a.shape, lambda i, j: (0,) * a.ndim)
    wspecs = [pl.BlockSpec((D_MODEL, tf), lambda i, j: (0, j)),
              pl.BlockSpec((D_MODEL, tf), lambda i, j: (0, D_FF // tf + j)),
              pl.BlockSpec((tf, D_MODEL), lambda i, j: (j, 0)),
              pl.BlockSpec((FFN_CONV, tf), lambda i, j: (0, j)),
              pl.BlockSpec((1, tf), lambda i, j: (0, j)),
              one(fw["g_post"])]
    wargs = (fw["w_up"], fw["w_up"], fw["w_down"], fw["conv_w"], fw["conv_b"], fw["g_post"])
    if seq_len is None:
        assert tm % HALO == 0
        pre_specs = [pl.BlockSpec((pl.Element(tm), pl.Element(D_MODEL)),
                                  lambda i, j: (pl.multiple_of(HALO + i * tm, HALO), 0)),
                     pl.BlockSpec((HALO, D_MODEL), lambda i, j: (i * (tm // HALO), 0)),
                     one(fw["g_pre"])]
        pre_args = (x1, x1, fw["g_pre"])
        tail_shape = (nm, SUBLANES, D_FF)
        tail_spec = pl.BlockSpec((None, SUBLANES, tf), lambda i, j: (i, 0, j))
        hrows = tm + HALO
    else:
        nseq = tm // seq_len
        assert nm == 1
        pre_specs = [pl.BlockSpec((tm, D_MODEL), lambda i, j: (i, 0)),
                     pl.BlockSpec((nseq, tf), lambda i, j: (0, j)),
                     pl.BlockSpec((nseq, tf), lambda i, j: (0, j)),
                     one(fw["g_pre"])]
        pre_args = (x1, conv_prev[:, 0, :], conv_prev[:, 1, :], fw["g_pre"])
        tail_shape = (nseq, SUBLANES, D_FF)
        tail_spec = pl.BlockSpec((nseq, SUBLANES, tf), lambda i, j: (0, 0, j))
        hrows = tm
    return pl.pallas_call(
        functools.partial(_ffn_kernel, seq_len=seq_len),
        grid=(nm, nf),
        in_specs=pre_specs + wspecs,
        out_specs=[pl.BlockSpec((tm, D_MODEL), lambda i, j: (i, 0), pipeline_mode=pl.Buffered(1)), tail_spec],
        out_shape=[jax.ShapeDtypeStruct((m, D_MODEL), F32),
                   jax.ShapeDtypeStruct(tail_shape, F32)],
        scratch_shapes=[pltpu.VMEM((hrows, D_MODEL), BF16)],
        compiler_params=_cparams(("parallel", "arbitrary")),
        name="conv_ffn",
    )(*pre_args, *wargs)


def _pack_lora(x, axis=1):
    cut = lambda a, b: lax.slice_in_dim(x, a, b, axis=axis)

    def z(n):
        shape = list(x.shape)
        shape[axis] = n
        return jnp.zeros(shape, x.dtype)

    xw = cut(0, DECAY_LORA)
    xa = cut(DECAY_LORA, DECAY_LORA + ICLR_LORA)
    xg = cut(DECAY_LORA + ICLR_LORA, DECAY_LORA + ICLR_LORA + GATE_LORA)
    return jnp.concatenate([xw, z(LORA_XA - DECAY_LORA), xa, z(LORA_XG - LORA_XA - ICLR_LORA),
                            xg, z(LORA_W - LORA_XG - GATE_LORA)], axis=axis)


def _unpack_lora(x):
    return jnp.concatenate([x[:, LORA_XW:LORA_XW + DECAY_LORA], x[:, LORA_XA:LORA_XA + ICLR_LORA],
                            x[:, LORA_XG:LORA_XG + GATE_LORA]], axis=1)


def _pack_in_proj_t(wt):
    c = 3 * RWKV_DIM
    q0 = RWKV_PROJ
    kv0 = q0 + Q_DIM
    g0 = kv0 + 2 * KV_DIM
    return jnp.concatenate([wt[:2 * RWKV_DIM], wt[2 * RWKV_DIM:c][SCAN_ORDER], wt[q0:kv0],
                            wt[g0:g0 + 2 * D_MODEL], _pack_lora(wt[c:RWKV_PROJ], axis=0), wt[kv0:g0]], axis=0)


def _pack_rwkv_row(x):
    c = 3 * RWKV_DIM
    return jnp.concatenate([x[:, :2 * RWKV_DIM], x[:, 2 * RWKV_DIM:c][:, SCAN_ORDER], _pack_lora(x[:, c:])],
                           axis=1)


def _unpack_rwkv_row(p_row):
    return jnp.concatenate([p_row[:, :2 * RWKV_DIM], p_row[:, COL_V:COL_V + RWKV_DIM][:, NATURAL_ORDER],
                            _unpack_lora(p_row[:, COL_LORA:COL_LORA + LORA_W])], axis=1)


def _pick_tile(m, candidates):
    for c in candidates:
        if m % c == 0:
            return c
    raise ValueError(f"no row tile in {candidates} divides {m}")


def _pad_rows(w, rows):
    return jnp.concatenate([w, jnp.zeros((rows - w.shape[0], w.shape[1]), w.dtype)], axis=0)


def _trunk(x, lw, *, tm, tm_mix, tb_prep, shift_rows, scan_tb, s0, cos, sin, attend, ffn_tm,
           ffn_seq_len, conv_prev):
    nseq = s0.shape[0]
    seq_rows = x.shape[0] // nseq
    scan_nblk = seq_rows // scan_tb
    p_all = _norm_matmul(x, lw["g_pre_mix"], lw["w_in_t"], tm, 1280)

    ab, kr, cum, v_scan, g, bonus = _rwkv_prep(p_all, shift_rows, lw, tb_prep, seq_rows // tb_prep)
    y, s_fin = _rwkv_scan(ab, kr, cum, v_scan, s0, scan_tb, scan_nblk)

    q_roped, k_roped, v_att = _rope(p_all, cos, sin, tm)
    att = attend(q_roped, k_roped, v_att)

    mixed = _mix(y, bonus, g, att, p_all, lw["ln_g"], lw["ln_b"], lw["hsum_scan"],
                 lw["w_branch_a"], lw["w_branch_b"], tm_mix)
    x1 = _out_norm(mixed, lw["w_out"], x, lw["g_post_mix"], tm_mix)
    x2, tail = _ffn(x1, conv_prev, lw["ffn"], ffn_tm, 512, ffn_seq_len)
    return x2, p_all, k_roped, v_att, s_fin, tail


def kernel(x_prompt, x_sample, cache_meta_k, cache_meta_v, cache_win_k, cache_win_v, state_rwkv, state_shift, state_ffn_conv, meta_tokens, g_pre_mix, w_in, rwkv_mu, rwkv_w0, rwkv_w_lora_b, rwkv_a0, rwkv_a_lora_b, rwkv_g_lora_b, rwkv_k_k, rwkv_k_a, rwkv_r_k, rwkv_ln_g, rwkv_ln_b, attn_sinks, w_branch_a, w_branch_b, w_out, g_post_mix, g_pre_ffn, w_up, ffn_conv_w, ffn_conv_b, w_down, g_post_ffn):
    depth = w_in.shape[0]
    assert depth == 1
    l = 0
    seq = x_prompt.shape[1]
    nb, tn = x_sample.shape[0], x_sample.shape[1]
    row = lambda a: a.reshape(1, -1)
    lane = np.arange(LANES)
    chan = np.arange(RWKV_DIM)
    ones_bd = jnp.asarray(lane[:, None] // HEAD_DIM == lane[None, :] // HEAD_DIM, BF16)
    hsum_nat = jnp.asarray((chan // HEAD_DIM)[:, None] == (lane // V_LO)[None, :], BF16)
    hsum_scan = jnp.asarray((lane // V_LO)[:, None] == (lane // V_LO)[None, :], BF16)
    lw = dict(
        g_pre_mix=row(g_pre_mix[l]),
        w_in_t=_pack_in_proj_t(jnp.swapaxes(w_in[l], 0, 1)).astype(BF16),
        mu=_pack_rwkv_row(row(rwkv_mu[l])),
        w0=row(rwkv_w0[l]), a0=row(rwkv_a0[l]), k_k=row(rwkv_k_k[l]), k_a=row(rwkv_k_a[l]),
        r_k=row(rwkv_r_k[l]),
        w_lora=_pad_rows(rwkv_w_lora_b[l], LANES).astype(BF16),
        a_lora=_pad_rows(rwkv_a_lora_b[l], LANES).astype(BF16),
        g_lora=_pad_rows(rwkv_g_lora_b[l][:, SCAN_ORDER], 2 * LANES).astype(BF16),
        ones_bd=ones_bd, hsum_nat=hsum_nat, hsum_scan=hsum_scan,
        ln_g=row(rwkv_ln_g[l][SCAN_ORDER]), ln_b=row(rwkv_ln_b[l][SCAN_ORDER]),
        w_branch_a=w_branch_a[l][SCAN_ORDER, :].astype(BF16), w_branch_b=w_branch_b[l].astype(BF16),
        w_out=w_out[l].astype(BF16), g_post_mix=row(g_post_mix[l]),
        ffn=dict(g_pre=row(g_pre_ffn[l]), w_up=w_up[l].astype(BF16), w_down=w_down[l].astype(BF16),
                 conv_w=ffn_conv_w[l], conv_b=row(ffn_conv_b[l]), g_post=row(g_post_ffn[l])),
    )
    sinks = row(attn_sinks[l])

    mp = N_META + seq
    xp = jnp.concatenate([meta_tokens.astype(x_prompt.dtype), x_prompt[0]], axis=0)
    nchunk = seq // CHUNK
    tm_p = _pick_tile(mp, (912, 144, 48))
    tm_mix_p = _pick_tile(mp, (432, 144, 48))
    tb_prep_p = _pick_tile(mp, (144, 48))
    cos_p, sin_p = _rope_tables(jnp.arange(mp, dtype=jnp.int32))

    def attend_prompt(q, k, v):
        mk, mv = k[:N_META], v[:N_META]
        whole = lambda a: (a, (N_META, KV_DIM), lambda c: (0, 0))
        o_m = _attention(q[:N_META], sinks, [whole(mk)], [whole(mv)], N_META, 1, False)
        kf, vf = k[N_META:], v[N_META:]
        band = lambda a: [(a, (CHUNK, KV_DIM), lambda c, d=d: (jnp.maximum(c - d, 0), 0)) for d in (2, 1, 0)]
        o_f = _attention(q[N_META:], sinks, [whole(mk)] + band(kf), [whole(mv)] + band(vf),
                         CHUNK, nchunk, True)
        return jnp.concatenate([o_m, o_f], axis=0)

    s0_p = jnp.zeros((1, V_HI, HEAD_DIM, LANES), F32)
    scan_tb_p = _pick_tile(mp, (432, 144, 48))
    assert N_META == HALO
    yp, p_p, k_p, v_p, sfin_p, tail_p = _trunk(
        xp, lw, tm=tm_p, tm_mix=tm_mix_p, tb_prep=tb_prep_p, shift_rows=jnp.zeros((1, 1, RWKV_PACK), F32),
        scan_tb=scan_tb_p, s0=s0_p, cos=cos_p, sin=sin_p,
        attend=attend_prompt, ffn_tm=_pick_tile(seq, (1024, 64)), ffn_seq_len=None, conv_prev=None)

    ms = nb * tn
    xs = x_sample.reshape(ms, D_MODEL)
    pos_s = N_META + PAST_LEN + jnp.arange(tn, dtype=jnp.int32)
    cos_s, sin_s = _rope_tables(jnp.tile(pos_s, nb))
    nwin = cache_win_k.shape[2]
    cmk = cache_meta_k[l].reshape(nb, N_META, KV_DIM)
    cmv = cache_meta_v[l].reshape(nb, N_META, KV_DIM)
    cwk = cache_win_k[l].reshape(nb, nwin, KV_DIM)
    cwv = cache_win_v[l].reshape(nb, nwin, KV_DIM)

    shift_s = _pack_rwkv_row(state_shift[l].reshape(nb, RWKV_PROJ))[:, None, :]

    def attend_sample(q, k, v):
        per_b = lambda a, n: (a, (None, n, KV_DIM), lambda b: (b, 0, 0))
        own = lambda a: (a, (tn, KV_DIM), lambda b: (b, 0))
        return _attention(q, sinks, [per_b(cmk, N_META), per_b(cwk, nwin), own(k)],
                          [per_b(cmv, N_META), per_b(cwv, nwin), own(v)], tn, nb, False)

    xs2, p_s, k_s, v_s, sfin_s, tail_s = _trunk(
        xs, lw, tm=ms, tm_mix=ms, tb_prep=tn, shift_rows=shift_s, scan_tb=tn,
        s0=_state_to_scan(state_rwkv[l]), cos=cos_s, sin=sin_s, attend=attend_sample,
        ffn_tm=ms, ffn_seq_len=tn, conv_prev=state_ffn_conv[l])

    kv5 = lambda a, n: a.reshape(1, -1, n, KV_HEADS, HEAD_DIM)
    y_prompt = yp[None]
    y_sample = xs2.reshape(nb, tn, D_MODEL)
    prompt_meta_k = kv5(k_p[:N_META], N_META)
    prompt_meta_v = kv5(v_p[:N_META], N_META)
    prompt_win_k = kv5(k_p[mp - WINDOW:], WINDOW)
    prompt_win_v = kv5(v_p[mp - WINDOW:], WINDOW)
    prompt_rwkv = _state_from_scan(sfin_p)[None]
    prompt_shift = _unpack_rwkv_row(p_p[mp - 1:mp]).reshape(1, 1, 1, RWKV_PROJ)
    prompt_ffn_conv = tail_p[-1, SUBLANES - (FFN_CONV - 1):].reshape(1, 1, FFN_CONV - 1, D_FF)
    sample_win_k = kv5(k_s, tn)
    sample_win_v = kv5(v_s, tn)
    sample_rwkv = _state_from_scan(sfin_s)[None]
    sample_shift = _unpack_rwkv_row(p_s[tn - 1::tn]).reshape(1, nb, 1, RWKV_PROJ)
    sample_ffn_conv = tail_s[:, SUBLANES - (FFN_CONV - 1):].reshape(1, nb, FFN_CONV - 1, D_FF)
    return (y_prompt, y_sample, prompt_meta_k, prompt_meta_v, prompt_win_k, prompt_win_v,
            prompt_rwkv, prompt_shift, prompt_ffn_conv, sample_win_k, sample_win_v,
            sample_rwkv, sample_shift, sample_ffn_conv)
```

```python
import functools
import math

import numpy as np
import jax
import jax.numpy as jnp
from jax import lax
from jax.experimental import pallas as pl
from jax.experimental.pallas import tpu as pltpu

F32 = jnp.float32
BF16 = jnp.bfloat16

D_MODEL = 2048
CHUNK = 64
N_META = 16
NORM_EPS = 1e-6
HEADS = 16
HEAD_DIM = 64
RWKV_DIM = HEADS * HEAD_DIM
DECAY_LORA = 64
ICLR_LORA = 64
GATE_LORA = 160
RWKV_PROJ = 3 * RWKV_DIM + DECAY_LORA + ICLR_LORA + GATE_LORA
RWKV_LN_EPS = 64e-5
KV_HEADS = 2
GROUP = HEADS // KV_HEADS
Q_DIM = HEADS * HEAD_DIM
KV_DIM = KV_HEADS * HEAD_DIM
WINDOW = 128
ROPE_THETA = 10000.0
D_FF = 3 * D_MODEL
FFN_CONV = 3
PAST_LEN = 4096

SUBLANES = 8
LANES = 128
V_LO = LANES // HEADS
V_HI = HEAD_DIM // V_LO
K_BLOCKS = HEAD_DIM // SUBLANES
SCAN_BLOCK = SUBLANES

COL_R, COL_K, COL_V, COL_Q = 0, 1024, 2048, 3072
COL_GA, COL_GB = 4096, 6144
COL_LORA = 8192
LORA_W = 512
LORA_XW, LORA_XA, LORA_XG = 0, 128, 256
COL_KV = 8704
N_PACK = 8960
RWKV_PACK = 3 * RWKV_DIM + LORA_W

VMEM_LIMIT = 56 * 1024 * 1024


def _swap_channel_order(x, axis, outer, inner):
    axis = axis % x.ndim
    shape = x.shape
    x = x.reshape(shape[:axis] + (outer, inner, V_LO) + shape[axis + 1:])
    return jnp.swapaxes(x, axis, axis + 1).reshape(shape)


def _to_scan_order(x, axis):
    return _swap_channel_order(x, axis, HEADS, V_HI)


def _to_natural_order(x, axis):
    return _swap_channel_order(x, axis, V_HI, HEADS)


def _cparams(sem):
    return pltpu.CompilerParams(dimension_semantics=sem, vmem_limit_bytes=VMEM_LIMIT)


def _rms(x, g):
    return x * lax.rsqrt(jnp.mean(x * x, axis=-1, keepdims=True) + NORM_EPS) * g


def _split3(x):
    hi = x.astype(BF16)
    r1 = x - hi.astype(F32)
    mid = r1.astype(BF16)
    lo = (r1 - mid.astype(F32)).astype(BF16)
    return hi, mid, lo


def _dot3(x, mat):
    hi, mid, lo = _split3(x)
    acc = jnp.dot(hi, mat, preferred_element_type=F32)
    acc = acc + jnp.dot(mid, mat, preferred_element_type=F32)
    return acc + jnp.dot(lo, mat, preferred_element_type=F32)


def _seg_sum(x, ones_bd):
    hi, mid, lo = _split3(x)
    outs = []
    for j in range(x.shape[1] // LANES):
        sl = slice(j * LANES, (j + 1) * LANES)
        acc = jnp.dot(hi[:, sl], ones_bd, preferred_element_type=F32)
        acc = acc + jnp.dot(mid[:, sl], ones_bd, preferred_element_type=F32)
        acc = acc + jnp.dot(lo[:, sl], ones_bd, preferred_element_type=F32)
        outs.append(acc)
    return jnp.concatenate(outs, axis=1)


def _tile_lanes(x, n):
    return jnp.concatenate([x] * n, axis=1)


def _norm_matmul_kernel(x_ref, g_ref, w_ref, o_ref, h_ref):
    @pl.when(pl.program_id(1) == 0)
    def _():
        h_ref[...] = _rms(x_ref[...], g_ref[...]).astype(BF16)

    o_ref[...] = lax.dot_general(h_ref[...], w_ref[...], (((1,), (1,)), ((), ())),
                                 preferred_element_type=F32)


def _norm_matmul(x, g, w_t, tm, tn):
    m, d = x.shape
    n = w_t.shape[0]
    return pl.pallas_call(
        _norm_matmul_kernel,
        grid=(m // tm, n // tn),
        in_specs=[pl.BlockSpec((tm, d), lambda i, j: (i, 0)),
                  pl.BlockSpec((1, d), lambda i, j: (0, 0)),
                  pl.BlockSpec((tn, d), lambda i, j: (j, 0))],
        out_specs=pl.BlockSpec((tm, tn), lambda i, j: (i, j)),
        out_shape=jax.ShapeDtypeStruct((m, n), F32),
        scratch_shapes=[pltpu.VMEM((tm, d), BF16)],
        compiler_params=_cparams(("parallel", "arbitrary")),
        name="norm_matmul",
    )(x, g, w_t)


def _store_head_pairs(ref, a, b):
    lane = lax.broadcasted_iota(jnp.int32, (a.shape[0], LANES), 1)
    low = lane < HEAD_DIM
    for j in range(a.shape[1] // LANES):
        sl = slice(j * LANES, (j + 1) * LANES)
        aj, bj = a[:, sl], b[:, sl]
        ref[2 * j] = jnp.where(low, aj, pltpu.roll(bj, HEAD_DIM, 1))
        ref[2 * j + 1] = jnp.where(low, pltpu.roll(aj, HEAD_DIM, 1), bj)


def _rwkv_prep_kernel(r_ref, k_ref, v_ref, l_ref, hr_ref, hk_ref, hv_ref, hl_ref, state_ref,
                      mu_ref, w0_ref, a0_ref, kk_ref, ka_ref,
                      rk_ref, ww_ref, wa_ref, wg_ref, ones_ref, hsum_ref,
                      ab_ref, kr_ref, cum_ref, vo_ref, g_ref, bonus_ref, *, nblk):
    rows = r_ref.shape[0]
    first = lax.broadcasted_iota(jnp.int32, (rows, 1), 0) == 0
    seq_start = pl.program_id(0) % nblk == 0

    def shifted(x, halo_ref, lo, hi):
        before = jnp.where(seq_start, state_ref[:, lo:hi], halo_ref[SUBLANES - 1:SUBLANES, :])
        prev = jnp.where(first, before, pltpu.roll(x, 1, 0))
        return x + (prev - x) * mu_ref[:, lo:hi]

    r = shifted(r_ref[...], hr_ref, 0, RWKV_DIM)
    k = shifted(k_ref[...], hk_ref, RWKV_DIM, 2 * RWKV_DIM)
    v = shifted(v_ref[...], hv_ref, 2 * RWKV_DIM, 3 * RWKV_DIM)
    lora = shifted(l_ref[...], hl_ref, 3 * RWKV_DIM, RWKV_PACK)
    xw = lora[:, LORA_XW:LORA_XW + LANES]
    xa = lora[:, LORA_XA:LORA_XA + LANES]
    xg = lora[:, LORA_XG:LORA_XG + 2 * LANES]

    w_pre = w0_ref[...] + jnp.dot(jnp.tanh(xw).astype(BF16), ww_ref[...], preferred_element_type=F32)
    log_decay = -math.exp(-0.5) * jax.nn.sigmoid(w_pre)
    step_in_block = lax.broadcasted_iota(jnp.int32, (rows, 1), 0) % SCAN_BLOCK
    log_cum = log_decay
    shift = 1
    while shift < SCAN_BLOCK:
        log_cum = log_cum + jnp.where(step_in_block >= shift, pltpu.roll(log_cum, shift, 0), 0.0)
        shift *= 2
    cum = jnp.exp(log_cum)
    cum_before = jnp.exp(log_cum - log_decay)
    inv_cum = jnp.exp(-log_cum)
    a = jax.nn.sigmoid(a0_ref[...] + jnp.dot(xa.astype(BF16), wa_ref[...], preferred_element_type=F32))
    g = jnp.dot(jax.nn.sigmoid(xg).astype(BF16), wg_ref[...], preferred_element_type=F32)

    kk = k * kk_ref[...]
    kk = kk * lax.rsqrt(jnp.maximum(_seg_sum(kk * kk, ones_ref[...]), 1e-24))
    k_mod = k * (1.0 + (a - 1.0) * ka_ref[...])
    rk_sum = _dot3(r * k_mod * rk_ref[...], hsum_ref[...])

    _store_head_pairs(ab_ref, -kk * cum_before, kk * a * inv_cum)
    _store_head_pairs(kr_ref, k_mod * inv_cum, r * cum)
    _store_head_pairs(cum_ref, cum, cum)
    for vh in range(V_HI):
        vo_ref[vh] = v[:, vh * LANES:(vh + 1) * LANES]
    g_ref[...] = g
    bonus_ref[...] = _tile_lanes(rk_sum, V_HI) * v


def _rwkv_prep(p_all, state_rows, pw, tb, nblk):
    m = p_all.shape[0]
    nt = m // tb
    assert tb % SUBLANES == 0 and nt == state_rows.shape[0] * nblk
    row = lambda c: pl.BlockSpec((tb, RWKV_DIM), lambda i, c=c: (i, c))
    before = lambda i: jnp.maximum(i * (tb // SUBLANES) - 1, 0)
    halo = lambda c: pl.BlockSpec((SUBLANES, RWKV_DIM), lambda i, c=c: (before(i), c))
    full = lambda a: pl.BlockSpec(a.shape, lambda i: (0,) * a.ndim)
    params = (pw["mu"], pw["w0"], pw["a0"], pw["k_k"], pw["k_a"], pw["r_k"],
              pw["w_lora"], pw["a_lora"], pw["g_lora"], pw["ones_bd"], pw["hsum_nat"])
    per_head = jax.ShapeDtypeStruct((HEADS, m, LANES), F32)
    per_vhi = jax.ShapeDtypeStruct((V_HI, m, LANES), F32)
    narrow = jax.ShapeDtypeStruct((m, RWKV_DIM), F32)
    return pl.pallas_call(
        functools.partial(_rwkv_prep_kernel, nblk=nblk),
        grid=(nt,),
        in_specs=[row(COL_R // RWKV_DIM), row(COL_K // RWKV_DIM), row(COL_V // RWKV_DIM),
                  pl.BlockSpec((tb, LORA_W), lambda i: (i, COL_LORA // LORA_W)),
                  halo(COL_R // RWKV_DIM), halo(COL_K // RWKV_DIM), halo(COL_V // RWKV_DIM),
                  pl.BlockSpec((SUBLANES, LORA_W), lambda i: (before(i), COL_LORA // LORA_W)),
                  pl.BlockSpec((None, 1, RWKV_PACK), lambda i: (i // nblk, 0, 0))]
                 + [full(a) for a in params],
        out_specs=[pl.BlockSpec((HEADS, tb, LANES), lambda i: (0, i, 0))] * 3
                  + [pl.BlockSpec((V_HI, tb, LANES), lambda i: (0, i, 0))]
                  + [pl.BlockSpec((tb, RWKV_DIM), lambda i: (i, 0))] * 2,
        out_shape=[per_head, per_head, per_head, per_vhi, narrow, narrow],
        compiler_params=_cparams(("parallel",)),
        name="rwkv_prep",
    )(p_all, p_all, p_all, p_all, p_all, p_all, p_all, p_all, state_rows, *params)


def _sublane_allsum(x):
    x = x + pltpu.roll(x, 4, 0)
    x = x + pltpu.roll(x, 2, 0)
    return x + pltpu.roll(x, 1, 0)


PACKED_ROW = (0, 4, 2, 6, 1, 5, 3, 7)


def _packed_sublane_sums(a):
    sub = lax.broadcasted_iota(jnp.int32, (SUBLANES, LANES), 0)
    top_half = sub < 4
    z = [jnp.where(top_half, a[2 * j] + pltpu.roll(a[2 * j], 4, 0), a[2 * j + 1] + pltpu.roll(a[2 * j + 1], 4, 0))
         for j in range(4)]
    first_pair = (sub % 4) < 2
    w = [jnp.where(first_pair, z[2 * j] + pltpu.roll(z[2 * j], 6, 0), z[2 * j + 1] + pltpu.roll(z[2 * j + 1], 2, 0))
         for j in range(2)]
    return jnp.where(sub % 2 == 0, w[0] + pltpu.roll(w[0], 7, 0), w[1] + pltpu.roll(w[1], 1, 0))


def _scan_kernel(ab_ref, kr_ref, cum_ref, v_ref, s0_ref, y_ref, st_ref, ops_ref, blk_ref, *, tb, nblk):
    @pl.when(pl.program_id(0) % nblk == 0)
    def _():
        st_ref[...] = s0_ref[...]

    def col_tiles(ref, t):
        rep = jnp.concatenate(
            [jnp.broadcast_to(ref[h, pl.ds(t, 1), :], (V_LO, LANES)) for h in range(HEADS)], axis=0)
        return rep.T

    def stage(t, slot):
        ab = col_tiles(ab_ref, t)
        kr = col_tiles(kr_ref, t)
        ops_ref[slot, 0] = ab[:HEAD_DIM]
        ops_ref[slot, 1] = ab[HEAD_DIM:]
        ops_ref[slot, 2] = kr[:HEAD_DIM]
        ops_ref[slot, 3] = kr[HEAD_DIM:]

    def op_tile(slot, which, kb):
        return ops_ref[slot, which, kb * SUBLANES:(kb + 1) * SUBLANES, :]

    def step(t, slot, block_end):
        ksl = lambda kb: slice(kb * SUBLANES, (kb + 1) * SUBLANES)
        acc = [None] * V_HI
        for kb in range(K_BLOCKS):
            alpha = op_tile(slot, 0, kb)
            for vh in range(V_HI):
                p = st_ref[vh, ksl(kb), :] * alpha
                acc[vh] = p if acc[vh] is None else acc[vh] + p
        u = [_sublane_allsum(acc[vh]) for vh in range(V_HI)]
        vb = [jnp.broadcast_to(v_ref[vh, pl.ds(t, 1), :], (SUBLANES, LANES)) for vh in range(V_HI)]
        yacc = [None] * V_HI
        for kb in range(K_BLOCKS):
            beta = op_tile(slot, 1, kb)
            kmod = op_tile(slot, 2, kb)
            rr = op_tile(slot, 3, kb)
            for vh in range(V_HI):
                s_new = st_ref[vh, ksl(kb), :] + beta * u[vh] + kmod * vb[vh]
                p = s_new * rr
                yacc[vh] = p if yacc[vh] is None else yacc[vh] + p
                if block_end:
                    s_new = s_new * blk_ref[ksl(kb), :]
                st_ref[vh, ksl(kb), :] = s_new
        ysum = _packed_sublane_sums(yacc)
        for vh in range(V_HI):
            y_ref[vh, pl.ds(t, 1), :] = ysum[PACKED_ROW[vh]:PACKED_ROW[vh] + 1, :]

    stage(0, 0)

    def body(i, carry):
        base = pl.multiple_of(i * SCAN_BLOCK, SCAN_BLOCK)
        following = pl.multiple_of(jnp.minimum(base + SCAN_BLOCK, tb - SCAN_BLOCK), SCAN_BLOCK)
        blk_ref[...] = col_tiles(cum_ref, base + (SCAN_BLOCK - 1))[:HEAD_DIM]
        for j in range(SCAN_BLOCK):
            nxt = base + (j + 1) if j + 1 < SCAN_BLOCK else following
            stage(nxt, (j + 1) % 2)
            step(base + j, j % 2, j + 1 == SCAN_BLOCK)
        return carry

    lax.fori_loop(0, tb // SCAN_BLOCK, body, 0)


SCAN_STATE = (V_HI, HEAD_DIM, LANES)
SCAN_SCRATCH = [pltpu.VMEM((2, 4, HEAD_DIM, LANES), F32), pltpu.VMEM((HEAD_DIM, LANES), F32)]


def _rwkv_scan(ab, kr, cum, v, s0, tb, nblk):
    steps = v.shape[1]
    nseq = s0.shape[0]
    grid = steps // tb
    assert grid == nseq * nblk and tb % SCAN_BLOCK == 0 and SCAN_BLOCK % SUBLANES == 0
    per_head = pl.BlockSpec((HEADS, tb, LANES), lambda i: (0, i, 0))
    per_vhi = pl.BlockSpec((V_HI, tb, LANES), lambda i: (0, i, 0))
    state = pl.BlockSpec((None,) + SCAN_STATE, lambda i: (i // nblk, 0, 0, 0))
    return pl.pallas_call(
        functools.partial(_scan_kernel, tb=tb, nblk=nblk),
        grid=(grid,),
        in_specs=[per_head, per_head, per_head, per_vhi, state],
        out_specs=[per_vhi, state],
        out_shape=[jax.ShapeDtypeStruct((V_HI, steps, LANES), F32),
                   jax.ShapeDtypeStruct((nseq,) + SCAN_STATE, F32)],
        scratch_shapes=SCAN_SCRATCH,
        compiler_params=_cparams(("arbitrary",)),
        name="rwkv_scan",
    )(ab, kr, cum, v, s0)


def _state_to_scan(s):
    b = s.shape[0]
    s = s.reshape(b, HEADS, V_HI, V_LO, HEAD_DIM)
    return s.transpose(0, 2, 4, 1, 3).reshape(b, V_HI, HEAD_DIM, LANES)


def _state_from_scan(st):
    b = st.shape[0]
    st = st.reshape(b, V_HI, HEAD_DIM, HEADS, V_LO)
    return st.transpose(0, 3, 1, 4, 2).reshape(b, HEADS, HEAD_DIM, HEAD_DIM)


def _rope_tile(x, cos, sin_signed):
    lane = lax.broadcasted_iota(jnp.int32, x.shape, 1)
    first_half = (lane % HEAD_DIM) < (HEAD_DIM // 2)
    rot = jnp.where(first_half, pltpu.roll(x, LANES - HEAD_DIM // 2, 1), pltpu.roll(x, HEAD_DIM // 2, 1))
    return x * cos + rot * sin_signed


def _rope_kernel(q_ref, kv_ref, cos_ref, sin_ref, qo_ref, ko_ref, vo_ref):
    cos = cos_ref[...]
    sin = sin_ref[...]
    scale = HEAD_DIM ** -0.5
    assert math.log2(scale).is_integer()
    for j in range(Q_DIM // LANES):
        sl = slice(j * LANES, (j + 1) * LANES)
        qo_ref[:, sl] = (_rope_tile(q_ref[:, sl], cos, sin) * scale).astype(BF16)
    ko_ref[...] = _rope_tile(kv_ref[:, :KV_DIM], cos, sin)
    vo_ref[...] = kv_ref[:, KV_DIM:]


def _rope(p_all, cos, sin, tb):
    m = p_all.shape[0]
    return pl.pallas_call(
        _rope_kernel,
        grid=(m // tb,),
        in_specs=[pl.BlockSpec((tb, Q_DIM), lambda i: (i, COL_Q // Q_DIM)),
                  pl.BlockSpec((tb, 2 * KV_DIM), lambda i: (i, COL_KV // (2 * KV_DIM))),
                  pl.BlockSpec((tb, LANES), lambda i: (i, 0)),
                  pl.BlockSpec((tb, LANES), lambda i: (i, 0))],
        out_specs=[pl.BlockSpec((tb, Q_DIM), lambda i: (i, 0)),
                   pl.BlockSpec((tb, KV_DIM), lambda i: (i, 0)),
                   pl.BlockSpec((tb, KV_DIM), lambda i: (i, 0))],
        out_shape=[jax.ShapeDtypeStruct((m, Q_DIM), BF16),
                   jax.ShapeDtypeStruct((m, KV_DIM), F32),
                   jax.ShapeDtypeStruct((m, KV_DIM), F32)],
        compiler_params=_cparams(("parallel",)),
        name="rope",
    )(p_all, p_all, cos, sin)


def _rope_tables(pos):
    half = HEAD_DIM // 2
    inv_freq = ROPE_THETA ** (-jnp.arange(half, dtype=F32) / half)
    ang = pos.astype(F32)[:, None] * inv_freq[None, :]
    cos = jnp.cos(ang)
    sin = jnp.sin(ang)
    cos = jnp.concatenate([cos, cos, cos, cos], axis=1)
    sin = jnp.concatenate([-sin, sin, -sin, sin], axis=1)
    return cos, sin


PAIRS = GROUP // 2


def _attend(q_lanes, tq, sink_ref, k_parts, v_parts, band_chunk):
    nkeys = sum(p.shape[0] for p in k_parts)
    nk_pad = pl.cdiv(nkeys + 1, LANES) * LANES
    pad = jnp.zeros((nk_pad - nkeys, LANES), F32)
    k_all = jnp.concatenate(list(k_parts) + [pad], axis=0)
    v_all = jnp.concatenate(list(v_parts) + [pad], axis=0)
    low = lax.broadcasted_iota(jnp.int32, (nk_pad, LANES), 1) < HEAD_DIM
    k_swap = pltpu.roll(k_all, HEAD_DIM, 1)
    v_swap = pltpu.roll(v_all, HEAD_DIM, 1)
    col = lax.broadcasted_iota(jnp.int32, (1, nk_pad), 1)
    visible = col < nkeys
    if band_chunk is not None:
        first_valid = N_META + jnp.maximum(2 - band_chunk, 0) * CHUNK
        visible = visible & ((col < N_META) | (col >= first_valid))
    bias = jnp.where(visible, 0.0, -jnp.inf).astype(F32)
    is_sink = col == nkeys
    ones_kv = (lax.broadcasted_iota(jnp.int32, (nk_pad, LANES), 0) <= nkeys).astype(BF16)
    rows = PAIRS * tq
    scores, values = [], []
    for kvh in range(KV_HEADS):
        k_src, k_oth = (k_all, k_swap) if kvh == 0 else (k_swap, k_all)
        v_src, v_oth = (v_all, v_swap) if kvh == 0 else (v_swap, v_all)
        halves = ((jnp.where(low, k_src, 0.0), jnp.where(low, v_src, 0.0)),
                  (jnp.where(low, 0.0, k_oth), jnp.where(low, 0.0, v_oth)))
        qp = jnp.concatenate([q_lanes(kvh * PAIRS + p) for p in range(PAIRS)], axis=0)
        for parity, (kx, vx) in enumerate(halves):
            s = lax.dot_general(qp, kx.astype(BF16), (((1,), (1,)), ((), ())), preferred_element_type=F32) + bias
            for p in range(PAIRS):
                h = kvh * GROUP + 2 * p + parity
                scores.append(jnp.where(is_sink, sink_ref[:, h:h + 1], s[p * tq:(p + 1) * tq]))
            values.append(jnp.concatenate([vx.astype(BF16), ones_kv], axis=1))
    s_all = jnp.concatenate(scores, axis=0)
    e = jnp.exp(s_all - jnp.max(s_all, axis=1, keepdims=True)).astype(BF16)
    groups = []
    for kvh in range(KV_HEADS):
        total = None
        for parity in range(2):
            idx = kvh * 2 + parity
            ov = jnp.dot(e[idx * rows:(idx + 1) * rows], values[idx], preferred_element_type=F32)
            o = ov[:, :LANES] / ov[:, LANES:]
            total = o if total is None else total + o
        groups.extend(total[p * tq:(p + 1) * tq] for p in range(PAIRS))
    return jnp.concatenate(groups, axis=1).astype(BF16)


def _attn_kernel(*refs, nseg, band_mask):
    q_ref, sink_ref = refs[0], refs[1]
    k_refs = refs[2:2 + nseg]
    v_refs = refs[2 + nseg:2 + 2 * nseg]
    o_ref = refs[2 + 2 * nseg]
    o_ref[...] = _attend(lambda j: q_ref[:, j * LANES:(j + 1) * LANES], q_ref.shape[0], sink_ref,
                         [r[...] for r in k_refs], [r[...] for r in v_refs],
                         pl.program_id(0) if band_mask else None)


def _attention(q, sinks, k_segs, v_segs, tq, grid, band_mask):
    nseg = len(k_segs)
    seg_specs = [pl.BlockSpec(bs, im) for (_, bs, im) in k_segs + v_segs]
    return pl.pallas_call(
        functools.partial(_attn_kernel, nseg=nseg, band_mask=band_mask),
        grid=(grid,),
        in_specs=[pl.BlockSpec((tq, Q_DIM), lambda i: (i, 0)),
                  pl.BlockSpec((1, HEADS), lambda i: (0, 0))] + seg_specs,
        out_specs=pl.BlockSpec((tq, Q_DIM), lambda i: (i, 0)),
        out_shape=jax.ShapeDtypeStruct((grid * tq, Q_DIM), BF16),
        compiler_params=_cparams(("parallel",)),
        name="attention",
    )(q, sinks, *[a for (a, _, _) in k_segs + v_segs])


def _mix_kernel(y_ref, bonus_ref, g_ref, yb_ref, ga_ref, gb_ref, lng_ref, lnb_ref, hsum_ref,
                wa_ref, wb_ref, o_ref, ya_ref):
    hsum = hsum_ref[...]
    mean = _dot3(sum(y_ref[vh] for vh in range(V_HI)), hsum) * (1.0 / HEAD_DIM)
    d = [y_ref[vh] - mean for vh in range(V_HI)]
    var = _dot3(sum(dv * dv for dv in d), hsum) * (1.0 / HEAD_DIM)
    inv = lax.rsqrt(var + RWKV_LN_EPS)
    for vh in range(V_HI):
        sl = slice(vh * LANES, (vh + 1) * LANES)
        yn = d[vh] * inv * lng_ref[:, sl] + lnb_ref[:, sl]
        ya_ref[:, sl] = ((yn + bonus_ref[:, sl]) * g_ref[:, sl]).astype(BF16)

    pa = jnp.dot(ya_ref[...], wa_ref[...], preferred_element_type=F32)
    pb = jnp.dot(yb_ref[...], wb_ref[...], preferred_element_type=F32)
    o_ref[...] = (jax.nn.sigmoid(ga_ref[...]) * pa + jax.nn.sigmoid(gb_ref[...]) * pb).astype(BF16)


def _mix(y, bonus, g, yb, p_all, ln_g, ln_b, hsum, wa, wb, tm):
    m = bonus.shape[0]
    row = pl.BlockSpec((tm, RWKV_DIM), lambda i: (i, 0))
    one = lambda a: pl.BlockSpec(a.shape, lambda i: (0,) * a.ndim)
    resident = lambda a: pl.BlockSpec(a.shape, lambda i: (0, 0), pipeline_mode=pl.Buffered(1))
    return pl.pallas_call(
        _mix_kernel,
        grid=(m // tm,),
        in_specs=[pl.BlockSpec((V_HI, tm, LANES), lambda i: (0, i, 0)), row, row, row,
                  pl.BlockSpec((tm, D_MODEL), lambda i: (i, COL_GA // D_MODEL)),
                  pl.BlockSpec((tm, D_MODEL), lambda i: (i, COL_GB // D_MODEL)),
                  one(ln_g), one(ln_b), one(hsum), resident(wa), resident(wb)],
        out_specs=pl.BlockSpec((tm, D_MODEL), lambda i: (i, 0)),
        out_shape=jax.ShapeDtypeStruct((m, D_MODEL), BF16),
        scratch_shapes=[pltpu.VMEM((tm, RWKV_DIM), BF16)],
        compiler_params=_cparams(("parallel",)),
        name="branch_mix",
    )(y, bonus, g, yb, p_all, p_all, ln_g, ln_b, hsum, wa, wb)


def _out_norm_kernel(mix_ref, w_ref, x_ref, g_ref, o_ref):
    z = jnp.dot(mix_ref[...], w_ref[...], preferred_element_type=F32)
    o_ref[...] = x_ref[...] + _rms(z, g_ref[...])


def _out_norm(mixed, w, x, g, tm):
    m = x.shape[0]
    return pl.pallas_call(
        _out_norm_kernel,
        grid=(m // tm,),
        in_specs=[pl.BlockSpec((tm, D_MODEL), lambda i: (i, 0)),
                  pl.BlockSpec((D_MODEL, D_MODEL), lambda i: (0, 0)),
                  pl.BlockSpec((tm, D_MODEL), lambda i: (i, 0)),
                  pl.BlockSpec((1, D_MODEL), lambda i: (0, 0))],
        out_specs=pl.BlockSpec((tm, D_MODEL), lambda i: (i, 0)),
        out_shape=jax.ShapeDtypeStruct((m, D_MODEL), F32),
        compiler_params=_cparams(("parallel",)),
        name="out_norm",
    )(mixed, w, x, g)


HALO = 16


def _gelu_tanh(x):
    return 0.5 * x * (1.0 + jnp.tanh(0.7978845608028654 * (x + 0.044715 * x * x * x)))


def _ffn_kernel(*refs, seq_len):
    if seq_len is None:
        (x_ref, halo_ref, gpre_ref, wg_ref, wv_ref, wd_ref, cw_ref, cb_ref, gpost_ref,
         o_ref, tail_ref, h_ref) = refs
    else:
        (x_ref, sa_ref, sb_ref, gpre_ref, wg_ref, wv_ref, wd_ref, cw_ref, cb_ref, gpost_ref,
         o_ref, tail_ref, h_ref) = refs
    acc_ref = o_ref
    tm = x_ref.shape[0]
    j = pl.program_id(1)
    top = HALO if seq_len is None else 0

    @pl.when(j == 0)
    def _():
        if seq_len is None:
            h_ref[:HALO] = _rms(halo_ref[...], gpre_ref[...]).astype(BF16)
        h_ref[top:] = _rms(x_ref[...], gpre_ref[...]).astype(BF16)
        acc_ref[...] = jnp.zeros_like(acc_ref)

    h_all = h_ref[...]
    gate_all = jnp.dot(h_all, wg_ref[...], preferred_element_type=F32)
    val = jnp.dot(h_all[top:], wv_ref[...], preferred_element_type=F32)
    g0 = gate_all[top:]
    g1 = pltpu.roll(gate_all, 1, 0)[top:]
    g2 = pltpu.roll(gate_all, 2, 0)[top:]
    if seq_len is not None:
        nseq = tm // seq_len
        tf = g0.shape[1]
        tau = lax.broadcasted_iota(jnp.int32, (tm, 1), 0) % seq_len
        expand = lambda s: jnp.concatenate(
            [jnp.broadcast_to(s[b:b + 1, :], (seq_len, tf)) for b in range(nseq)], axis=0)
        st_a = expand(sa_ref[...])
        st_b = expand(sb_ref[...])
        g1 = jnp.where(tau == 0, st_b, g1)
        g2 = jnp.where(tau == 0, st_a, jnp.where(tau == 1, st_b, g2))
    conv = cb_ref[...] + g2 * cw_ref[0:1, :] + g1 * cw_ref[1:2, :] + g0 * cw_ref[2:3, :]
    act = (_gelu_tanh(conv) * val).astype(BF16)
    acc_ref[...] += jnp.dot(act, wd_ref[...], preferred_element_type=F32)

    if seq_len is None:
        tail_ref[...] = g0[tm - SUBLANES:]
    else:
        for b in range(tm // seq_len):
            tail_ref[b] = g0[(b + 1) * seq_len - SUBLANES:(b + 1) * seq_len]

    @pl.when(j == pl.num_programs(1) - 1)
    def _():
        o_ref[...] = x_ref[...] + _rms(acc_ref[...], gpost_ref[...])


def _ffn(x1, conv_prev, fw, tm, tf, seq_len):
    m = x1.shape[0] - (HALO if seq_len is None else 0)
    nm, nf = m // tm, D_FF // tf
    one = lambda a: pl.BlockSpec(a.shape, lambda i, j: (0,) * a.ndim)
    wspecs = [pl.BlockSpec((D_MODEL, tf), lambda i, j: (0, j)),
              pl.BlockSpec((D_MODEL, tf), lambda i, j: (0, D_FF // tf + j)),
              pl.BlockSpec((tf, D_MODEL), lambda i, j: (j, 0)),
              pl.BlockSpec((FFN_CONV, tf), lambda i, j: (0, j)),
              pl.BlockSpec((1, tf), lambda i, j: (0, j)),
              one(fw["g_post"])]
    wargs = (fw["w_up"], fw["w_up"], fw["w_down"], fw["conv_w"], fw["conv_b"], fw["g_post"])
    if seq_len is None:
        assert tm % HALO == 0
        pre_specs = [pl.BlockSpec((pl.Element(tm), pl.Element(D_MODEL)),
                                  lambda i, j: (pl.multiple_of(HALO + i * tm, HALO), 0)),
                     pl.BlockSpec((HALO, D_MODEL), lambda i, j: (i * (tm // HALO), 0)),
                     one(fw["g_pre"])]
        pre_args = (x1, x1, fw["g_pre"])
        tail_shape = (nm, SUBLANES, D_FF)
        tail_spec = pl.BlockSpec((None, SUBLANES, tf), lambda i, j: (i, 0, j))
        hrows = tm + HALO
    else:
        nseq = tm // seq_len
        assert nm == 1
        pre_specs = [pl.BlockSpec((tm, D_MODEL), lambda i, j: (i, 0)),
                     pl.BlockSpec((nseq, tf), lambda i, j: (0, j)),
                     pl.BlockSpec((nseq, tf), lambda i, j: (0, j)),
                     one(fw["g_pre"])]
        pre_args = (x1, conv_prev[:, 0, :], conv_prev[:, 1, :], fw["g_pre"])
        tail_shape = (nseq, SUBLANES, D_FF)
        tail_spec = pl.BlockSpec((nseq, SUBLANES, tf), lambda i, j: (0, 0, j))
        hrows = tm
    return pl.pallas_call(
        functools.partial(_ffn_kernel, seq_len=seq_len),
        grid=(nm, nf),
        in_specs=pre_specs + wspecs,
        out_specs=[pl.BlockSpec((tm, D_MODEL), lambda i, j: (i, 0), pipeline_mode=pl.Buffered(1)), tail_spec],
        out_shape=[jax.ShapeDtypeStruct((m, D_MODEL), F32),
                   jax.ShapeDtypeStruct(tail_shape, F32)],
        scratch_shapes=[pltpu.VMEM((hrows, D_MODEL), BF16)],
        compiler_params=_cparams(("parallel", "arbitrary")),
        name="conv_ffn",
    )(*pre_args, *wargs)


def _pack_lora(x, axis=1):
    cut = lambda a, b: lax.slice_in_dim(x, a, b, axis=axis)

    def z(n):
        shape = list(x.shape)
        shape[axis] = n
        return jnp.zeros(shape, x.dtype)

    xw = cut(0, DECAY_LORA)
    xa = cut(DECAY_LORA, DECAY_LORA + ICLR_LORA)
    xg = cut(DECAY_LORA + ICLR_LORA, DECAY_LORA + ICLR_LORA + GATE_LORA)
    return jnp.concatenate([xw, z(LORA_XA - DECAY_LORA), xa, z(LORA_XG - LORA_XA - ICLR_LORA),
                            xg, z(LORA_W - LORA_XG - GATE_LORA)], axis=axis)


def _unpack_lora(x):
    return jnp.concatenate([x[:, LORA_XW:LORA_XW + DECAY_LORA], x[:, LORA_XA:LORA_XA + ICLR_LORA],
                            x[:, LORA_XG:LORA_XG + GATE_LORA]], axis=1)


def _pack_in_proj_t(wt):
    c = 3 * RWKV_DIM
    q0 = RWKV_PROJ
    kv0 = q0 + Q_DIM
    g0 = kv0 + 2 * KV_DIM
    pieces = [wt[:2 * RWKV_DIM], _to_scan_order(wt[2 * RWKV_DIM:c], 0), wt[q0:kv0],
              wt[g0:g0 + 2 * D_MODEL], _pack_lora(wt[c:RWKV_PROJ], axis=0), wt[kv0:g0]]
    return jnp.concatenate([p.astype(BF16) for p in pieces], axis=0)


def _pack_rwkv_row(x):
    c = 3 * RWKV_DIM
    return jnp.concatenate([x[:, :2 * RWKV_DIM], _to_scan_order(x[:, 2 * RWKV_DIM:c], 1), _pack_lora(x[:, c:])],
                           axis=1)


def _unpack_rwkv_row(p_row):
    return jnp.concatenate([p_row[:, :2 * RWKV_DIM], _to_natural_order(p_row[:, COL_V:COL_V + RWKV_DIM], 1),
                            _unpack_lora(p_row[:, COL_LORA:COL_LORA + LORA_W])], axis=1)


def _pick_tile(m, candidates):
    for c in candidates:
        if m % c == 0:
            return c
    raise ValueError(f"no row tile in {candidates} divides {m}")


def _pad_rows(w, rows):
    return jnp.concatenate([w, jnp.zeros((rows - w.shape[0], w.shape[1]), w.dtype)], axis=0)


def _trunk(x, lw, *, tm, tm_mix, tb_prep, shift_rows, nseq, cos, sin, scan_attend, ffn_tm,
           ffn_seq_len, conv_prev):
    seq_rows = x.shape[0] // nseq
    p_all = _norm_matmul(x, lw["g_pre_mix"], lw["w_in_t"], tm, 1280)
    ab, kr, cum, v_scan, g, bonus = _rwkv_prep(p_all, shift_rows, lw, tb_prep, seq_rows // tb_prep)
    q_roped, k_roped, v_att = _rope(p_all, cos, sin, tm)
    y, s_fin, att = scan_attend((ab, kr, cum, v_scan), q_roped, k_roped, v_att)

    mixed = _mix(y, bonus, g, att, p_all, lw["ln_g"], lw["ln_b"], lw["hsum_scan"],
                 lw["w_branch_a"], lw["w_branch_b"], tm_mix)
    x1 = _out_norm(mixed, lw["w_out"], x, lw["g_post_mix"], tm_mix)
    x2, tail = _ffn(x1, conv_prev, lw["ffn"], ffn_tm, 512, ffn_seq_len)
    return x2, p_all, k_roped, v_att, s_fin, tail


def kernel(x_prompt, x_sample, cache_meta_k, cache_meta_v, cache_win_k, cache_win_v, state_rwkv, state_shift, state_ffn_conv, meta_tokens, g_pre_mix, w_in, rwkv_mu, rwkv_w0, rwkv_w_lora_b, rwkv_a0, rwkv_a_lora_b, rwkv_g_lora_b, rwkv_k_k, rwkv_k_a, rwkv_r_k, rwkv_ln_g, rwkv_ln_b, attn_sinks, w_branch_a, w_branch_b, w_out, g_post_mix, g_pre_ffn, w_up, ffn_conv_w, ffn_conv_b, w_down, g_post_ffn):
    depth = w_in.shape[0]
    assert depth == 1
    l = 0
    seq = x_prompt.shape[1]
    nb, tn = x_sample.shape[0], x_sample.shape[1]
    row = lambda a: a.reshape(1, -1)
    lane = np.arange(LANES)
    chan = np.arange(RWKV_DIM)
    ones_bd = jnp.asarray(lane[:, None] // HEAD_DIM == lane[None, :] // HEAD_DIM, BF16)
    hsum_nat = jnp.asarray((chan // HEAD_DIM)[:, None] == (lane // V_LO)[None, :], BF16)
    hsum_scan = jnp.asarray((lane // V_LO)[:, None] == (lane // V_LO)[None, :], BF16)
    lw = dict(
        g_pre_mix=row(g_pre_mix[l]),
        w_in_t=_pack_in_proj_t(jnp.swapaxes(w_in[l], 0, 1)),
        mu=_pack_rwkv_row(row(rwkv_mu[l])),
        w0=row(rwkv_w0[l]), a0=row(rwkv_a0[l]), k_k=row(rwkv_k_k[l]), k_a=row(rwkv_k_a[l]),
        r_k=row(rwkv_r_k[l]),
        w_lora=_pad_rows(rwkv_w_lora_b[l], LANES).astype(BF16),
        a_lora=_pad_rows(rwkv_a_lora_b[l], LANES).astype(BF16),
        g_lora=_pad_rows(_to_scan_order(rwkv_g_lora_b[l], 1), 2 * LANES).astype(BF16),
        ones_bd=ones_bd, hsum_nat=hsum_nat, hsum_scan=hsum_scan,
        ln_g=row(_to_scan_order(rwkv_ln_g[l], 0)), ln_b=row(_to_scan_order(rwkv_ln_b[l], 0)),
        w_branch_a=_to_scan_order(w_branch_a[l], 0).astype(BF16), w_branch_b=w_branch_b[l].astype(BF16),
        w_out=w_out[l].astype(BF16), g_post_mix=row(g_post_mix[l]),
        ffn=dict(g_pre=row(g_pre_ffn[l]), w_up=w_up[l].astype(BF16), w_down=w_down[l].astype(BF16),
                 conv_w=ffn_conv_w[l], conv_b=row(ffn_conv_b[l]), g_post=row(g_post_ffn[l])),
    )
    sinks = row(attn_sinks[l])

    mp = N_META + seq
    xp = jnp.concatenate([meta_tokens.astype(x_prompt.dtype), x_prompt[0]], axis=0)
    nchunk = seq // CHUNK
    tm_p = _pick_tile(mp, (912, 144, 48))
    tm_mix_p = _pick_tile(mp, (432, 144, 48))
    tb_prep_p = _pick_tile(mp, (144, 48))
    cos_p, sin_p = _rope_tables(jnp.arange(mp, dtype=jnp.int32))

    def scan_attend_prompt(scan_ops, q, k, v):
        scan_tb = _pick_tile(mp, (432, 144, 48))
        y, s_fin = _rwkv_scan(*scan_ops, jnp.zeros((1,) + SCAN_STATE, F32), scan_tb, mp // scan_tb)
        mk, mv = k[:N_META], v[:N_META]
        whole = lambda a: (a, (N_META, KV_DIM), lambda c: (0, 0))
        o_m = _attention(q[:N_META], sinks, [whole(mk)], [whole(mv)], N_META, 1, False)
        kf, vf = k[N_META:], v[N_META:]
        band = lambda a: [(a, (CHUNK, KV_DIM), lambda c, d=d: (jnp.maximum(c - d, 0), 0)) for d in (2, 1, 0)]
        o_f = _attention(q[N_META:], sinks, [whole(mk)] + band(kf), [whole(mv)] + band(vf),
                         CHUNK, nchunk, True)
        return y, s_fin, jnp.concatenate([o_m, o_f], axis=0)

    assert N_META == HALO
    yp, p_p, k_p, v_p, sfin_p, tail_p = _trunk(
        xp, lw, tm=tm_p, tm_mix=tm_mix_p, tb_prep=tb_prep_p, shift_rows=jnp.zeros((1, 1, RWKV_PACK), F32),
        nseq=1, cos=cos_p, sin=sin_p, scan_attend=scan_attend_prompt,
        ffn_tm=_pick_tile(seq, (1024, 64)), ffn_seq_len=None, conv_prev=None)

    ms = nb * tn
    xs = x_sample.reshape(ms, D_MODEL)
    pos_s = N_META + PAST_LEN + jnp.arange(tn, dtype=jnp.int32)
    cos_s, sin_s = _rope_tables(jnp.tile(pos_s, nb))
    nwin = cache_win_k.shape[2]
    cmk = cache_meta_k[l].reshape(nb, N_META, KV_DIM)
    cmv = cache_meta_v[l].reshape(nb, N_META, KV_DIM)
    cwk = cache_win_k[l].reshape(nb, nwin, KV_DIM)
    cwv = cache_win_v[l].reshape(nb, nwin, KV_DIM)

    shift_s = _pack_rwkv_row(state_shift[l].reshape(nb, RWKV_PROJ))[:, None, :]

    def scan_attend_sample(scan_ops, q, k, v):
        y, s_fin = _rwkv_scan(*scan_ops, _state_to_scan(state_rwkv[l]), tn, 1)
        per_b = lambda a, n: (a, (None, n, KV_DIM), lambda b: (b, 0, 0))
        own = lambda a: (a, (tn, KV_DIM), lambda b: (b, 0))
        att = _attention(q, sinks, [per_b(cmk, N_META), per_b(cwk, nwin), own(k)],
                         [per_b(cmv, N_META), per_b(cwv, nwin), own(v)], tn, nb, False)
        return y, s_fin, att

    xs2, p_s, k_s, v_s, sfin_s, tail_s = _trunk(
        xs, lw, tm=ms, tm_mix=ms, tb_prep=tn, shift_rows=shift_s, nseq=nb, cos=cos_s, sin=sin_s,
        scan_attend=scan_attend_sample, ffn_tm=ms, ffn_seq_len=tn, conv_prev=state_ffn_conv[l])

    kv5 = lambda a, n: a.reshape(1, -1, n, KV_HEADS, HEAD_DIM)
    y_prompt = yp[None]
    y_sample = xs2.reshape(nb, tn, D_MODEL)
    prompt_meta_k = kv5(k_p[:N_META], N_META)
    prompt_meta_v = kv5(v_p[:N_META], N_META)
    prompt_win_k = kv5(k_p[mp - WINDOW:], WINDOW)
    prompt_win_v = kv5(v_p[mp - WINDOW:], WINDOW)
    prompt_rwkv = _state_from_scan(sfin_p)[None]
    prompt_shift = _unpack_rwkv_row(p_p[mp - 1:mp]).reshape(1, 1, 1, RWKV_PROJ)
    prompt_ffn_conv = tail_p[-1, SUBLANES - (FFN_CONV - 1):].reshape(1, 1, FFN_CONV - 1, D_FF)
    sample_win_k = kv5(k_s, tn)
    sample_win_v = kv5(v_s, tn)
    sample_rwkv = _state_from_scan(sfin_s)[None]
    sample_shift = _unpack_rwkv_row(p_s[tn - 1::tn]).reshape(1, nb, 1, RWKV_PROJ)
    sample_ffn_conv = tail_s[:, SUBLANES - (FFN_CONV - 1):].reshape(1, nb, FFN_CONV - 1, D_FF)
    return (y_prompt, y_sample, prompt_meta_k, prompt_meta_v, prompt_win_k, prompt_win_v,
            prompt_rwkv, prompt_shift, prompt_ffn_conv, sample_win_k, sample_win_v,
            sample_rwkv, sample_shift, sample_ffn_conv)
```

```python
import functools
import math

import numpy as np
import jax
import jax.numpy as jnp
from jax import lax
from jax.experimental import pallas as pl
from jax.experimental.pallas import tpu as pltpu

F32 = jnp.float32
BF16 = jnp.bfloat16

D_MODEL = 2048
CHUNK = 64
N_META = 16
NORM_EPS = 1e-6
HEADS = 16
HEAD_DIM = 64
RWKV_DIM = HEADS * HEAD_DIM
DECAY_LORA = 64
ICLR_LORA = 64
GATE_LORA = 160
RWKV_PROJ = 3 * RWKV_DIM + DECAY_LORA + ICLR_LORA + GATE_LORA
RWKV_LN_EPS = 64e-5
KV_HEADS = 2
GROUP = HEADS // KV_HEADS
Q_DIM = HEADS * HEAD_DIM
KV_DIM = KV_HEADS * HEAD_DIM
WINDOW = 128
ROPE_THETA = 10000.0
D_FF = 3 * D_MODEL
FFN_CONV = 3
PAST_LEN = 4096

SUBLANES = 8
LANES = 128
V_LO = LANES // HEADS
V_HI = HEAD_DIM // V_LO
K_BLOCKS = HEAD_DIM // SUBLANES
SCAN_BLOCK = SUBLANES

COL_R, COL_K, COL_V, COL_Q = 0, 1024, 2048, 3072
COL_GA, COL_GB = 4096, 6144
COL_LORA = 8192
LORA_W = 512
LORA_XW, LORA_XA, LORA_XG = 0, 128, 256
COL_KV = 8704
N_PACK = 8960
RWKV_PACK = 3 * RWKV_DIM + LORA_W

VMEM_LIMIT = 56 * 1024 * 1024


def _swap_channel_order(x, axis, outer, inner):
    axis = axis % x.ndim
    shape = x.shape
    x = x.reshape(shape[:axis] + (outer, inner, V_LO) + shape[axis + 1:])
    return jnp.swapaxes(x, axis, axis + 1).reshape(shape)


def _to_scan_order(x, axis):
    return _swap_channel_order(x, axis, HEADS, V_HI)


def _to_natural_order(x, axis):
    return _swap_channel_order(x, axis, V_HI, HEADS)


def _cparams(sem):
    return pltpu.CompilerParams(dimension_semantics=sem, vmem_limit_bytes=VMEM_LIMIT)


def _rms(x, g):
    return x * lax.rsqrt(jnp.mean(x * x, axis=-1, keepdims=True) + NORM_EPS) * g


def _split3(x):
    hi = x.astype(BF16)
    r1 = x - hi.astype(F32)
    mid = r1.astype(BF16)
    lo = (r1 - mid.astype(F32)).astype(BF16)
    return hi, mid, lo


def _dot3(x, mat):
    hi, mid, lo = _split3(x)
    acc = jnp.dot(hi, mat, preferred_element_type=F32)
    acc = acc + jnp.dot(mid, mat, preferred_element_type=F32)
    return acc + jnp.dot(lo, mat, preferred_element_type=F32)


def _seg_sum(x, ones_bd):
    hi, mid, lo = _split3(x)
    outs = []
    for j in range(x.shape[1] // LANES):
        sl = slice(j * LANES, (j + 1) * LANES)
        acc = jnp.dot(hi[:, sl], ones_bd, preferred_element_type=F32)
        acc = acc + jnp.dot(mid[:, sl], ones_bd, preferred_element_type=F32)
        acc = acc + jnp.dot(lo[:, sl], ones_bd, preferred_element_type=F32)
        outs.append(acc)
    return jnp.concatenate(outs, axis=1)


def _tile_lanes(x, n):
    return jnp.concatenate([x] * n, axis=1)


def _norm_matmul_kernel(x_ref, g_ref, w_ref, o_ref, h_ref):
    @pl.when(pl.program_id(1) == 0)
    def _():
        h_ref[...] = _rms(x_ref[...], g_ref[...]).astype(BF16)

    o_ref[...] = lax.dot_general(h_ref[...], w_ref[...], (((1,), (1,)), ((), ())),
                                 preferred_element_type=F32)


def _norm_matmul(x, g, w_t, tm, tn):
    m, d = x.shape
    n = w_t.shape[0]
    return pl.pallas_call(
        _norm_matmul_kernel,
        grid=(m // tm, n // tn),
        in_specs=[pl.BlockSpec((tm, d), lambda i, j: (i, 0)),
                  pl.BlockSpec((1, d), lambda i, j: (0, 0)),
                  pl.BlockSpec((tn, d), lambda i, j: (j, 0))],
        out_specs=pl.BlockSpec((tm, tn), lambda i, j: (i, j)),
        out_shape=jax.ShapeDtypeStruct((m, n), F32),
        scratch_shapes=[pltpu.VMEM((tm, d), BF16)],
        compiler_params=_cparams(("parallel", "arbitrary")),
        name="norm_matmul",
    )(x, g, w_t)


def _store_head_pairs(ref, a, b):
    lane = lax.broadcasted_iota(jnp.int32, (a.shape[0], LANES), 1)
    low = lane < HEAD_DIM
    for j in range(a.shape[1] // LANES):
        sl = slice(j * LANES, (j + 1) * LANES)
        aj, bj = a[:, sl], b[:, sl]
        ref[2 * j] = jnp.where(low, aj, pltpu.roll(bj, HEAD_DIM, 1))
        ref[2 * j + 1] = jnp.where(low, pltpu.roll(aj, HEAD_DIM, 1), bj)


def _rwkv_prep_kernel(r_ref, k_ref, v_ref, l_ref, hr_ref, hk_ref, hv_ref, hl_ref, state_ref,
                      q_ref, kv_ref, cos_ref, sin_ref,
                      mu_ref, w0_ref, a0_ref, kk_ref, ka_ref,
                      rk_ref, ww_ref, wa_ref, wg_ref, ones_ref, hsum_ref,
                      ab_ref, kr_ref, cum_ref, vo_ref, g_ref, bonus_ref, aq_ref, ak_ref, av_ref, *, nblk):
    rows = r_ref.shape[0]
    first = lax.broadcasted_iota(jnp.int32, (rows, 1), 0) == 0
    seq_start = pl.program_id(0) % nblk == 0

    def shifted(x, halo_ref, lo, hi):
        before = jnp.where(seq_start, state_ref[:, lo:hi], halo_ref[SUBLANES - 1:SUBLANES, :])
        prev = jnp.where(first, before, pltpu.roll(x, 1, 0))
        return x + (prev - x) * mu_ref[:, lo:hi]

    r = shifted(r_ref[...], hr_ref, 0, RWKV_DIM)
    k = shifted(k_ref[...], hk_ref, RWKV_DIM, 2 * RWKV_DIM)
    v = shifted(v_ref[...], hv_ref, 2 * RWKV_DIM, 3 * RWKV_DIM)
    lora = shifted(l_ref[...], hl_ref, 3 * RWKV_DIM, RWKV_PACK)
    xw = lora[:, LORA_XW:LORA_XW + LANES]
    xa = lora[:, LORA_XA:LORA_XA + LANES]
    xg = lora[:, LORA_XG:LORA_XG + 2 * LANES]

    w_pre = w0_ref[...] + jnp.dot(jnp.tanh(xw).astype(BF16), ww_ref[...], preferred_element_type=F32)
    log_decay = -math.exp(-0.5) * jax.nn.sigmoid(w_pre)
    step_in_block = lax.broadcasted_iota(jnp.int32, (rows, 1), 0) % SCAN_BLOCK
    log_cum = log_decay
    shift = 1
    while shift < SCAN_BLOCK:
        log_cum = log_cum + jnp.where(step_in_block >= shift, pltpu.roll(log_cum, shift, 0), 0.0)
        shift *= 2
    cum = jnp.exp(log_cum)
    cum_before = jnp.exp(log_cum - log_decay)
    inv_cum = jnp.exp(-log_cum)
    a = jax.nn.sigmoid(a0_ref[...] + jnp.dot(xa.astype(BF16), wa_ref[...], preferred_element_type=F32))
    g = jnp.dot(jax.nn.sigmoid(xg).astype(BF16), wg_ref[...], preferred_element_type=F32)

    kk = k * kk_ref[...]
    kk = kk * lax.rsqrt(jnp.maximum(_seg_sum(kk * kk, ones_ref[...]), 1e-24))
    k_mod = k * (1.0 + (a - 1.0) * ka_ref[...])
    rk_sum = _dot3(r * k_mod * rk_ref[...], hsum_ref[...])

    _store_head_pairs(ab_ref, -kk * cum_before, kk * a * inv_cum)
    _store_head_pairs(kr_ref, k_mod * inv_cum, r * cum)
    _store_head_pairs(cum_ref, cum, cum)
    for vh in range(V_HI):
        vo_ref[vh] = v[:, vh * LANES:(vh + 1) * LANES]
    g_ref[...] = g
    bonus_ref[...] = _tile_lanes(rk_sum, V_HI) * v
    _rope_rows(q_ref, kv_ref, cos_ref, sin_ref, aq_ref, ak_ref, av_ref)


def _rwkv_prep(p_all, state_rows, pw, cos, sin, tb, nblk):
    m = p_all.shape[0]
    nt = m // tb
    assert tb % SUBLANES == 0 and nt == state_rows.shape[0] * nblk
    row = lambda c: pl.BlockSpec((tb, RWKV_DIM), lambda i, c=c: (i, c))
    before = lambda i: jnp.maximum(i * (tb // SUBLANES) - 1, 0)
    halo = lambda c: pl.BlockSpec((SUBLANES, RWKV_DIM), lambda i, c=c: (before(i), c))
    full = lambda a: pl.BlockSpec(a.shape, lambda i: (0,) * a.ndim)
    params = (pw["mu"], pw["w0"], pw["a0"], pw["k_k"], pw["k_a"], pw["r_k"],
              pw["w_lora"], pw["a_lora"], pw["g_lora"], pw["ones_bd"], pw["hsum_nat"])
    per_head = jax.ShapeDtypeStruct((HEADS, m, LANES), F32)
    per_vhi = jax.ShapeDtypeStruct((V_HI, m, LANES), F32)
    narrow = jax.ShapeDtypeStruct((m, RWKV_DIM), F32)
    return pl.pallas_call(
        functools.partial(_rwkv_prep_kernel, nblk=nblk),
        grid=(nt,),
        in_specs=[row(COL_R // RWKV_DIM), row(COL_K // RWKV_DIM), row(COL_V // RWKV_DIM),
                  pl.BlockSpec((tb, LORA_W), lambda i: (i, COL_LORA // LORA_W)),
                  halo(COL_R // RWKV_DIM), halo(COL_K // RWKV_DIM), halo(COL_V // RWKV_DIM),
                  pl.BlockSpec((SUBLANES, LORA_W), lambda i: (before(i), COL_LORA // LORA_W)),
                  pl.BlockSpec((None, 1, RWKV_PACK), lambda i: (i // nblk, 0, 0)),
                  pl.BlockSpec((tb, Q_DIM), lambda i: (i, COL_Q // Q_DIM)),
                  pl.BlockSpec((tb, 2 * KV_DIM), lambda i: (i, COL_KV // (2 * KV_DIM))),
                  pl.BlockSpec((tb, LANES), lambda i: (i, 0)),
                  pl.BlockSpec((tb, LANES), lambda i: (i, 0))]
                 + [full(a) for a in params],
        out_specs=[pl.BlockSpec((HEADS, tb, LANES), lambda i: (0, i, 0))] * 3
                  + [pl.BlockSpec((V_HI, tb, LANES), lambda i: (0, i, 0))]
                  + [pl.BlockSpec((tb, RWKV_DIM), lambda i: (i, 0))] * 2
                  + [pl.BlockSpec((tb, Q_DIM), lambda i: (i, 0)),
                     pl.BlockSpec((tb, KV_DIM), lambda i: (i, 0)),
                     pl.BlockSpec((tb, KV_DIM), lambda i: (i, 0))],
        out_shape=[per_head, per_head, per_head, per_vhi, narrow, narrow,
                   jax.ShapeDtypeStruct((m, Q_DIM), BF16),
                   jax.ShapeDtypeStruct((m, KV_DIM), F32),
                   jax.ShapeDtypeStruct((m, KV_DIM), F32)],
        compiler_params=_cparams(("parallel",)),
        name="rwkv_prep",
    )(p_all, p_all, p_all, p_all, p_all, p_all, p_all, p_all, state_rows, p_all, p_all, cos, sin, *params)


def _sublane_allsum(x):
    x = x + pltpu.roll(x, 4, 0)
    x = x + pltpu.roll(x, 2, 0)
    return x + pltpu.roll(x, 1, 0)


PACKED_ROW = (0, 4, 2, 6, 1, 5, 3, 7)


def _packed_sublane_sums(a):
    sub = lax.broadcasted_iota(jnp.int32, (SUBLANES, LANES), 0)
    top_half = sub < 4
    z = [jnp.where(top_half, a[2 * j] + pltpu.roll(a[2 * j], 4, 0), a[2 * j + 1] + pltpu.roll(a[2 * j + 1], 4, 0))
         for j in range(4)]
    first_pair = (sub % 4) < 2
    w = [jnp.where(first_pair, z[2 * j] + pltpu.roll(z[2 * j], 6, 0), z[2 * j + 1] + pltpu.roll(z[2 * j + 1], 2, 0))
         for j in range(2)]
    return jnp.where(sub % 2 == 0, w[0] + pltpu.roll(w[0], 7, 0), w[1] + pltpu.roll(w[1], 1, 0))


def _scan_kernel(ab_ref, kr_ref, cum_ref, v_ref, s0_ref, y_ref, st_ref, ops_ref, blk_ref, *, tb, nblk):
    @pl.when(pl.program_id(0) % nblk == 0)
    def _():
        st_ref[...] = s0_ref[...]

    def col_tiles(ref, t):
        rep = jnp.concatenate(
            [jnp.broadcast_to(ref[h, pl.ds(t, 1), :], (V_LO, LANES)) for h in range(HEADS)], axis=0)
        return rep.T

    def stage(t, slot):
        ab = col_tiles(ab_ref, t)
        kr = col_tiles(kr_ref, t)
        ops_ref[slot, 0] = ab[:HEAD_DIM]
        ops_ref[slot, 1] = ab[HEAD_DIM:]
        ops_ref[slot, 2] = kr[:HEAD_DIM]
        ops_ref[slot, 3] = kr[HEAD_DIM:]

    def op_tile(slot, which, kb):
        return ops_ref[slot, which, kb * SUBLANES:(kb + 1) * SUBLANES, :]

    def step(t, slot, block_end):
        ksl = lambda kb: slice(kb * SUBLANES, (kb + 1) * SUBLANES)
        acc = [None] * V_HI
        for kb in range(K_BLOCKS):
            alpha = op_tile(slot, 0, kb)
            for vh in range(V_HI):
                p = st_ref[vh, ksl(kb), :] * alpha
                acc[vh] = p if acc[vh] is None else acc[vh] + p
        u = [_sublane_allsum(acc[vh]) for vh in range(V_HI)]
        vb = [jnp.broadcast_to(v_ref[vh, pl.ds(t, 1), :], (SUBLANES, LANES)) for vh in range(V_HI)]
        yacc = [None] * V_HI
        for kb in range(K_BLOCKS):
            beta = op_tile(slot, 1, kb)
            kmod = op_tile(slot, 2, kb)
            rr = op_tile(slot, 3, kb)
            for vh in range(V_HI):
                s_new = st_ref[vh, ksl(kb), :] + beta * u[vh] + kmod * vb[vh]
                p = s_new * rr
                yacc[vh] = p if yacc[vh] is None else yacc[vh] + p
                if block_end:
                    s_new = s_new * blk_ref[ksl(kb), :]
                st_ref[vh, ksl(kb), :] = s_new
        ysum = _packed_sublane_sums(yacc)
        for vh in range(V_HI):
            y_ref[vh, pl.ds(t, 1), :] = ysum[PACKED_ROW[vh]:PACKED_ROW[vh] + 1, :]

    stage(0, 0)

    def body(i, carry):
        base = pl.multiple_of(i * SCAN_BLOCK, SCAN_BLOCK)
        following = pl.multiple_of(jnp.minimum(base + SCAN_BLOCK, tb - SCAN_BLOCK), SCAN_BLOCK)
        blk_ref[...] = col_tiles(cum_ref, base + (SCAN_BLOCK - 1))[:HEAD_DIM]
        for j in range(SCAN_BLOCK):
            nxt = base + (j + 1) if j + 1 < SCAN_BLOCK else following
            stage(nxt, (j + 1) % 2)
            step(base + j, j % 2, j + 1 == SCAN_BLOCK)
        return carry

    lax.fori_loop(0, tb // SCAN_BLOCK, body, 0)


SCAN_STATE = (V_HI, HEAD_DIM, LANES)
SCAN_SCRATCH = [pltpu.VMEM((2, 4, HEAD_DIM, LANES), F32), pltpu.VMEM((HEAD_DIM, LANES), F32)]


def _rwkv_scan(ab, kr, cum, v, s0, tb, nblk):
    steps = v.shape[1]
    nseq = s0.shape[0]
    grid = steps // tb
    assert grid == nseq * nblk and tb % SCAN_BLOCK == 0 and SCAN_BLOCK % SUBLANES == 0
    per_head = pl.BlockSpec((HEADS, tb, LANES), lambda i: (0, i, 0))
    per_vhi = pl.BlockSpec((V_HI, tb, LANES), lambda i: (0, i, 0))
    state = pl.BlockSpec((None,) + SCAN_STATE, lambda i: (i // nblk, 0, 0, 0))
    return pl.pallas_call(
        functools.partial(_scan_kernel, tb=tb, nblk=nblk),
        grid=(grid,),
        in_specs=[per_head, per_head, per_head, per_vhi, state],
        out_specs=[per_vhi, state],
        out_shape=[jax.ShapeDtypeStruct((V_HI, steps, LANES), F32),
                   jax.ShapeDtypeStruct((nseq,) + SCAN_STATE, F32)],
        scratch_shapes=SCAN_SCRATCH,
        compiler_params=_cparams(("arbitrary",)),
        name="rwkv_scan",
    )(ab, kr, cum, v, s0)


def _state_to_scan(s):
    b = s.shape[0]
    s = s.reshape(b, HEADS, V_HI, V_LO, HEAD_DIM)
    return s.transpose(0, 2, 4, 1, 3).reshape(b, V_HI, HEAD_DIM, LANES)


def _state_from_scan(st):
    b = st.shape[0]
    st = st.reshape(b, V_HI, HEAD_DIM, HEADS, V_LO)
    return st.transpose(0, 3, 1, 4, 2).reshape(b, HEADS, HEAD_DIM, HEAD_DIM)


def _rope_tile(x, cos, sin_signed):
    lane = lax.broadcasted_iota(jnp.int32, x.shape, 1)
    first_half = (lane % HEAD_DIM) < (HEAD_DIM // 2)
    rot = jnp.where(first_half, pltpu.roll(x, LANES - HEAD_DIM // 2, 1), pltpu.roll(x, HEAD_DIM // 2, 1))
    return x * cos + rot * sin_signed


def _rope_rows(q_ref, kv_ref, cos_ref, sin_ref, qo_ref, ko_ref, vo_ref):
    cos = cos_ref[...]
    sin = sin_ref[...]
    scale = HEAD_DIM ** -0.5
    assert math.log2(scale).is_integer()
    for j in range(Q_DIM // LANES):
        sl = slice(j * LANES, (j + 1) * LANES)
        qo_ref[:, sl] = (_rope_tile(q_ref[:, sl], cos, sin) * scale).astype(BF16)
    ko_ref[...] = _rope_tile(kv_ref[:, :KV_DIM], cos, sin)
    vo_ref[...] = kv_ref[:, KV_DIM:]


def _rope_tables(pos):
    half = HEAD_DIM // 2
    inv_freq = ROPE_THETA ** (-jnp.arange(half, dtype=F32) / half)
    ang = pos.astype(F32)[:, None] * inv_freq[None, :]
    cos = jnp.cos(ang)
    sin = jnp.sin(ang)
    cos = jnp.concatenate([cos, cos, cos, cos], axis=1)
    sin = jnp.concatenate([-sin, sin, -sin, sin], axis=1)
    return cos, sin


PAIRS = GROUP // 2


def _attend(q_lanes, tq, sink_ref, k_parts, v_parts, band_chunk):
    nkeys = sum(p.shape[0] for p in k_parts)
    nk_pad = pl.cdiv(nkeys + 1, LANES) * LANES
    pad = jnp.zeros((nk_pad - nkeys, LANES), F32)
    k_all = jnp.concatenate(list(k_parts) + [pad], axis=0)
    v_all = jnp.concatenate(list(v_parts) + [pad], axis=0)
    low = lax.broadcasted_iota(jnp.int32, (nk_pad, LANES), 1) < HEAD_DIM
    k_swap = pltpu.roll(k_all, HEAD_DIM, 1)
    v_swap = pltpu.roll(v_all, HEAD_DIM, 1)
    col = lax.broadcasted_iota(jnp.int32, (1, nk_pad), 1)
    visible = col < nkeys
    if band_chunk is not None:
        first_valid = N_META + jnp.maximum(2 - band_chunk, 0) * CHUNK
        visible = visible & ((col < N_META) | (col >= first_valid))
    bias = jnp.where(visible, 0.0, -jnp.inf).astype(F32)
    is_sink = col == nkeys
    ones_kv = (lax.broadcasted_iota(jnp.int32, (nk_pad, LANES), 0) <= nkeys).astype(BF16)
    rows = PAIRS * tq
    scores, values = [], []
    for kvh in range(KV_HEADS):
        k_src, k_oth = (k_all, k_swap) if kvh == 0 else (k_swap, k_all)
        v_src, v_oth = (v_all, v_swap) if kvh == 0 else (v_swap, v_all)
        halves = ((jnp.where(low, k_src, 0.0), jnp.where(low, v_src, 0.0)),
                  (jnp.where(low, 0.0, k_oth), jnp.where(low, 0.0, v_oth)))
        qp = jnp.concatenate([q_lanes(kvh * PAIRS + p) for p in range(PAIRS)], axis=0)
        for parity, (kx, vx) in enumerate(halves):
            s = lax.dot_general(qp, kx.astype(BF16), (((1,), (1,)), ((), ())), preferred_element_type=F32) + bias
            for p in range(PAIRS):
                h = kvh * GROUP + 2 * p + parity
                scores.append(jnp.where(is_sink, sink_ref[:, h:h + 1], s[p * tq:(p + 1) * tq]))
            values.append(jnp.concatenate([vx.astype(BF16), ones_kv], axis=1))
    s_all = jnp.concatenate(scores, axis=0)
    e = jnp.exp(s_all - jnp.max(s_all, axis=1, keepdims=True)).astype(BF16)
    groups = []
    for kvh in range(KV_HEADS):
        total = None
        for parity in range(2):
            idx = kvh * 2 + parity
            ov = jnp.dot(e[idx * rows:(idx + 1) * rows], values[idx], preferred_element_type=F32)
            o = ov[:, :LANES] / ov[:, LANES:]
            total = o if total is None else total + o
        groups.extend(total[p * tq:(p + 1) * tq] for p in range(PAIRS))
    return jnp.concatenate(groups, axis=1).astype(BF16)


def _attn_kernel(*refs, nseg, band_mask):
    q_ref, sink_ref = refs[0], refs[1]
    k_refs = refs[2:2 + nseg]
    v_refs = refs[2 + nseg:2 + 2 * nseg]
    o_ref = refs[2 + 2 * nseg]
    o_ref[...] = _attend(lambda j: q_ref[:, j * LANES:(j + 1) * LANES], q_ref.shape[0], sink_ref,
                         [r[...] for r in k_refs], [r[...] for r in v_refs],
                         pl.program_id(0) if band_mask else None)


def _attention(q, sinks, k_segs, v_segs, tq, grid, band_mask):
    nseg = len(k_segs)
    seg_specs = [pl.BlockSpec(bs, im) for (_, bs, im) in k_segs + v_segs]
    return pl.pallas_call(
        functools.partial(_attn_kernel, nseg=nseg, band_mask=band_mask),
        grid=(grid,),
        in_specs=[pl.BlockSpec((tq, Q_DIM), lambda i: (i, 0)),
                  pl.BlockSpec((1, HEADS), lambda i: (0, 0))] + seg_specs,
        out_specs=pl.BlockSpec((tq, Q_DIM), lambda i: (i, 0)),
        out_shape=jax.ShapeDtypeStruct((grid * tq, Q_DIM), BF16),
        compiler_params=_cparams(("parallel",)),
        name="attention",
    )(q, sinks, *[a for (a, _, _) in k_segs + v_segs])


def _mix_kernel(y_ref, bonus_ref, g_ref, yb_ref, ga_ref, gb_ref, lng_ref, lnb_ref, hsum_ref,
                wa_ref, wb_ref, o_ref, ya_ref):
    hsum = hsum_ref[...]
    mean = _dot3(sum(y_ref[vh] for vh in range(V_HI)), hsum) * (1.0 / HEAD_DIM)
    d = [y_ref[vh] - mean for vh in range(V_HI)]
    var = _dot3(sum(dv * dv for dv in d), hsum) * (1.0 / HEAD_DIM)
    inv = lax.rsqrt(var + RWKV_LN_EPS)
    for vh in range(V_HI):
        sl = slice(vh * LANES, (vh + 1) * LANES)
        yn = d[vh] * inv * lng_ref[:, sl] + lnb_ref[:, sl]
        ya_ref[:, sl] = ((yn + bonus_ref[:, sl]) * g_ref[:, sl]).astype(BF16)

    pa = jnp.dot(ya_ref[...], wa_ref[...], preferred_element_type=F32)
    pb = jnp.dot(yb_ref[...], wb_ref[...], preferred_element_type=F32)
    o_ref[...] = (jax.nn.sigmoid(ga_ref[...]) * pa + jax.nn.sigmoid(gb_ref[...]) * pb).astype(BF16)


def _mix(y, bonus, g, yb, p_all, ln_g, ln_b, hsum, wa, wb, tm):
    m = bonus.shape[0]
    row = pl.BlockSpec((tm, RWKV_DIM), lambda i: (i, 0))
    one = lambda a: pl.BlockSpec(a.shape, lambda i: (0,) * a.ndim)
    resident = lambda a: pl.BlockSpec(a.shape, lambda i: (0, 0), pipeline_mode=pl.Buffered(1))
    return pl.pallas_call(
        _mix_kernel,
        grid=(m // tm,),
        in_specs=[pl.BlockSpec((V_HI, tm, LANES), lambda i: (0, i, 0)), row, row, row,
                  pl.BlockSpec((tm, D_MODEL), lambda i: (i, COL_GA // D_MODEL)),
                  pl.BlockSpec((tm, D_MODEL), lambda i: (i, COL_GB // D_MODEL)),
                  one(ln_g), one(ln_b), one(hsum), resident(wa), resident(wb)],
        out_specs=pl.BlockSpec((tm, D_MODEL), lambda i: (i, 0)),
        out_shape=jax.ShapeDtypeStruct((m, D_MODEL), BF16),
        scratch_shapes=[pltpu.VMEM((tm, RWKV_DIM), BF16)],
        compiler_params=_cparams(("parallel",)),
        name="branch_mix",
    )(y, bonus, g, yb, p_all, p_all, ln_g, ln_b, hsum, wa, wb)


def _out_norm_kernel(mix_ref, w_ref, x_ref, g_ref, o_ref):
    z = jnp.dot(mix_ref[...], w_ref[...], preferred_element_type=F32)
    o_ref[...] = x_ref[...] + _rms(z, g_ref[...])


def _out_norm(mixed, w, x, g, tm):
    m = x.shape[0]
    return pl.pallas_call(
        _out_norm_kernel,
        grid=(m // tm,),
        in_specs=[pl.BlockSpec((tm, D_MODEL), lambda i: (i, 0)),
                  pl.BlockSpec((D_MODEL, D_MODEL), lambda i: (0, 0)),
                  pl.BlockSpec((tm, D_MODEL), lambda i: (i, 0)),
                  pl.BlockSpec((1, D_MODEL), lambda i: (0, 0))],
        out_specs=pl.BlockSpec((tm, D_MODEL), lambda i: (i, 0)),
        out_shape=jax.ShapeDtypeStruct((m, D_MODEL), F32),
        compiler_params=_cparams(("parallel",)),
        name="out_norm",
    )(mixed, w, x, g)


HALO = 16


def _gelu_tanh(x):
    return 0.5 * x * (1.0 + jnp.tanh(0.7978845608028654 * (x + 0.044715 * x * x * x)))


def _ffn_kernel(*refs, seq_len):
    if seq_len is None:
        (x_ref, halo_ref, gpre_ref, wg_ref, wv_ref, wd_ref, cw_ref, cb_ref, gpost_ref,
         o_ref, tail_ref, h_ref) = refs
    else:
        (x_ref, sa_ref, sb_ref, gpre_ref, wg_ref, wv_ref, wd_ref, cw_ref, cb_ref, gpost_ref,
         o_ref, tail_ref, h_ref) = refs
    acc_ref = o_ref
    tm = x_ref.shape[0]
    j = pl.program_id(1)
    top = HALO if seq_len is None else 0

    @pl.when(j == 0)
    def _():
        if seq_len is None:
            h_ref[:HALO] = _rms(halo_ref[...], gpre_ref[...]).astype(BF16)
        h_ref[top:] = _rms(x_ref[...], gpre_ref[...]).astype(BF16)
        acc_ref[...] = jnp.zeros_like(acc_ref)

    h_all = h_ref[...]
    gate_all = jnp.dot(h_all, wg_ref[...], preferred_element_type=F32)
    val = jnp.dot(h_all[top:], wv_ref[...], preferred_element_type=F32)
    g0 = gate_all[top:]
    g1 = pltpu.roll(gate_all, 1, 0)[top:]
    g2 = pltpu.roll(gate_all, 2, 0)[top:]
    if seq_len is not None:
        nseq = tm // seq_len
        tf = g0.shape[1]
        tau = lax.broadcasted_iota(jnp.int32, (tm, 1), 0) % seq_len
        expand = lambda s: jnp.concatenate(
            [jnp.broadcast_to(s[b:b + 1, :], (seq_len, tf)) for b in range(nseq)], axis=0)
        st_a = expand(sa_ref[...])
        st_b = expand(sb_ref[...])
        g1 = jnp.where(tau == 0, st_b, g1)
        g2 = jnp.where(tau == 0, st_a, jnp.where(tau == 1, st_b, g2))
    conv = cb_ref[...] + g2 * cw_ref[0:1, :] + g1 * cw_ref[1:2, :] + g0 * cw_ref[2:3, :]
    act = (_gelu_tanh(conv) * val).astype(BF16)
    acc_ref[...] += jnp.dot(act, wd_ref[...], preferred_element_type=F32)

    if seq_len is None:
        tail_ref[...] = g0[tm - SUBLANES:]
    else:
        for b in range(tm // seq_len):
            tail_ref[b] = g0[(b + 1) * seq_len - SUBLANES:(b + 1) * seq_len]

    @pl.when(j == pl.num_programs(1) - 1)
    def _():
        o_ref[...] = x_ref[...] + _rms(acc_ref[...], gpost_ref[...])


def _ffn(x1, conv_prev, fw, tm, tf, seq_len):
    m = x1.shape[0] - (HALO if seq_len is None else 0)
    nm, nf = m // tm, D_FF // tf
    one = lambda a: pl.BlockSpec(a.shape, lambda i, j: (0,) * a.ndim)
    wspecs = [pl.BlockSpec((D_MODEL, tf), lambda i, j: (0, j)),
              pl.BlockSpec((D_MODEL, tf), lambda i, j: (0, D_FF // tf + j)),
              pl.BlockSpec((tf, D_MODEL), lambda i, j: (j, 0)),
              pl.BlockSpec((FFN_CONV, tf), lambda i, j: (0, j)),
              pl.BlockSpec((1, tf), lambda i, j: (0, j)),
              one(fw["g_post"])]
    wargs = (fw["w_up"], fw["w_up"], fw["w_down"], fw["conv_w"], fw["conv_b"], fw["g_post"])
    if seq_len is None:
        assert tm % HALO == 0
        pre_specs = [pl.BlockSpec((pl.Element(tm), pl.Element(D_MODEL)),
                                  lambda i, j: (pl.multiple_of(HALO + i * tm, HALO), 0)),
                     pl.BlockSpec((HALO, D_MODEL), lambda i, j: (i * (tm // HALO), 0)),
                     one(fw["g_pre"])]
        pre_args = (x1, x1, fw["g_pre"])
        tail_shape = (nm, SUBLANES, D_FF)
        tail_spec = pl.BlockSpec((None, SUBLANES, tf), lambda i, j: (i, 0, j))
        hrows = tm + HALO
    else:
        nseq = tm // seq_len
        assert nm == 1
        pre_specs = [pl.BlockSpec((tm, D_MODEL), lambda i, j: (i, 0)),
                     pl.BlockSpec((nseq, tf), lambda i, j: (0, j)),
                     pl.BlockSpec((nseq, tf), lambda i, j: (0, j)),
                     one(fw["g_pre"])]
        pre_args = (x1, conv_prev[:, 0, :], conv_prev[:, 1, :], fw["g_pre"])
        tail_shape = (nseq, SUBLANES, D_FF)
        tail_spec = pl.BlockSpec((nseq, SUBLANES, tf), lambda i, j: (0, 0, j))
        hrows = tm
    return pl.pallas_call(
        functools.partial(_ffn_kernel, seq_len=seq_len),
        grid=(nm, nf),
        in_specs=pre_specs + wspecs,
        out_specs=[pl.BlockSpec((tm, D_MODEL), lambda i, j: (i, 0), pipeline_mode=pl.Buffered(1)), tail_spec],
        out_shape=[jax.ShapeDtypeStruct((m, D_MODEL), F32),
                   jax.ShapeDtypeStruct(tail_shape, F32)],
        scratch_shapes=[pltpu.VMEM((hrows, D_MODEL), BF16)],
        compiler_params=_cparams(("parallel", "arbitrary")),
        name="conv_ffn",
    )(*pre_args, *wargs)


def _pack_lora(x, axis=1):
    cut = lambda a, b: lax.slice_in_dim(x, a, b, axis=axis)

    def z(n):
        shape = list(x.shape)
        shape[axis] = n
        return jnp.zeros(shape, x.dtype)

    xw = cut(0, DECAY_LORA)
    xa = cut(DECAY_LORA, DECAY_LORA + ICLR_LORA)
    xg = cut(DECAY_LORA + ICLR_LORA, DECAY_LORA + ICLR_LORA + GATE_LORA)
    return jnp.concatenate([xw, z(LORA_XA - DECAY_LORA), xa, z(LORA_XG - LORA_XA - ICLR_LORA),
                            xg, z(LORA_W - LORA_XG - GATE_LORA)], axis=axis)


def _unpack_lora(x):
    return jnp.concatenate([x[:, LORA_XW:LORA_XW + DECAY_LORA], x[:, LORA_XA:LORA_XA + ICLR_LORA],
                            x[:, LORA_XG:LORA_XG + GATE_LORA]], axis=1)


def _pack_in_proj_t(wt):
    c = 3 * RWKV_DIM
    q0 = RWKV_PROJ
    kv0 = q0 + Q_DIM
    g0 = kv0 + 2 * KV_DIM
    pieces = [wt[:2 * RWKV_DIM], _to_scan_order(wt[2 * RWKV_DIM:c], 0), wt[q0:kv0],
              wt[g0:g0 + 2 * D_MODEL], _pack_lora(wt[c:RWKV_PROJ], axis=0), wt[kv0:g0]]
    return jnp.concatenate([p.astype(BF16) for p in pieces], axis=0)


def _pack_rwkv_row(x):
    c = 3 * RWKV_DIM
    return jnp.concatenate([x[:, :2 * RWKV_DIM], _to_scan_order(x[:, 2 * RWKV_DIM:c], 1), _pack_lora(x[:, c:])],
                           axis=1)


def _unpack_rwkv_row(p_row):
    return jnp.concatenate([p_row[:, :2 * RWKV_DIM], _to_natural_order(p_row[:, COL_V:COL_V + RWKV_DIM], 1),
                            _unpack_lora(p_row[:, COL_LORA:COL_LORA + LORA_W])], axis=1)


def _pick_tile(m, candidates):
    for c in candidates:
        if m % c == 0:
            return c
    raise ValueError(f"no row tile in {candidates} divides {m}")


def _pad_rows(w, rows):
    return jnp.concatenate([w, jnp.zeros((rows - w.shape[0], w.shape[1]), w.dtype)], axis=0)


def _trunk(x, lw, *, tm, tm_mix, tb_prep, shift_rows, nseq, cos, sin, scan_attend, ffn_tm,
           ffn_seq_len, conv_prev):
    seq_rows = x.shape[0] // nseq
    p_all = _norm_matmul(x, lw["g_pre_mix"], lw["w_in_t"], tm, 1792)
    ab, kr, cum, v_scan, g, bonus, q_roped, k_roped, v_att = _rwkv_prep(
        p_all, shift_rows, lw, cos, sin, tb_prep, seq_rows // tb_prep)
    y, s_fin, att = scan_attend((ab, kr, cum, v_scan), q_roped, k_roped, v_att)

    mixed = _mix(y, bonus, g, att, p_all, lw["ln_g"], lw["ln_b"], lw["hsum_scan"],
                 lw["w_branch_a"], lw["w_branch_b"], tm_mix)
    x1 = _out_norm(mixed, lw["w_out"], x, lw["g_post_mix"], tm_mix)
    x2, tail = _ffn(x1, conv_prev, lw["ffn"], ffn_tm, 512, ffn_seq_len)
    return x2, p_all, k_roped, v_att, s_fin, tail


def kernel(x_prompt, x_sample, cache_meta_k, cache_meta_v, cache_win_k, cache_win_v, state_rwkv, state_shift, state_ffn_conv, meta_tokens, g_pre_mix, w_in, rwkv_mu, rwkv_w0, rwkv_w_lora_b, rwkv_a0, rwkv_a_lora_b, rwkv_g_lora_b, rwkv_k_k, rwkv_k_a, rwkv_r_k, rwkv_ln_g, rwkv_ln_b, attn_sinks, w_branch_a, w_branch_b, w_out, g_post_mix, g_pre_ffn, w_up, ffn_conv_w, ffn_conv_b, w_down, g_post_ffn):
    depth = w_in.shape[0]
    assert depth == 1
    l = 0
    seq = x_prompt.shape[1]
    nb, tn = x_sample.shape[0], x_sample.shape[1]
    row = lambda a: a.reshape(1, -1)
    lane = np.arange(LANES)
    chan = np.arange(RWKV_DIM)
    ones_bd = jnp.asarray(lane[:, None] // HEAD_DIM == lane[None, :] // HEAD_DIM, BF16)
    hsum_nat = jnp.asarray((chan // HEAD_DIM)[:, None] == (lane // V_LO)[None, :], BF16)
    hsum_scan = jnp.asarray((lane // V_LO)[:, None] == (lane // V_LO)[None, :], BF16)
    lw = dict(
        g_pre_mix=row(g_pre_mix[l]),
        w_in_t=_pack_in_proj_t(jnp.swapaxes(w_in[l], 0, 1)),
        mu=_pack_rwkv_row(row(rwkv_mu[l])),
        w0=row(rwkv_w0[l]), a0=row(rwkv_a0[l]), k_k=row(rwkv_k_k[l]), k_a=row(rwkv_k_a[l]),
        r_k=row(rwkv_r_k[l]),
        w_lora=_pad_rows(rwkv_w_lora_b[l], LANES).astype(BF16),
        a_lora=_pad_rows(rwkv_a_lora_b[l], LANES).astype(BF16),
        g_lora=_pad_rows(_to_scan_order(rwkv_g_lora_b[l], 1), 2 * LANES).astype(BF16),
        ones_bd=ones_bd, hsum_nat=hsum_nat, hsum_scan=hsum_scan,
        ln_g=row(_to_scan_order(rwkv_ln_g[l], 0)), ln_b=row(_to_scan_order(rwkv_ln_b[l], 0)),
        w_branch_a=_to_scan_order(w_branch_a[l], 0).astype(BF16), w_branch_b=w_branch_b[l].astype(BF16),
        w_out=w_out[l].astype(BF16), g_post_mix=row(g_post_mix[l]),
        ffn=dict(g_pre=row(g_pre_ffn[l]), w_up=w_up[l].astype(BF16), w_down=w_down[l].astype(BF16),
                 conv_w=ffn_conv_w[l], conv_b=row(ffn_conv_b[l]), g_post=row(g_post_ffn[l])),
    )
    sinks = row(attn_sinks[l])

    mp = N_META + seq
    xp = jnp.concatenate([meta_tokens.astype(x_prompt.dtype), x_prompt[0]], axis=0)
    nchunk = seq // CHUNK
    tm_p = _pick_tile(mp, (912, 144, 48))
    tm_mix_p = _pick_tile(mp, (432, 144, 48))
    tb_prep_p = _pick_tile(mp, (144, 48))
    cos_p, sin_p = _rope_tables(jnp.arange(mp, dtype=jnp.int32))

    def scan_attend_prompt(scan_ops, q, k, v):
        scan_tb = _pick_tile(mp, (432, 144, 48))
        y, s_fin = _rwkv_scan(*scan_ops, jnp.zeros((1,) + SCAN_STATE, F32), scan_tb, mp // scan_tb)
        mk, mv = k[:N_META], v[:N_META]
        whole = lambda a: (a, (N_META, KV_DIM), lambda c: (0, 0))
        o_m = _attention(q[:N_META], sinks, [whole(mk)], [whole(mv)], N_META, 1, False)
        kf, vf = k[N_META:], v[N_META:]
        band = lambda a: [(a, (CHUNK, KV_DIM), lambda c, d=d: (jnp.maximum(c - d, 0), 0)) for d in (2, 1, 0)]
        o_f = _attention(q[N_META:], sinks, [whole(mk)] + band(kf), [whole(mv)] + band(vf),
                         CHUNK, nchunk, True)
        return y, s_fin, jnp.concatenate([o_m, o_f], axis=0)

    assert N_META == HALO
    yp, p_p, k_p, v_p, sfin_p, tail_p = _trunk(
        xp, lw, tm=tm_p, tm_mix=tm_mix_p, tb_prep=tb_prep_p, shift_rows=jnp.zeros((1, 1, RWKV_PACK), F32),
        nseq=1, cos=cos_p, sin=sin_p, scan_attend=scan_attend_prompt,
        ffn_tm=_pick_tile(seq, (1024, 64)), ffn_seq_len=None, conv_prev=None)

    ms = nb * tn
    xs = x_sample.reshape(ms, D_MODEL)
    pos_s = N_META + PAST_LEN + jnp.arange(tn, dtype=jnp.int32)
    cos_s, sin_s = _rope_tables(jnp.tile(pos_s, nb))
    nwin = cache_win_k.shape[2]
    cmk = cache_meta_k[l].reshape(nb, N_META, KV_DIM)
    cmv = cache_meta_v[l].reshape(nb, N_META, KV_DIM)
    cwk = cache_win_k[l].reshape(nb, nwin, KV_DIM)
    cwv = cache_win_v[l].reshape(nb, nwin, KV_DIM)

    shift_s = _pack_rwkv_row(state_shift[l].reshape(nb, RWKV_PROJ))[:, None, :]

    def scan_attend_sample(scan_ops, q, k, v):
        y, s_fin = _rwkv_scan(*scan_ops, _state_to_scan(state_rwkv[l]), tn, 1)
        per_b = lambda a, n: (a, (None, n, KV_DIM), lambda b: (b, 0, 0))
        own = lambda a: (a, (tn, KV_DIM), lambda b: (b, 0))
        att = _attention(q, sinks, [per_b(cmk, N_META), per_b(cwk, nwin), own(k)],
                         [per_b(cmv, N_META), per_b(cwv, nwin), own(v)], tn, nb, False)
        return y, s_fin, att

    xs2, p_s, k_s, v_s, sfin_s, tail_s = _trunk(
        xs, lw, tm=ms, tm_mix=ms, tb_prep=tn, shift_rows=shift_s, nseq=nb, cos=cos_s, sin=sin_s,
        scan_attend=scan_attend_sample, ffn_tm=ms, ffn_seq_len=tn, conv_prev=state_ffn_conv[l])

    kv5 = lambda a, n: a.reshape(1, -1, n, KV_HEADS, HEAD_DIM)
    y_prompt = yp[None]
    y_sample = xs2.reshape(nb, tn, D_MODEL)
    prompt_meta_k = kv5(k_p[:N_META], N_META)
    prompt_meta_v = kv5(v_p[:N_META], N_META)
    prompt_win_k = kv5(k_p[mp - WINDOW:], WINDOW)
    prompt_win_v = kv5(v_p[mp - WINDOW:], WINDOW)
    prompt_rwkv = _state_from_scan(sfin_p)[None]
    prompt_shift = _unpack_rwkv_row(p_p[mp - 1:mp]).reshape(1, 1, 1, RWKV_PROJ)
    prompt_ffn_conv = tail_p[-1, SUBLANES - (FFN_CONV - 1):].reshape(1, 1, FFN_CONV - 1, D_FF)
    sample_win_k = kv5(k_s, tn)
    sample_win_v = kv5(v_s, tn)
    sample_rwkv = _state_from_scan(sfin_s)[None]
    sample_shift = _unpack_rwkv_row(p_s[tn - 1::tn]).reshape(1, nb, 1, RWKV_PROJ)
    sample_ffn_conv = tail_s[:, SUBLANES - (FFN_CONV - 1):].reshape(1, nb, FFN_CONV - 1, D_FF)
    return (y_prompt, y_sample, prompt_meta_k, prompt_meta_v, prompt_win_k, prompt_win_v,
            prompt_rwkv, prompt_shift, prompt_ffn_conv, sample_win_k, sample_win_v,
            sample_rwkv, sample_shift, sample_ffn_conv)
```

```python
import functools
import math

import numpy as np
import jax
import jax.numpy as jnp
from jax import lax
from jax.experimental import pallas as pl
from jax.experimental.pallas import tpu as pltpu

F32 = jnp.float32
BF16 = jnp.bfloat16

D_MODEL = 2048
CHUNK = 64
N_META = 16
NORM_EPS = 1e-6
HEADS = 16
HEAD_DIM = 64
RWKV_DIM = HEADS * HEAD_DIM
DECAY_LORA = 64
ICLR_LORA = 64
GATE_LORA = 160
RWKV_PROJ = 3 * RWKV_DIM + DECAY_LORA + ICLR_LORA + GATE_LORA
RWKV_LN_EPS = 64e-5
KV_HEADS = 2
GROUP = HEADS // KV_HEADS
Q_DIM = HEADS * HEAD_DIM
KV_DIM = KV_HEADS * HEAD_DIM
WINDOW = 128
ROPE_THETA = 10000.0
D_FF = 3 * D_MODEL
FFN_CONV = 3
PAST_LEN = 4096

SUBLANES = 8
LANES = 128
V_LO = LANES // HEADS
V_HI = HEAD_DIM // V_LO
K_BLOCKS = HEAD_DIM // SUBLANES
SCAN_BLOCK = SUBLANES

COL_R, COL_K, COL_V, COL_Q = 0, 1024, 2048, 3072
COL_GA, COL_GB = 4096, 6144
COL_LORA = 8192
LORA_W = 512
LORA_XW, LORA_XA, LORA_XG = 0, 128, 256
COL_KV = 8704
N_PACK = 8960
RWKV_PACK = 3 * RWKV_DIM + LORA_W

VMEM_LIMIT = 56 * 1024 * 1024


def _swap_channel_order(x, axis, outer, inner):
    axis = axis % x.ndim
    shape = x.shape
    x = x.reshape(shape[:axis] + (outer, inner, V_LO) + shape[axis + 1:])
    return jnp.swapaxes(x, axis, axis + 1).reshape(shape)


def _to_scan_order(x, axis):
    return _swap_channel_order(x, axis, HEADS, V_HI)


def _to_natural_order(x, axis):
    return _swap_channel_order(x, axis, V_HI, HEADS)


def _cparams(sem):
    return pltpu.CompilerParams(dimension_semantics=sem, vmem_limit_bytes=VMEM_LIMIT)


def _rms(x, g):
    return x * lax.rsqrt(jnp.mean(x * x, axis=-1, keepdims=True) + NORM_EPS) * g


def _split3(x):
    hi = x.astype(BF16)
    r1 = x - hi.astype(F32)
    mid = r1.astype(BF16)
    lo = (r1 - mid.astype(F32)).astype(BF16)
    return hi, mid, lo


def _dot3(x, mat):
    hi, mid, lo = _split3(x)
    acc = jnp.dot(hi, mat, preferred_element_type=F32)
    acc = acc + jnp.dot(mid, mat, preferred_element_type=F32)
    return acc + jnp.dot(lo, mat, preferred_element_type=F32)


def _seg_sum(x, ones_bd):
    hi, mid, lo = _split3(x)
    outs = []
    for j in range(x.shape[1] // LANES):
        sl = slice(j * LANES, (j + 1) * LANES)
        acc = jnp.dot(hi[:, sl], ones_bd, preferred_element_type=F32)
        acc = acc + jnp.dot(mid[:, sl], ones_bd, preferred_element_type=F32)
        acc = acc + jnp.dot(lo[:, sl], ones_bd, preferred_element_type=F32)
        outs.append(acc)
    return jnp.concatenate(outs, axis=1)


def _tile_lanes(x, n):
    return jnp.concatenate([x] * n, axis=1)


def _prefixed_rows_spec(tm, width, lead):
    def index(i, *_):
        return pl.multiple_of(jnp.maximum(i * tm - lead, 0), SUBLANES), 0
    return pl.BlockSpec((pl.Element(tm), pl.Element(width)), index)


def _for_prefixed_tile(x_ref, prefix_ref, use):
    if prefix_ref is None:
        use(x_ref[...])
        return
    lead = prefix_ref.shape[0]

    @pl.when(pl.program_id(0) == 0)
    def _():
        use(jnp.concatenate([prefix_ref[...], x_ref[:x_ref.shape[0] - lead, :]], axis=0))

    @pl.when(pl.program_id(0) != 0)
    def _():
        use(x_ref[...])


def _norm_matmul_kernel(*refs, prefixed):
    if prefixed:
        x_ref, prefix_ref, g_ref, w_ref, o_ref, h_ref = refs
    else:
        (x_ref, g_ref, w_ref, o_ref, h_ref), prefix_ref = refs, None

    @pl.when(pl.program_id(1) == 0)
    def _():
        def normalise(x):
            h_ref[...] = _rms(x, g_ref[...]).astype(BF16)

        _for_prefixed_tile(x_ref, prefix_ref, normalise)

    o_ref[...] = lax.dot_general(h_ref[...], w_ref[...], (((1,), (1,)), ((), ())),
                                 preferred_element_type=F32)


def _norm_matmul(x, prefix, g, w_t, tm, tn):
    d = x.shape[1]
    lead = 0 if prefix is None else prefix.shape[0]
    m = x.shape[0] + lead
    n = w_t.shape[0]
    if prefix is None:
        x_specs, x_args = [pl.BlockSpec((tm, d), lambda i, j: (i, 0))], (x,)
    else:
        assert lead % SUBLANES == 0 and tm % SUBLANES == 0
        x_specs = [_prefixed_rows_spec(tm, d, lead), pl.BlockSpec((lead, d), lambda i, j: (0, 0))]
        x_args = (x, prefix)
    return pl.pallas_call(
        functools.partial(_norm_matmul_kernel, prefixed=prefix is not None),
        grid=(m // tm, n // tn),
        in_specs=x_specs + [pl.BlockSpec((1, d), lambda i, j: (0, 0)),
                            pl.BlockSpec((tn, d), lambda i, j: (j, 0))],
        out_specs=pl.BlockSpec((tm, tn), lambda i, j: (i, j)),
        out_shape=jax.ShapeDtypeStruct((m, n), F32),
        scratch_shapes=[pltpu.VMEM((tm, d), BF16)],
        compiler_params=_cparams(("parallel", "arbitrary")),
        name="norm_matmul",
    )(*x_args, g, w_t)


def _store_head_pairs(ref, a, b):
    lane = lax.broadcasted_iota(jnp.int32, (a.shape[0], LANES), 1)
    low = lane < HEAD_DIM
    for j in range(a.shape[1] // LANES):
        sl = slice(j * LANES, (j + 1) * LANES)
        aj, bj = a[:, sl], b[:, sl]
        ref[2 * j] = jnp.where(low, aj, pltpu.roll(bj, HEAD_DIM, 1))
        ref[2 * j + 1] = jnp.where(low, pltpu.roll(aj, HEAD_DIM, 1), bj)


def _rwkv_prep_kernel(r_ref, k_ref, v_ref, l_ref, hr_ref, hk_ref, hv_ref, hl_ref, state_ref,
                      q_ref, kv_ref, cos_ref, sin_ref,
                      mu_ref, w0_ref, a0_ref, kk_ref, ka_ref,
                      rk_ref, ww_ref, wa_ref, wg_ref, ones_ref, hsum_ref,
                      ab_ref, kr_ref, cum_ref, vo_ref, g_ref, bonus_ref, aq_ref, ak_ref, av_ref, *, nblk):
    rows = r_ref.shape[0]
    first = lax.broadcasted_iota(jnp.int32, (rows, 1), 0) == 0
    seq_start = pl.program_id(0) % nblk == 0

    def shifted(x, halo_ref, lo, hi):
        before = jnp.where(seq_start, state_ref[:, lo:hi], halo_ref[SUBLANES - 1:SUBLANES, :])
        prev = jnp.where(first, before, pltpu.roll(x, 1, 0))
        return x + (prev - x) * mu_ref[:, lo:hi]

    r = shifted(r_ref[...], hr_ref, 0, RWKV_DIM)
    k = shifted(k_ref[...], hk_ref, RWKV_DIM, 2 * RWKV_DIM)
    v = shifted(v_ref[...], hv_ref, 2 * RWKV_DIM, 3 * RWKV_DIM)
    lora = shifted(l_ref[...], hl_ref, 3 * RWKV_DIM, RWKV_PACK)
    xw = lora[:, LORA_XW:LORA_XW + LANES]
    xa = lora[:, LORA_XA:LORA_XA + LANES]
    xg = lora[:, LORA_XG:LORA_XG + 2 * LANES]

    w_pre = w0_ref[...] + jnp.dot(jnp.tanh(xw).astype(BF16), ww_ref[...], preferred_element_type=F32)
    log_decay = -math.exp(-0.5) * jax.nn.sigmoid(w_pre)
    step_in_block = lax.broadcasted_iota(jnp.int32, (rows, 1), 0) % SCAN_BLOCK
    log_cum = log_decay
    shift = 1
    while shift < SCAN_BLOCK:
        log_cum = log_cum + jnp.where(step_in_block >= shift, pltpu.roll(log_cum, shift, 0), 0.0)
        shift *= 2
    cum = jnp.exp(log_cum)
    cum_before = jnp.exp(log_cum - log_decay)
    inv_cum = jnp.exp(-log_cum)
    a = jax.nn.sigmoid(a0_ref[...] + jnp.dot(xa.astype(BF16), wa_ref[...], preferred_element_type=F32))
    g = jnp.dot(jax.nn.sigmoid(xg).astype(BF16), wg_ref[...], preferred_element_type=F32)

    kk = k * kk_ref[...]
    kk = kk * lax.rsqrt(jnp.maximum(_seg_sum(kk * kk, ones_ref[...]), 1e-24))
    k_mod = k * (1.0 + (a - 1.0) * ka_ref[...])
    rk_sum = _dot3(r * k_mod * rk_ref[...], hsum_ref[...])

    _store_head_pairs(ab_ref, -kk * cum_before, kk * a * inv_cum)
    _store_head_pairs(kr_ref, k_mod * inv_cum, r * cum)
    _store_head_pairs(cum_ref, cum, cum)
    for vh in range(V_HI):
        vo_ref[vh] = v[:, vh * LANES:(vh + 1) * LANES]
    g_ref[...] = g
    bonus_ref[...] = _tile_lanes(rk_sum, V_HI) * v
    _rope_rows(q_ref, kv_ref, cos_ref, sin_ref, aq_ref, ak_ref, av_ref)


def _rwkv_prep(p_all, state_rows, pw, cos, sin, tb, nblk):
    m = p_all.shape[0]
    nt = m // tb
    assert tb % SUBLANES == 0 and nt == state_rows.shape[0] * nblk
    row = lambda c: pl.BlockSpec((tb, RWKV_DIM), lambda i, c=c: (i, c))
    before = lambda i: jnp.maximum(i * (tb // SUBLANES) - 1, 0)
    halo = lambda c: pl.BlockSpec((SUBLANES, RWKV_DIM), lambda i, c=c: (before(i), c))
    full = lambda a: pl.BlockSpec(a.shape, lambda i: (0,) * a.ndim)
    params = (pw["mu"], pw["w0"], pw["a0"], pw["k_k"], pw["k_a"], pw["r_k"],
              pw["w_lora"], pw["a_lora"], pw["g_lora"], pw["ones_bd"], pw["hsum_nat"])
    per_head = jax.ShapeDtypeStruct((HEADS, m, LANES), F32)
    per_vhi = jax.ShapeDtypeStruct((V_HI, m, LANES), F32)
    narrow = jax.ShapeDtypeStruct((m, RWKV_DIM), F32)
    return pl.pallas_call(
        functools.partial(_rwkv_prep_kernel, nblk=nblk),
        grid=(nt,),
        in_specs=[row(COL_R // RWKV_DIM), row(COL_K // RWKV_DIM), row(COL_V // RWKV_DIM),
                  pl.BlockSpec((tb, LORA_W), lambda i: (i, COL_LORA // LORA_W)),
                  halo(COL_R // RWKV_DIM), halo(COL_K // RWKV_DIM), halo(COL_V // RWKV_DIM),
                  pl.BlockSpec((SUBLANES, LORA_W), lambda i: (before(i), COL_LORA // LORA_W)),
                  pl.BlockSpec((None, 1, RWKV_PACK), lambda i: (i // nblk, 0, 0)),
                  pl.BlockSpec((tb, Q_DIM), lambda i: (i, COL_Q // Q_DIM)),
                  pl.BlockSpec((tb, 2 * KV_DIM), lambda i: (i, COL_KV // (2 * KV_DIM))),
                  pl.BlockSpec((tb, LANES), lambda i: (i, 0)),
                  pl.BlockSpec((tb, LANES), lambda i: (i, 0))]
                 + [full(a) for a in params],
        out_specs=[pl.BlockSpec((HEADS, tb, LANES), lambda i: (0, i, 0))] * 3
                  + [pl.BlockSpec((V_HI, tb, LANES), lambda i: (0, i, 0))]
                  + [pl.BlockSpec((tb, RWKV_DIM), lambda i: (i, 0))] * 2
                  + [pl.BlockSpec((tb, Q_DIM), lambda i: (i, 0)),
                     pl.BlockSpec((tb, KV_DIM), lambda i: (i, 0)),
                     pl.BlockSpec((tb, KV_DIM), lambda i: (i, 0))],
        out_shape=[per_head, per_head, per_head, per_vhi, narrow, narrow,
                   jax.ShapeDtypeStruct((m, Q_DIM), BF16),
                   jax.ShapeDtypeStruct((m, KV_DIM), F32),
                   jax.ShapeDtypeStruct((m, KV_DIM), F32)],
        compiler_params=_cparams(("parallel",)),
        name="rwkv_prep",
    )(p_all, p_all, p_all, p_all, p_all, p_all, p_all, p_all, state_rows, p_all, p_all, cos, sin, *params)


def _sublane_allsum(x):
    x = x + pltpu.roll(x, 4, 0)
    x = x + pltpu.roll(x, 2, 0)
    return x + pltpu.roll(x, 1, 0)


PACKED_ROW = (0, 4, 2, 6, 1, 5, 3, 7)


def _packed_sublane_sums(a):
    sub = lax.broadcasted_iota(jnp.int32, (SUBLANES, LANES), 0)
    top_half = sub < 4
    z = [jnp.where(top_half, a[2 * j] + pltpu.roll(a[2 * j], 4, 0), a[2 * j + 1] + pltpu.roll(a[2 * j + 1], 4, 0))
         for j in range(4)]
    first_pair = (sub % 4) < 2
    w = [jnp.where(first_pair, z[2 * j] + pltpu.roll(z[2 * j], 6, 0), z[2 * j + 1] + pltpu.roll(z[2 * j + 1], 2, 0))
         for j in range(2)]
    return jnp.where(sub % 2 == 0, w[0] + pltpu.roll(w[0], 7, 0), w[1] + pltpu.roll(w[1], 1, 0))


def _scan_kernel(ab_ref, kr_ref, cum_ref, v_ref, s0_ref, y_ref, st_ref, ops_ref, blk_ref, *, tb, nblk):
    @pl.when(pl.program_id(0) % nblk == 0)
    def _():
        st_ref[...] = s0_ref[...]

    def col_tiles(ref, t):
        rep = jnp.concatenate(
            [jnp.broadcast_to(ref[h, pl.ds(t, 1), :], (V_LO, LANES)) for h in range(HEADS)], axis=0)
        return rep.T

    def stage(t, slot):
        ab = col_tiles(ab_ref, t)
        kr = col_tiles(kr_ref, t)
        ops_ref[slot, 0] = ab[:HEAD_DIM]
        ops_ref[slot, 1] = ab[HEAD_DIM:]
        ops_ref[slot, 2] = kr[:HEAD_DIM]
        ops_ref[slot, 3] = kr[HEAD_DIM:]

    def op_tile(slot, which, kb):
        return ops_ref[slot, which, kb * SUBLANES:(kb + 1) * SUBLANES, :]

    def step(t, slot, block_end):
        ksl = lambda kb: slice(kb * SUBLANES, (kb + 1) * SUBLANES)
        acc = [None] * V_HI
        for kb in range(K_BLOCKS):
            alpha = op_tile(slot, 0, kb)
            for vh in range(V_HI):
                p = st_ref[vh, ksl(kb), :] * alpha
                acc[vh] = p if acc[vh] is None else acc[vh] + p
        u = [_sublane_allsum(acc[vh]) for vh in range(V_HI)]
        vb = [jnp.broadcast_to(v_ref[vh, pl.ds(t, 1), :], (SUBLANES, LANES)) for vh in range(V_HI)]
        yacc = [None] * V_HI
        for kb in range(K_BLOCKS):
            beta = op_tile(slot, 1, kb)
            kmod = op_tile(slot, 2, kb)
            rr = op_tile(slot, 3, kb)
            for vh in range(V_HI):
                s_new = st_ref[vh, ksl(kb), :] + beta * u[vh] + kmod * vb[vh]
                p = s_new * rr
                yacc[vh] = p if yacc[vh] is None else yacc[vh] + p
                if block_end:
                    s_new = s_new * blk_ref[ksl(kb), :]
                st_ref[vh, ksl(kb), :] = s_new
        ysum = _packed_sublane_sums(yacc)
        for vh in range(V_HI):
            y_ref[vh, pl.ds(t, 1), :] = ysum[PACKED_ROW[vh]:PACKED_ROW[vh] + 1, :]

    stage(0, 0)

    def body(i, carry):
        base = pl.multiple_of(i * SCAN_BLOCK, SCAN_BLOCK)
        following = pl.multiple_of(jnp.minimum(base + SCAN_BLOCK, tb - SCAN_BLOCK), SCAN_BLOCK)
        blk_ref[...] = col_tiles(cum_ref, base + (SCAN_BLOCK - 1))[:HEAD_DIM]
        for j in range(SCAN_BLOCK):
            nxt = base + (j + 1) if j + 1 < SCAN_BLOCK else following
            stage(nxt, (j + 1) % 2)
            step(base + j, j % 2, j + 1 == SCAN_BLOCK)
        return carry

    lax.fori_loop(0, tb // SCAN_BLOCK, body, 0)


SCAN_STATE = (V_HI, HEAD_DIM, LANES)
SCAN_SCRATCH = [pltpu.VMEM((2, 4, HEAD_DIM, LANES), F32), pltpu.VMEM((HEAD_DIM, LANES), F32)]


def _rwkv_scan(ab, kr, cum, v, s0, tb, nblk):
    steps = v.shape[1]
    nseq = s0.shape[0]
    grid = steps // tb
    assert grid == nseq * nblk and tb % SCAN_BLOCK == 0 and SCAN_BLOCK % SUBLANES == 0
    per_head = pl.BlockSpec((HEADS, tb, LANES), lambda i: (0, i, 0))
    per_vhi = pl.BlockSpec((V_HI, tb, LANES), lambda i: (0, i, 0))
    state = pl.BlockSpec((None,) + SCAN_STATE, lambda i: (i // nblk, 0, 0, 0))
    return pl.pallas_call(
        functools.partial(_scan_kernel, tb=tb, nblk=nblk),
        grid=(grid,),
        in_specs=[per_head, per_head, per_head, per_vhi, state],
        out_specs=[per_vhi, state],
        out_shape=[jax.ShapeDtypeStruct((V_HI, steps, LANES), F32),
                   jax.ShapeDtypeStruct((nseq,) + SCAN_STATE, F32)],
        scratch_shapes=SCAN_SCRATCH,
        compiler_params=_cparams(("arbitrary",)),
        name="rwkv_scan",
    )(ab, kr, cum, v, s0)


def _state_to_scan(s):
    b = s.shape[0]
    s = s.reshape(b, HEADS, V_HI, V_LO, HEAD_DIM)
    return s.transpose(0, 2, 4, 1, 3).reshape(b, V_HI, HEAD_DIM, LANES)


def _state_from_scan(st):
    b = st.shape[0]
    st = st.reshape(b, V_HI, HEAD_DIM, HEADS, V_LO)
    return st.transpose(0, 3, 1, 4, 2).reshape(b, HEADS, HEAD_DIM, HEAD_DIM)


def _rope_tile(x, cos, sin_signed):
    lane = lax.broadcasted_iota(jnp.int32, x.shape, 1)
    first_half = (lane % HEAD_DIM) < (HEAD_DIM // 2)
    rot = jnp.where(first_half, pltpu.roll(x, LANES - HEAD_DIM // 2, 1), pltpu.roll(x, HEAD_DIM // 2, 1))
    return x * cos + rot * sin_signed


def _rope_rows(q_ref, kv_ref, cos_ref, sin_ref, qo_ref, ko_ref, vo_ref):
    cos = cos_ref[...]
    sin = sin_ref[...]
    scale = HEAD_DIM ** -0.5
    assert math.log2(scale).is_integer()
    for j in range(Q_DIM // LANES):
        sl = slice(j * LANES, (j + 1) * LANES)
        qo_ref[:, sl] = (_rope_tile(q_ref[:, sl], cos, sin) * scale).astype(BF16)
    ko_ref[...] = _rope_tile(kv_ref[:, :KV_DIM], cos, sin)
    vo_ref[...] = kv_ref[:, KV_DIM:]


def _rope_tables(pos):
    half = HEAD_DIM // 2
    inv_freq = ROPE_THETA ** (-jnp.arange(half, dtype=F32) / half)
    ang = pos.astype(F32)[:, None] * inv_freq[None, :]
    cos = jnp.cos(ang)
    sin = jnp.sin(ang)
    cos = jnp.concatenate([cos, cos, cos, cos], axis=1)
    sin = jnp.concatenate([-sin, sin, -sin, sin], axis=1)
    return cos, sin


PAIRS = GROUP // 2


def _attend(q_lanes, tq, sink_ref, k_parts, v_parts, band_chunk):
    nkeys = sum(p.shape[0] for p in k_parts)
    nk_pad = pl.cdiv(nkeys + 1, LANES) * LANES
    pad = jnp.zeros((nk_pad - nkeys, LANES), F32)
    k_all = jnp.concatenate(list(k_parts) + [pad], axis=0)
    v_all = jnp.concatenate(list(v_parts) + [pad], axis=0)
    low = lax.broadcasted_iota(jnp.int32, (nk_pad, LANES), 1) < HEAD_DIM
    k_swap = pltpu.roll(k_all, HEAD_DIM, 1)
    v_swap = pltpu.roll(v_all, HEAD_DIM, 1)
    col = lax.broadcasted_iota(jnp.int32, (1, nk_pad), 1)
    visible = col < nkeys
    if band_chunk is not None:
        first_valid = N_META + jnp.maximum(2 - band_chunk, 0) * CHUNK
        visible = visible & ((col < N_META) | (col >= first_valid))
    bias = jnp.where(visible, 0.0, -jnp.inf).astype(F32)
    is_sink = col == nkeys
    ones_kv = (lax.broadcasted_iota(jnp.int32, (nk_pad, LANES), 0) <= nkeys).astype(BF16)
    rows = PAIRS * tq
    scores, values = [], []
    for kvh in range(KV_HEADS):
        k_src, k_oth = (k_all, k_swap) if kvh == 0 else (k_swap, k_all)
        v_src, v_oth = (v_all, v_swap) if kvh == 0 else (v_swap, v_all)
        halves = ((jnp.where(low, k_src, 0.0), jnp.where(low, v_src, 0.0)),
                  (jnp.where(low, 0.0, k_oth), jnp.where(low, 0.0, v_oth)))
        qp = jnp.concatenate([q_lanes(kvh * PAIRS + p) for p in range(PAIRS)], axis=0)
        for parity, (kx, vx) in enumerate(halves):
            s = lax.dot_general(qp, kx.astype(BF16), (((1,), (1,)), ((), ())), preferred_element_type=F32) + bias
            for p in range(PAIRS):
                h = kvh * GROUP + 2 * p + parity
                scores.append(jnp.where(is_sink, sink_ref[:, h:h + 1], s[p * tq:(p + 1) * tq]))
            values.append(jnp.concatenate([vx.astype(BF16), ones_kv], axis=1))
    s_all = jnp.concatenate(scores, axis=0)
    e = jnp.exp(s_all - jnp.max(s_all, axis=1, keepdims=True)).astype(BF16)
    groups = []
    for kvh in range(KV_HEADS):
        total = None
        for parity in range(2):
            idx = kvh * 2 + parity
            ov = jnp.dot(e[idx * rows:(idx + 1) * rows], values[idx], preferred_element_type=F32)
            o = ov[:, :LANES] / ov[:, LANES:]
            total = o if total is None else total + o
        groups.extend(total[p * tq:(p + 1) * tq] for p in range(PAIRS))
    return jnp.concatenate(groups, axis=1).astype(BF16)


def _attn_kernel(*refs, nseg, band_mask):
    q_ref, sink_ref = refs[0], refs[1]
    k_refs = refs[2:2 + nseg]
    v_refs = refs[2 + nseg:2 + 2 * nseg]
    o_ref = refs[2 + 2 * nseg]
    o_ref[...] = _attend(lambda j: q_ref[:, j * LANES:(j + 1) * LANES], q_ref.shape[0], sink_ref,
                         [r[...] for r in k_refs], [r[...] for r in v_refs],
                         pl.program_id(0) if band_mask else None)


def _attention(q, sinks, k_segs, v_segs, tq, grid, band_mask):
    nseg = len(k_segs)
    seg_specs = [pl.BlockSpec(bs, im) for (_, bs, im) in k_segs + v_segs]
    return pl.pallas_call(
        functools.partial(_attn_kernel, nseg=nseg, band_mask=band_mask),
        grid=(grid,),
        in_specs=[pl.BlockSpec((tq, Q_DIM), lambda i: (i, 0)),
                  pl.BlockSpec((1, HEADS), lambda i: (0, 0))] + seg_specs,
        out_specs=pl.BlockSpec((tq, Q_DIM), lambda i: (i, 0)),
        out_shape=jax.ShapeDtypeStruct((grid * tq, Q_DIM), BF16),
        compiler_params=_cparams(("parallel",)),
        name="attention",
    )(q, sinks, *[a for (a, _, _) in k_segs + v_segs])


def _mix_kernel(y_ref, bonus_ref, g_ref, yb_ref, ga_ref, gb_ref, lng_ref, lnb_ref, hsum_ref,
                wa_ref, wb_ref, o_ref, ya_ref):
    hsum = hsum_ref[...]
    mean = _dot3(sum(y_ref[vh] for vh in range(V_HI)), hsum) * (1.0 / HEAD_DIM)
    d = [y_ref[vh] - mean for vh in range(V_HI)]
    var = _dot3(sum(dv * dv for dv in d), hsum) * (1.0 / HEAD_DIM)
    inv = lax.rsqrt(var + RWKV_LN_EPS)
    for vh in range(V_HI):
        sl = slice(vh * LANES, (vh + 1) * LANES)
        yn = d[vh] * inv * lng_ref[:, sl] + lnb_ref[:, sl]
        ya_ref[:, sl] = ((yn + bonus_ref[:, sl]) * g_ref[:, sl]).astype(BF16)

    pa = jnp.dot(ya_ref[...], wa_ref[...], preferred_element_type=F32)
    pb = jnp.dot(yb_ref[...], wb_ref[...], preferred_element_type=F32)
    o_ref[...] = (jax.nn.sigmoid(ga_ref[...]) * pa + jax.nn.sigmoid(gb_ref[...]) * pb).astype(BF16)


def _mix(y, bonus, g, yb, p_all, ln_g, ln_b, hsum, wa, wb, tm):
    m = bonus.shape[0]
    row = pl.BlockSpec((tm, RWKV_DIM), lambda i: (i, 0))
    one = lambda a: pl.BlockSpec(a.shape, lambda i: (0,) * a.ndim)
    resident = lambda a: pl.BlockSpec(a.shape, lambda i: (0, 0), pipeline_mode=pl.Buffered(1))
    return pl.pallas_call(
        _mix_kernel,
        grid=(m // tm,),
        in_specs=[pl.BlockSpec((V_HI, tm, LANES), lambda i: (0, i, 0)), row, row, row,
                  pl.BlockSpec((tm, D_MODEL), lambda i: (i, COL_GA // D_MODEL)),
                  pl.BlockSpec((tm, D_MODEL), lambda i: (i, COL_GB // D_MODEL)),
                  one(ln_g), one(ln_b), one(hsum), resident(wa), resident(wb)],
        out_specs=pl.BlockSpec((tm, D_MODEL), lambda i: (i, 0)),
        out_shape=jax.ShapeDtypeStruct((m, D_MODEL), BF16),
        scratch_shapes=[pltpu.VMEM((tm, RWKV_DIM), BF16)],
        compiler_params=_cparams(("parallel",)),
        name="branch_mix",
    )(y, bonus, g, yb, p_all, p_all, ln_g, ln_b, hsum, wa, wb)


def _out_norm_kernel(*refs, prefixed):
    if prefixed:
        mix_ref, w_ref, x_ref, prefix_ref, g_ref, o_ref = refs
    else:
        (mix_ref, w_ref, x_ref, g_ref, o_ref), prefix_ref = refs, None
    update = _rms(jnp.dot(mix_ref[...], w_ref[...], preferred_element_type=F32), g_ref[...])

    def add_residual(x):
        o_ref[...] = x + update

    _for_prefixed_tile(x_ref, prefix_ref, add_residual)


def _out_norm(mixed, w, x, prefix, g, tm):
    m = mixed.shape[0]
    if prefix is None:
        x_specs, x_args = [pl.BlockSpec((tm, D_MODEL), lambda i: (i, 0))], (x,)
    else:
        lead = prefix.shape[0]
        x_specs = [_prefixed_rows_spec(tm, D_MODEL, lead), pl.BlockSpec((lead, D_MODEL), lambda i: (0, 0))]
        x_args = (x, prefix)
    return pl.pallas_call(
        functools.partial(_out_norm_kernel, prefixed=prefix is not None),
        grid=(m // tm,),
        in_specs=[pl.BlockSpec((tm, D_MODEL), lambda i: (i, 0)),
                  pl.BlockSpec((D_MODEL, D_MODEL), lambda i: (0, 0))] + x_specs
                 + [pl.BlockSpec((1, D_MODEL), lambda i: (0, 0))],
        out_specs=pl.BlockSpec((tm, D_MODEL), lambda i: (i, 0)),
        out_shape=jax.ShapeDtypeStruct((m, D_MODEL), F32),
        compiler_params=_cparams(("parallel",)),
        name="out_norm",
    )(mixed, w, *x_args, g)


HALO = 16


def _gelu_tanh(x):
    return 0.5 * x * (1.0 + jnp.tanh(0.7978845608028654 * (x + 0.044715 * x * x * x)))


def _ffn_kernel(*refs, seq_len):
    if seq_len is None:
        (x_ref, halo_ref, gpre_ref, wg_ref, wv_ref, wd_ref, cw_ref, cb_ref, gpost_ref,
         o_ref, tail_ref, h_ref) = refs
    else:
        (x_ref, sa_ref, sb_ref, gpre_ref, wg_ref, wv_ref, wd_ref, cw_ref, cb_ref, gpost_ref,
         o_ref, tail_ref, h_ref) = refs
    acc_ref = o_ref
    tm = x_ref.shape[0]
    j = pl.program_id(1)
    top = HALO if seq_len is None else 0

    @pl.when(j == 0)
    def _():
        if seq_len is None:
            h_ref[:HALO] = _rms(halo_ref[...], gpre_ref[...]).astype(BF16)
        h_ref[top:] = _rms(x_ref[...], gpre_ref[...]).astype(BF16)
        acc_ref[...] = jnp.zeros_like(acc_ref)

    h_all = h_ref[...]
    gate_all = jnp.dot(h_all, wg_ref[...], preferred_element_type=F32)
    val = jnp.dot(h_all[top:], wv_ref[...], preferred_element_type=F32)
    g0 = gate_all[top:]
    g1 = pltpu.roll(gate_all, 1, 0)[top:]
    g2 = pltpu.roll(gate_all, 2, 0)[top:]
    if seq_len is not None:
        nseq = tm // seq_len
        tf = g0.shape[1]
        tau = lax.broadcasted_iota(jnp.int32, (tm, 1), 0) % seq_len
        expand = lambda s: jnp.concatenate(
            [jnp.broadcast_to(s[b:b + 1, :], (seq_len, tf)) for b in range(nseq)], axis=0)
        st_a = expand(sa_ref[...])
        st_b = expand(sb_ref[...])
        g1 = jnp.where(tau == 0, st_b, g1)
        g2 = jnp.where(tau == 0, st_a, jnp.where(tau == 1, st_b, g2))
    conv = cb_ref[...] + g2 * cw_ref[0:1, :] + g1 * cw_ref[1:2, :] + g0 * cw_ref[2:3, :]
    act = (_gelu_tanh(conv) * val).astype(BF16)
    acc_ref[...] += jnp.dot(act, wd_ref[...], preferred_element_type=F32)

    if seq_len is None:
        tail_ref[...] = g0[tm - SUBLANES:]
    else:
        for b in range(tm // seq_len):
            tail_ref[b] = g0[(b + 1) * seq_len - SUBLANES:(b + 1) * seq_len]

    @pl.when(j == pl.num_programs(1) - 1)
    def _():
        o_ref[...] = x_ref[...] + _rms(acc_ref[...], gpost_ref[...])


def _ffn(x1, conv_prev, fw, tm, tf, seq_len):
    m = x1.shape[0] - (HALO if seq_len is None else 0)
    nm, nf = m // tm, D_FF // tf
    one = lambda a: pl.BlockSpec(a.shape, lambda i, j: (0,) * a.ndim)
    wspecs = [pl.BlockSpec((D_MODEL, tf), lambda i, j: (0, j)),
              pl.BlockSpec((D_MODEL, tf), lambda i, j: (0, D_FF // tf + j)),
              pl.BlockSpec((tf, D_MODEL), lambda i, j: (j, 0)),
              pl.BlockSpec((FFN_CONV, tf), lambda i, j: (0, j)),
              pl.BlockSpec((1, tf), lambda i, j: (0, j)),
              one(fw["g_post"])]
    wargs = (fw["w_up"], fw["w_up"], fw["w_down"], fw["conv_w"], fw["conv_b"], fw["g_post"])
    if seq_len is None:
        assert tm % HALO == 0
        pre_specs = [pl.BlockSpec((pl.Element(tm), pl.Element(D_MODEL)),
                                  lambda i, j: (pl.multiple_of(HALO + i * tm, HALO), 0)),
                     pl.BlockSpec((HALO, D_MODEL), lambda i, j: (i * (tm // HALO), 0)),
                     one(fw["g_pre"])]
        pre_args = (x1, x1, fw["g_pre"])
        tail_shape = (nm, SUBLANES, D_FF)
        tail_spec = pl.BlockSpec((None, SUBLANES, tf), lambda i, j: (i, 0, j))
        hrows = tm + HALO
    else:
        nseq = tm // seq_len
        assert nm == 1
        pre_specs = [pl.BlockSpec((tm, D_MODEL), lambda i, j: (i, 0)),
                     pl.BlockSpec((nseq, tf), lambda i, j: (0, j)),
                     pl.BlockSpec((nseq, tf), lambda i, j: (0, j)),
                     one(fw["g_pre"])]
        pre_args = (x1, conv_prev[:, 0, :], conv_prev[:, 1, :], fw["g_pre"])
        tail_shape = (nseq, SUBLANES, D_FF)
        tail_spec = pl.BlockSpec((nseq, SUBLANES, tf), lambda i, j: (0, 0, j))
        hrows = tm
    return pl.pallas_call(
        functools.partial(_ffn_kernel, seq_len=seq_len),
        grid=(nm, nf),
        in_specs=pre_specs + wspecs,
        out_specs=[pl.BlockSpec((tm, D_MODEL), lambda i, j: (i, 0), pipeline_mode=pl.Buffered(1)), tail_spec],
        out_shape=[jax.ShapeDtypeStruct((m, D_MODEL), F32),
                   jax.ShapeDtypeStruct(tail_shape, F32)],
        scratch_shapes=[pltpu.VMEM((hrows, D_MODEL), BF16)],
        compiler_params=_cparams(("parallel", "arbitrary")),
        name="conv_ffn",
    )(*pre_args, *wargs)


def _pack_lora(x, axis=1):
    cut = lambda a, b: lax.slice_in_dim(x, a, b, axis=axis)

    def z(n):
        shape = list(x.shape)
        shape[axis] = n
        return jnp.zeros(shape, x.dtype)

    xw = cut(0, DECAY_LORA)
    xa = cut(DECAY_LORA, DECAY_LORA + ICLR_LORA)
    xg = cut(DECAY_LORA + ICLR_LORA, DECAY_LORA + ICLR_LORA + GATE_LORA)
    return jnp.concatenate([xw, z(LORA_XA - DECAY_LORA), xa, z(LORA_XG - LORA_XA - ICLR_LORA),
                            xg, z(LORA_W - LORA_XG - GATE_LORA)], axis=axis)


def _unpack_lora(x):
    return jnp.concatenate([x[:, LORA_XW:LORA_XW + DECAY_LORA], x[:, LORA_XA:LORA_XA + ICLR_LORA],
                            x[:, LORA_XG:LORA_XG + GATE_LORA]], axis=1)


def _pack_in_proj_t(wt):
    c = 3 * RWKV_DIM
    q0 = RWKV_PROJ
    kv0 = q0 + Q_DIM
    g0 = kv0 + 2 * KV_DIM
    pieces = [wt[:2 * RWKV_DIM], _to_scan_order(wt[2 * RWKV_DIM:c], 0), wt[q0:kv0],
              wt[g0:g0 + 2 * D_MODEL], _pack_lora(wt[c:RWKV_PROJ], axis=0), wt[kv0:g0]]
    return jnp.concatenate([p.astype(BF16) for p in pieces], axis=0)


def _pack_rwkv_row(x):
    c = 3 * RWKV_DIM
    return jnp.concatenate([x[:, :2 * RWKV_DIM], _to_scan_order(x[:, 2 * RWKV_DIM:c], 1), _pack_lora(x[:, c:])],
                           axis=1)


def _unpack_rwkv_row(p_row):
    return jnp.concatenate([p_row[:, :2 * RWKV_DIM], _to_natural_order(p_row[:, COL_V:COL_V + RWKV_DIM], 1),
                            _unpack_lora(p_row[:, COL_LORA:COL_LORA + LORA_W])], axis=1)


def _pick_tile(m, candidates):
    for c in candidates:
        if m % c == 0:
            return c
    raise ValueError(f"no row tile in {candidates} divides {m}")


def _pad_rows(w, rows):
    return jnp.concatenate([w, jnp.zeros((rows - w.shape[0], w.shape[1]), w.dtype)], axis=0)


def _trunk(x, prefix, lw, *, tm, tm_mix, tb_prep, shift_rows, nseq, cos, sin, scan_attend, ffn_tm,
           ffn_seq_len, conv_prev):
    rows = x.shape[0] + (0 if prefix is None else prefix.shape[0])
    seq_rows = rows // nseq
    p_all = _norm_matmul(x, prefix, lw["g_pre_mix"], lw["w_in_t"], tm, 1792)
    ab, kr, cum, v_scan, g, bonus, q_roped, k_roped, v_att = _rwkv_prep(
        p_all, shift_rows, lw, cos, sin, tb_prep, seq_rows // tb_prep)
    y, s_fin, att = scan_attend((ab, kr, cum, v_scan), q_roped, k_roped, v_att)

    mixed = _mix(y, bonus, g, att, p_all, lw["ln_g"], lw["ln_b"], lw["hsum_scan"],
                 lw["w_branch_a"], lw["w_branch_b"], tm_mix)
    x1 = _out_norm(mixed, lw["w_out"], x, prefix, lw["g_post_mix"], tm_mix)
    x2, tail = _ffn(x1, conv_prev, lw["ffn"], ffn_tm, 512, ffn_seq_len)
    return x2, p_all, k_roped, v_att, s_fin, tail


def kernel(x_prompt, x_sample, cache_meta_k, cache_meta_v, cache_win_k, cache_win_v, state_rwkv, state_shift, state_ffn_conv, meta_tokens, g_pre_mix, w_in, rwkv_mu, rwkv_w0, rwkv_w_lora_b, rwkv_a0, rwkv_a_lora_b, rwkv_g_lora_b, rwkv_k_k, rwkv_k_a, rwkv_r_k, rwkv_ln_g, rwkv_ln_b, attn_sinks, w_branch_a, w_branch_b, w_out, g_post_mix, g_pre_ffn, w_up, ffn_conv_w, ffn_conv_b, w_down, g_post_ffn):
    depth = w_in.shape[0]
    assert depth == 1
    l = 0
    seq = x_prompt.shape[1]
    nb, tn = x_sample.shape[0], x_sample.shape[1]
    row = lambda a: a.reshape(1, -1)
    lane = np.arange(LANES)
    chan = np.arange(RWKV_DIM)
    ones_bd = jnp.asarray(lane[:, None] // HEAD_DIM == lane[None, :] // HEAD_DIM, BF16)
    hsum_nat = jnp.asarray((chan // HEAD_DIM)[:, None] == (lane // V_LO)[None, :], BF16)
    hsum_scan = jnp.asarray((lane // V_LO)[:, None] == (lane // V_LO)[None, :], BF16)
    lw = dict(
        g_pre_mix=row(g_pre_mix[l]),
        w_in_t=_pack_in_proj_t(jnp.swapaxes(w_in[l], 0, 1)),
        mu=_pack_rwkv_row(row(rwkv_mu[l])),
        w0=row(rwkv_w0[l]), a0=row(rwkv_a0[l]), k_k=row(rwkv_k_k[l]), k_a=row(rwkv_k_a[l]),
        r_k=row(rwkv_r_k[l]),
        w_lora=_pad_rows(rwkv_w_lora_b[l], LANES).astype(BF16),
        a_lora=_pad_rows(rwkv_a_lora_b[l], LANES).astype(BF16),
        g_lora=_pad_rows(_to_scan_order(rwkv_g_lora_b[l], 1), 2 * LANES).astype(BF16),
        ones_bd=ones_bd, hsum_nat=hsum_nat, hsum_scan=hsum_scan,
        ln_g=row(_to_scan_order(rwkv_ln_g[l], 0)), ln_b=row(_to_scan_order(rwkv_ln_b[l], 0)),
        w_branch_a=_to_scan_order(w_branch_a[l], 0).astype(BF16), w_branch_b=w_branch_b[l].astype(BF16),
        w_out=w_out[l].astype(BF16), g_post_mix=row(g_post_mix[l]),
        ffn=dict(g_pre=row(g_pre_ffn[l]), w_up=w_up[l].astype(BF16), w_down=w_down[l].astype(BF16),
                 conv_w=ffn_conv_w[l], conv_b=row(ffn_conv_b[l]), g_post=row(g_post_ffn[l])),
    )
    sinks = row(attn_sinks[l])

    mp = N_META + seq
    meta_rows = meta_tokens.astype(x_prompt.dtype)
    nchunk = seq // CHUNK
    tm_p = _pick_tile(mp, (912, 144, 48))
    tm_mix_p = _pick_tile(mp, (432, 144, 48))
    tb_prep_p = _pick_tile(mp, (144, 48))
    cos_p, sin_p = _rope_tables(jnp.arange(mp, dtype=jnp.int32))

    def scan_attend_prompt(scan_ops, q, k, v):
        scan_tb = _pick_tile(mp, (432, 144, 48))
        y, s_fin = _rwkv_scan(*scan_ops, jnp.zeros((1,) + SCAN_STATE, F32), scan_tb, mp // scan_tb)
        mk, mv = k[:N_META], v[:N_META]
        whole = lambda a: (a, (N_META, KV_DIM), lambda c: (0, 0))
        o_m = _attention(q[:N_META], sinks, [whole(mk)], [whole(mv)], N_META, 1, False)
        kf, vf = k[N_META:], v[N_META:]
        band = lambda a: [(a, (CHUNK, KV_DIM), lambda c, d=d: (jnp.maximum(c - d, 0), 0)) for d in (2, 1, 0)]
        o_f = _attention(q[N_META:], sinks, [whole(mk)] + band(kf), [whole(mv)] + band(vf),
                         CHUNK, nchunk, True)
        return y, s_fin, jnp.concatenate([o_m, o_f], axis=0)

    assert N_META == HALO
    yp, p_p, k_p, v_p, sfin_p, tail_p = _trunk(
        x_prompt[0], meta_rows, lw, tm=tm_p, tm_mix=tm_mix_p, tb_prep=tb_prep_p,
        shift_rows=jnp.zeros((1, 1, RWKV_PACK), F32),
        nseq=1, cos=cos_p, sin=sin_p, scan_attend=scan_attend_prompt,
        ffn_tm=_pick_tile(seq, (1024, 64)), ffn_seq_len=None, conv_prev=None)

    ms = nb * tn
    xs = x_sample.reshape(ms, D_MODEL)
    pos_s = N_META + PAST_LEN + jnp.arange(tn, dtype=jnp.int32)
    cos_s, sin_s = _rope_tables(jnp.tile(pos_s, nb))
    nwin = cache_win_k.shape[2]
    cmk = cache_meta_k[l].reshape(nb, N_META, KV_DIM)
    cmv = cache_meta_v[l].reshape(nb, N_META, KV_DIM)
    cwk = cache_win_k[l].reshape(nb, nwin, KV_DIM)
    cwv = cache_win_v[l].reshape(nb, nwin, KV_DIM)

    shift_s = _pack_rwkv_row(state_shift[l].reshape(nb, RWKV_PROJ))[:, None, :]

    def scan_attend_sample(scan_ops, q, k, v):
        y, s_fin = _rwkv_scan(*scan_ops, _state_to_scan(state_rwkv[l]), tn, 1)
        per_b = lambda a, n: (a, (None, n, KV_DIM), lambda b: (b, 0, 0))
        own = lambda a: (a, (tn, KV_DIM), lambda b: (b, 0))
        att = _attention(q, sinks, [per_b(cmk, N_META), per_b(cwk, nwin), own(k)],
                         [per_b(cmv, N_META), per_b(cwv, nwin), own(v)], tn, nb, False)
        return y, s_fin, att

    xs2, p_s, k_s, v_s, sfin_s, tail_s = _trunk(
        xs, None, lw, tm=ms, tm_mix=ms, tb_prep=tn, shift_rows=shift_s, nseq=nb, cos=cos_s, sin=sin_s,
        scan_attend=scan_attend_sample, ffn_tm=ms, ffn_seq_len=tn, conv_prev=state_ffn_conv[l])

    kv5 = lambda a, n: a.reshape(1, -1, n, KV_HEADS, HEAD_DIM)
    y_prompt = yp[None]
    y_sample = xs2.reshape(nb, tn, D_MODEL)
    prompt_meta_k = kv5(k_p[:N_META], N_META)
    prompt_meta_v = kv5(v_p[:N_META], N_META)
    prompt_win_k = kv5(k_p[mp - WINDOW:], WINDOW)
    prompt_win_v = kv5(v_p[mp - WINDOW:], WINDOW)
    prompt_rwkv = _state_from_scan(sfin_p)[None]
    prompt_shift = _unpack_rwkv_row(p_p[mp - 1:mp]).reshape(1, 1, 1, RWKV_PROJ)
    prompt_ffn_conv = tail_p[-1, SUBLANES - (FFN_CONV - 1):].reshape(1, 1, FFN_CONV - 1, D_FF)
    sample_win_k = kv5(k_s, tn)
    sample_win_v = kv5(v_s, tn)
    sample_rwkv = _state_from_scan(sfin_s)[None]
    sample_shift = _unpack_rwkv_row(p_s[tn - 1::tn]).reshape(1, nb, 1, RWKV_PROJ)
    sample_ffn_conv = tail_s[:, SUBLANES - (FFN_CONV - 1):].reshape(1, nb, FFN_CONV - 1, D_FF)
    return (y_prompt, y_sample, prompt_meta_k, prompt_meta_v, prompt_win_k, prompt_win_v,
            prompt_rwkv, prompt_shift, prompt_ffn_conv, sample_win_k, sample_win_v,
            sample_rwkv, sample_shift, sample_ffn_conv)
```

```python
import functools
import math

import numpy as np
import jax
import jax.numpy as jnp
from jax import lax
from jax.experimental import pallas as pl
from jax.experimental.pallas import tpu as pltpu

F32 = jnp.float32
BF16 = jnp.bfloat16

D_MODEL = 2048
CHUNK = 64
N_META = 16
NORM_EPS = 1e-6
HEADS = 16
HEAD_DIM = 64
RWKV_DIM = HEADS * HEAD_DIM
DECAY_LORA = 64
ICLR_LORA = 64
GATE_LORA = 160
RWKV_PROJ = 3 * RWKV_DIM + DECAY_LORA + ICLR_LORA + GATE_LORA
RWKV_LN_EPS = 64e-5
KV_HEADS = 2
GROUP = HEADS // KV_HEADS
Q_DIM = HEADS * HEAD_DIM
KV_DIM = KV_HEADS * HEAD_DIM
WINDOW = 128
ROPE_THETA = 10000.0
D_FF = 3 * D_MODEL
FFN_CONV = 3
PAST_LEN = 4096

SUBLANES = 8
LANES = 128
V_LO = LANES // HEADS
V_HI = HEAD_DIM // V_LO
K_BLOCKS = HEAD_DIM // SUBLANES
SCAN_BLOCK = SUBLANES

COL_R, COL_K, COL_V, COL_Q = 0, 1024, 2048, 3072
COL_GA, COL_GB = 4096, 6144
COL_LORA = 8192
LORA_W = 512
LORA_XW, LORA_XA, LORA_XG = 0, 128, 256
COL_KV = 8704
N_PACK = 8960
RWKV_PACK = 3 * RWKV_DIM + LORA_W

VMEM_LIMIT = 56 * 1024 * 1024


def _swap_channel_order(x, axis, outer, inner):
    axis = axis % x.ndim
    shape = x.shape
    x = x.reshape(shape[:axis] + (outer, inner, V_LO) + shape[axis + 1:])
    return jnp.swapaxes(x, axis, axis + 1).reshape(shape)


def _to_scan_order(x, axis):
    return _swap_channel_order(x, axis, HEADS, V_HI)


def _to_natural_order(x, axis):
    return _swap_channel_order(x, axis, V_HI, HEADS)


def _cparams(sem):
    return pltpu.CompilerParams(dimension_semantics=sem, vmem_limit_bytes=VMEM_LIMIT)


def _rms(x, g):
    return x * lax.rsqrt(jnp.mean(x * x, axis=-1, keepdims=True) + NORM_EPS) * g


def _split3(x):
    hi = x.astype(BF16)
    r1 = x - hi.astype(F32)
    mid = r1.astype(BF16)
    lo = (r1 - mid.astype(F32)).astype(BF16)
    return hi, mid, lo


def _dot3(x, mat):
    hi, mid, lo = _split3(x)
    acc = jnp.dot(hi, mat, preferred_element_type=F32)
    acc = acc + jnp.dot(mid, mat, preferred_element_type=F32)
    return acc + jnp.dot(lo, mat, preferred_element_type=F32)


def _seg_sum(x, ones_bd):
    hi, mid, lo = _split3(x)
    outs = []
    for j in range(x.shape[1] // LANES):
        sl = slice(j * LANES, (j + 1) * LANES)
        acc = jnp.dot(hi[:, sl], ones_bd, preferred_element_type=F32)
        acc = acc + jnp.dot(mid[:, sl], ones_bd, preferred_element_type=F32)
        acc = acc + jnp.dot(lo[:, sl], ones_bd, preferred_element_type=F32)
        outs.append(acc)
    return jnp.concatenate(outs, axis=1)


def _tile_lanes(x, n):
    return jnp.concatenate([x] * n, axis=1)


def _prefixed_rows_spec(tm, width, lead):
    def index(i, *_):
        return pl.multiple_of(jnp.maximum(i * tm - lead, 0), SUBLANES), 0
    return pl.BlockSpec((pl.Element(tm), pl.Element(width)), index)


def _for_prefixed_tile(x_ref, prefix_ref, use):
    if prefix_ref is None:
        use(x_ref[...])
        return
    lead = prefix_ref.shape[0]

    @pl.when(pl.program_id(0) == 0)
    def _():
        use(jnp.concatenate([prefix_ref[...], x_ref[:x_ref.shape[0] - lead, :]], axis=0))

    @pl.when(pl.program_id(0) != 0)
    def _():
        use(x_ref[...])


def _norm_matmul_kernel(*refs, prefixed):
    if prefixed:
        x_ref, prefix_ref, g_ref, w_ref, o_ref, h_ref = refs
    else:
        (x_ref, g_ref, w_ref, o_ref, h_ref), prefix_ref = refs, None

    @pl.when(pl.program_id(1) == 0)
    def _():
        def normalise(x):
            h_ref[...] = _rms(x, g_ref[...]).astype(BF16)

        _for_prefixed_tile(x_ref, prefix_ref, normalise)

    o_ref[...] = lax.dot_general(h_ref[...], w_ref[...], (((1,), (1,)), ((), ())),
                                 preferred_element_type=F32)


def _norm_matmul(x, prefix, g, w_t, tm, tn):
    d = x.shape[1]
    lead = 0 if prefix is None else prefix.shape[0]
    m = x.shape[0] + lead
    n = w_t.shape[0]
    if prefix is None:
        x_specs, x_args = [pl.BlockSpec((tm, d), lambda i, j: (i, 0))], (x,)
    else:
        assert lead % SUBLANES == 0 and tm % SUBLANES == 0
        x_specs = [_prefixed_rows_spec(tm, d, lead), pl.BlockSpec((lead, d), lambda i, j: (0, 0))]
        x_args = (x, prefix)
    return pl.pallas_call(
        functools.partial(_norm_matmul_kernel, prefixed=prefix is not None),
        grid=(m // tm, n // tn),
        in_specs=x_specs + [pl.BlockSpec((1, d), lambda i, j: (0, 0)),
                            pl.BlockSpec((tn, d), lambda i, j: (j, 0))],
        out_specs=pl.BlockSpec((tm, tn), lambda i, j: (i, j)),
        out_shape=jax.ShapeDtypeStruct((m, n), F32),
        scratch_shapes=[pltpu.VMEM((tm, d), BF16)],
        compiler_params=_cparams(("parallel", "arbitrary")),
        name="norm_matmul",
    )(*x_args, g, w_t)


def _store_head_pairs(ref, a, b):
    lane = lax.broadcasted_iota(jnp.int32, (a.shape[0], LANES), 1)
    low = lane < HEAD_DIM
    for j in range(a.shape[1] // LANES):
        sl = slice(j * LANES, (j + 1) * LANES)
        aj, bj = a[:, sl], b[:, sl]
        ref[2 * j] = jnp.where(low, aj, pltpu.roll(bj, HEAD_DIM, 1))
        ref[2 * j + 1] = jnp.where(low, pltpu.roll(aj, HEAD_DIM, 1), bj)


def _rwkv_prep_kernel(r_ref, k_ref, v_ref, l_ref, hr_ref, hk_ref, hv_ref, hl_ref, state_ref,
                      q_ref, kv_ref, cos_ref, sin_ref,
                      mu_ref, w0_ref, a0_ref, kk_ref, ka_ref,
                      rk_ref, ww_ref, wa_ref, wg_ref, ones_ref, hsum_ref,
                      ab_ref, kr_ref, cum_ref, vo_ref, g_ref, bonus_ref, aq_ref, ak_ref, av_ref, *, nblk):
    rows = r_ref.shape[0]
    first = lax.broadcasted_iota(jnp.int32, (rows, 1), 0) == 0
    seq_start = pl.program_id(0) % nblk == 0

    def shifted(x, halo_ref, lo, hi):
        before = jnp.where(seq_start, state_ref[:, lo:hi], halo_ref[SUBLANES - 1:SUBLANES, :])
        prev = jnp.where(first, before, pltpu.roll(x, 1, 0))
        return x + (prev - x) * mu_ref[:, lo:hi]

    r = shifted(r_ref[...], hr_ref, 0, RWKV_DIM)
    k = shifted(k_ref[...], hk_ref, RWKV_DIM, 2 * RWKV_DIM)
    v = shifted(v_ref[...], hv_ref, 2 * RWKV_DIM, 3 * RWKV_DIM)
    lora = shifted(l_ref[...], hl_ref, 3 * RWKV_DIM, RWKV_PACK)
    xw = lora[:, LORA_XW:LORA_XW + LANES]
    xa = lora[:, LORA_XA:LORA_XA + LANES]
    xg = lora[:, LORA_XG:LORA_XG + 2 * LANES]

    w_pre = w0_ref[...] + jnp.dot(jnp.tanh(xw).astype(BF16), ww_ref[...], preferred_element_type=F32)
    log_decay = -math.exp(-0.5) * jax.nn.sigmoid(w_pre)
    step_in_block = lax.broadcasted_iota(jnp.int32, (rows, 1), 0) % SCAN_BLOCK
    log_cum = log_decay
    shift = 1
    while shift < SCAN_BLOCK:
        log_cum = log_cum + jnp.where(step_in_block >= shift, pltpu.roll(log_cum, shift, 0), 0.0)
        shift *= 2
    cum = jnp.exp(log_cum)
    cum_before = jnp.exp(log_cum - log_decay)
    inv_cum = jnp.exp(-log_cum)
    a = jax.nn.sigmoid(a0_ref[...] + jnp.dot(xa.astype(BF16), wa_ref[...], preferred_element_type=F32))
    g = jnp.dot(jax.nn.sigmoid(xg).astype(BF16), wg_ref[...], preferred_element_type=F32)

    kk = k * kk_ref[...]
    kk = kk * lax.rsqrt(jnp.maximum(_seg_sum(kk * kk, ones_ref[...]), 1e-24))
    k_mod = k * (1.0 + (a - 1.0) * ka_ref[...])
    rk_sum = _dot3(r * k_mod * rk_ref[...], hsum_ref[...])

    _store_head_pairs(ab_ref, -kk * cum_before, kk * a * inv_cum)
    _store_head_pairs(kr_ref, k_mod * inv_cum, r * cum)
    _store_head_pairs(cum_ref, cum, cum)
    for vh in range(V_HI):
        vo_ref[vh] = v[:, vh * LANES:(vh + 1) * LANES]
    g_ref[...] = g
    bonus_ref[...] = _tile_lanes(rk_sum, V_HI) * v
    _rope_rows(q_ref, kv_ref, cos_ref, sin_ref, aq_ref, ak_ref, av_ref)


def _rwkv_prep(p_all, state_rows, pw, cos, sin, tb, nblk):
    m = p_all.shape[0]
    nt = m // tb
    assert tb % SUBLANES == 0 and nt == state_rows.shape[0] * nblk
    row = lambda c: pl.BlockSpec((tb, RWKV_DIM), lambda i, c=c: (i, c))
    before = lambda i: jnp.maximum(i * (tb // SUBLANES) - 1, 0)
    halo = lambda c: pl.BlockSpec((SUBLANES, RWKV_DIM), lambda i, c=c: (before(i), c))
    full = lambda a: pl.BlockSpec(a.shape, lambda i: (0,) * a.ndim)
    params = (pw["mu"], pw["w0"], pw["a0"], pw["k_k"], pw["k_a"], pw["r_k"],
              pw["w_lora"], pw["a_lora"], pw["g_lora"], pw["ones_bd"], pw["hsum_nat"])
    per_head = jax.ShapeDtypeStruct((HEADS, m, LANES), F32)
    per_vhi = jax.ShapeDtypeStruct((V_HI, m, LANES), F32)
    narrow = jax.ShapeDtypeStruct((m, RWKV_DIM), F32)
    return pl.pallas_call(
        functools.partial(_rwkv_prep_kernel, nblk=nblk),
        grid=(nt,),
        in_specs=[row(COL_R // RWKV_DIM), row(COL_K // RWKV_DIM), row(COL_V // RWKV_DIM),
                  pl.BlockSpec((tb, LORA_W), lambda i: (i, COL_LORA // LORA_W)),
                  halo(COL_R // RWKV_DIM), halo(COL_K // RWKV_DIM), halo(COL_V // RWKV_DIM),
                  pl.BlockSpec((SUBLANES, LORA_W), lambda i: (before(i), COL_LORA // LORA_W)),
                  pl.BlockSpec((None, 1, RWKV_PACK), lambda i: (i // nblk, 0, 0)),
                  pl.BlockSpec((tb, Q_DIM), lambda i: (i, COL_Q // Q_DIM)),
                  pl.BlockSpec((tb, 2 * KV_DIM), lambda i: (i, COL_KV // (2 * KV_DIM))),
                  pl.BlockSpec((tb, LANES), lambda i: (i, 0)),
                  pl.BlockSpec((tb, LANES), lambda i: (i, 0))]
                 + [full(a) for a in params],
        out_specs=[pl.BlockSpec((HEADS, tb, LANES), lambda i: (0, i, 0))] * 3
                  + [pl.BlockSpec((V_HI, tb, LANES), lambda i: (0, i, 0))]
                  + [pl.BlockSpec((tb, RWKV_DIM), lambda i: (i, 0))] * 2
                  + [pl.BlockSpec((tb, Q_DIM), lambda i: (i, 0)),
                     pl.BlockSpec((tb, KV_DIM), lambda i: (i, 0)),
                     pl.BlockSpec((tb, KV_DIM), lambda i: (i, 0))],
        out_shape=[per_head, per_head, per_head, per_vhi, narrow, narrow,
                   jax.ShapeDtypeStruct((m, Q_DIM), BF16),
                   jax.ShapeDtypeStruct((m, KV_DIM), F32),
                   jax.ShapeDtypeStruct((m, KV_DIM), F32)],
        compiler_params=_cparams(("parallel",)),
        name="rwkv_prep",
    )(p_all, p_all, p_all, p_all, p_all, p_all, p_all, p_all, state_rows, p_all, p_all, cos, sin, *params)


def _sublane_allsum(x):
    x = x + pltpu.roll(x, 4, 0)
    x = x + pltpu.roll(x, 2, 0)
    return x + pltpu.roll(x, 1, 0)


PACKED_ROW = (0, 4, 2, 6, 1, 5, 3, 7)


def _packed_sublane_sums(a):
    sub = lax.broadcasted_iota(jnp.int32, (SUBLANES, LANES), 0)
    top_half = sub < 4
    z = [jnp.where(top_half, a[2 * j] + pltpu.roll(a[2 * j], 4, 0), a[2 * j + 1] + pltpu.roll(a[2 * j + 1], 4, 0))
         for j in range(4)]
    first_pair = (sub % 4) < 2
    w = [jnp.where(first_pair, z[2 * j] + pltpu.roll(z[2 * j], 6, 0), z[2 * j + 1] + pltpu.roll(z[2 * j + 1], 2, 0))
         for j in range(2)]
    return jnp.where(sub % 2 == 0, w[0] + pltpu.roll(w[0], 7, 0), w[1] + pltpu.roll(w[1], 1, 0))


def _scan_kernel(ab_ref, kr_ref, cum_ref, v_ref, s0_ref, y_ref, st_ref, ops_ref, blk_ref, *, tb, nblk):
    @pl.when(pl.program_id(0) % nblk == 0)
    def _():
        st_ref[...] = s0_ref[...]

    def col_tiles(ref, t):
        rep = jnp.concatenate(
            [jnp.broadcast_to(ref[h, pl.ds(t, 1), :], (V_LO, LANES)) for h in range(HEADS)], axis=0)
        return rep.T

    def stage(t, slot):
        ab = col_tiles(ab_ref, t)
        kr = col_tiles(kr_ref, t)
        ops_ref[slot, 0] = ab[:HEAD_DIM]
        ops_ref[slot, 1] = ab[HEAD_DIM:]
        ops_ref[slot, 2] = kr[:HEAD_DIM]
        ops_ref[slot, 3] = kr[HEAD_DIM:]

    def op_tile(slot, which, kb):
        return ops_ref[slot, which, kb * SUBLANES:(kb + 1) * SUBLANES, :]

    def step(t, slot, block_end):
        ksl = lambda kb: slice(kb * SUBLANES, (kb + 1) * SUBLANES)
        acc = [None] * V_HI
        for kb in range(K_BLOCKS):
            alpha = op_tile(slot, 0, kb)
            for vh in range(V_HI):
                p = st_ref[vh, ksl(kb), :] * alpha
                acc[vh] = p if acc[vh] is None else acc[vh] + p
        u = [_sublane_allsum(acc[vh]) for vh in range(V_HI)]
        vb = [jnp.broadcast_to(v_ref[vh, pl.ds(t, 1), :], (SUBLANES, LANES)) for vh in range(V_HI)]
        yacc = [None] * V_HI
        for kb in range(K_BLOCKS):
            beta = op_tile(slot, 1, kb)
            kmod = op_tile(slot, 2, kb)
            rr = op_tile(slot, 3, kb)
            for vh in range(V_HI):
                s_new = st_ref[vh, ksl(kb), :] + beta * u[vh] + kmod * vb[vh]
                p = s_new * rr
                yacc[vh] = p if yacc[vh] is None else yacc[vh] + p
                if block_end:
                    s_new = s_new * blk_ref[ksl(kb), :]
                st_ref[vh, ksl(kb), :] = s_new
        ysum = _packed_sublane_sums(yacc)
        for vh in range(V_HI):
            y_ref[vh, pl.ds(t, 1), :] = ysum[PACKED_ROW[vh]:PACKED_ROW[vh] + 1, :]

    stage(0, 0)

    def body(i, carry):
        base = pl.multiple_of(i * SCAN_BLOCK, SCAN_BLOCK)
        following = pl.multiple_of(jnp.minimum(base + SCAN_BLOCK, tb - SCAN_BLOCK), SCAN_BLOCK)
        blk_ref[...] = col_tiles(cum_ref, base + (SCAN_BLOCK - 1))[:HEAD_DIM]
        for j in range(SCAN_BLOCK):
            nxt = base + (j + 1) if j + 1 < SCAN_BLOCK else following
            stage(nxt, (j + 1) % 2)
            step(base + j, j % 2, j + 1 == SCAN_BLOCK)
        return carry

    lax.fori_loop(0, tb // SCAN_BLOCK, body, 0)


SCAN_STATE = (V_HI, HEAD_DIM, LANES)
SCAN_SCRATCH = [pltpu.VMEM((2, 4, HEAD_DIM, LANES), F32), pltpu.VMEM((HEAD_DIM, LANES), F32)]


def _rwkv_scan(ab, kr, cum, v, s0, tb, nblk):
    steps = v.shape[1]
    nseq = s0.shape[0]
    grid = steps // tb
    assert grid == nseq * nblk and tb % SCAN_BLOCK == 0 and SCAN_BLOCK % SUBLANES == 0
    per_head = pl.BlockSpec((HEADS, tb, LANES), lambda i: (0, i, 0))
    per_vhi = pl.BlockSpec((V_HI, tb, LANES), lambda i: (0, i, 0))
    state = pl.BlockSpec((None,) + SCAN_STATE, lambda i: (i // nblk, 0, 0, 0))
    return pl.pallas_call(
        functools.partial(_scan_kernel, tb=tb, nblk=nblk),
        grid=(grid,),
        in_specs=[per_head, per_head, per_head, per_vhi, state],
        out_specs=[per_vhi, state],
        out_shape=[jax.ShapeDtypeStruct((V_HI, steps, LANES), F32),
                   jax.ShapeDtypeStruct((nseq,) + SCAN_STATE, F32)],
        scratch_shapes=SCAN_SCRATCH,
        compiler_params=_cparams(("arbitrary",)),
        name="rwkv_scan",
    )(ab, kr, cum, v, s0)


def _state_to_scan(s):
    b = s.shape[0]
    s = s.reshape(b, HEADS, V_HI, V_LO, HEAD_DIM)
    return s.transpose(0, 2, 4, 1, 3).reshape(b, V_HI, HEAD_DIM, LANES)


def _state_from_scan(st):
    b = st.shape[0]
    st = st.reshape(b, V_HI, HEAD_DIM, HEADS, V_LO)
    return st.transpose(0, 3, 1, 4, 2).reshape(b, HEADS, HEAD_DIM, HEAD_DIM)


def _rope_tile(x, cos, sin_signed):
    lane = lax.broadcasted_iota(jnp.int32, x.shape, 1)
    first_half = (lane % HEAD_DIM) < (HEAD_DIM // 2)
    rot = jnp.where(first_half, pltpu.roll(x, LANES - HEAD_DIM // 2, 1), pltpu.roll(x, HEAD_DIM // 2, 1))
    return x * cos + rot * sin_signed


def _rope_rows(q_ref, kv_ref, cos_ref, sin_ref, qo_ref, ko_ref, vo_ref):
    cos = cos_ref[...]
    sin = sin_ref[...]
    scale = HEAD_DIM ** -0.5
    assert math.log2(scale).is_integer()
    for j in range(Q_DIM // LANES):
        sl = slice(j * LANES, (j + 1) * LANES)
        qo_ref[:, sl] = (_rope_tile(q_ref[:, sl], cos, sin) * scale).astype(BF16)
    ko_ref[...] = _rope_tile(kv_ref[:, :KV_DIM], cos, sin)
    vo_ref[...] = kv_ref[:, KV_DIM:]


def _rope_tables(pos):
    half = HEAD_DIM // 2
    inv_freq = ROPE_THETA ** (-jnp.arange(half, dtype=F32) / half)
    ang = pos.astype(F32)[:, None] * inv_freq[None, :]
    cos = jnp.cos(ang)
    sin = jnp.sin(ang)
    cos = jnp.concatenate([cos, cos, cos, cos], axis=1)
    sin = jnp.concatenate([-sin, sin, -sin, sin], axis=1)
    return cos, sin


PAIRS = GROUP // 2


def _attend(q_lanes, tq, sink_ref, k_parts, v_parts, band_chunk):
    nkeys = sum(p.shape[0] for p in k_parts)
    nk_pad = pl.cdiv(nkeys + 1, LANES) * LANES
    pad = jnp.zeros((nk_pad - nkeys, LANES), F32)
    k_all = jnp.concatenate(list(k_parts) + [pad], axis=0)
    v_all = jnp.concatenate(list(v_parts) + [pad], axis=0)
    low = lax.broadcasted_iota(jnp.int32, (nk_pad, LANES), 1) < HEAD_DIM
    k_swap = pltpu.roll(k_all, HEAD_DIM, 1)
    v_swap = pltpu.roll(v_all, HEAD_DIM, 1)
    col = lax.broadcasted_iota(jnp.int32, (1, nk_pad), 1)
    visible = col < nkeys
    if band_chunk is not None:
        first_valid = N_META + jnp.maximum(2 - band_chunk, 0) * CHUNK
        visible = visible & ((col < N_META) | (col >= first_valid))
    bias = jnp.where(visible, 0.0, -jnp.inf).astype(F32)
    is_sink = col == nkeys
    ones_kv = (lax.broadcasted_iota(jnp.int32, (nk_pad, LANES), 0) <= nkeys).astype(BF16)
    rows = PAIRS * tq
    scores, values = [], []
    for kvh in range(KV_HEADS):
        k_src, k_oth = (k_all, k_swap) if kvh == 0 else (k_swap, k_all)
        v_src, v_oth = (v_all, v_swap) if kvh == 0 else (v_swap, v_all)
        halves = ((jnp.where(low, k_src, 0.0), jnp.where(low, v_src, 0.0)),
                  (jnp.where(low, 0.0, k_oth), jnp.where(low, 0.0, v_oth)))
        qp = jnp.concatenate([q_lanes(kvh * PAIRS + p) for p in range(PAIRS)], axis=0)
        for parity, (kx, vx) in enumerate(halves):
            s = lax.dot_general(qp, kx.astype(BF16), (((1,), (1,)), ((), ())), preferred_element_type=F32) + bias
            for p in range(PAIRS):
                h = kvh * GROUP + 2 * p + parity
                scores.append(jnp.where(is_sink, sink_ref[:, h:h + 1], s[p * tq:(p + 1) * tq]))
            values.append(jnp.concatenate([vx.astype(BF16), ones_kv], axis=1))
    s_all = jnp.concatenate(scores, axis=0)
    e = jnp.exp(s_all - jnp.max(s_all, axis=1, keepdims=True)).astype(BF16)
    groups = []
    for kvh in range(KV_HEADS):
        total = None
        for parity in range(2):
            idx = kvh * 2 + parity
            ov = jnp.dot(e[idx * rows:(idx + 1) * rows], values[idx], preferred_element_type=F32)
            o = ov[:, :LANES] / ov[:, LANES:]
            total = o if total is None else total + o
        groups.extend(total[p * tq:(p + 1) * tq] for p in range(PAIRS))
    return jnp.concatenate(groups, axis=1).astype(BF16)


def _attn_kernel(*refs, nseg, band_mask):
    q_ref, sink_ref = refs[0], refs[1]
    k_refs = refs[2:2 + nseg]
    v_refs = refs[2 + nseg:2 + 2 * nseg]
    o_ref = refs[2 + 2 * nseg]
    o_ref[...] = _attend(lambda j: q_ref[:, j * LANES:(j + 1) * LANES], q_ref.shape[0], sink_ref,
                         [r[...] for r in k_refs], [r[...] for r in v_refs],
                         pl.program_id(0) if band_mask else None)


def _attention(q, sinks, k_segs, v_segs, tq, grid, band_mask, q_row0=0):
    nseg = len(k_segs)
    seg_specs = [pl.BlockSpec(bs, im) for (_, bs, im) in k_segs + v_segs]
    if q_row0:
        qspec = pl.BlockSpec((pl.Element(tq), pl.Element(Q_DIM)),
                             lambda i: (pl.multiple_of(q_row0 + i * tq, 16), 0))
    else:
        qspec = pl.BlockSpec((tq, Q_DIM), lambda i: (i, 0))
    return pl.pallas_call(
        functools.partial(_attn_kernel, nseg=nseg, band_mask=band_mask),
        grid=(grid,),
        in_specs=[qspec, pl.BlockSpec((1, HEADS), lambda i: (0, 0))] + seg_specs,
        out_specs=pl.BlockSpec((tq, Q_DIM), lambda i: (i, 0)),
        out_shape=jax.ShapeDtypeStruct((grid * tq, Q_DIM), BF16),
        compiler_params=_cparams(("parallel",)),
        name="attention",
    )(q, sinks, *[a for (a, _, _) in k_segs + v_segs])


def _mix_kernel(y_ref, bonus_ref, g_ref, yb_ref, ga_ref, gb_ref, lng_ref, lnb_ref, hsum_ref,
                wa_ref, wb_ref, o_ref, ya_ref):
    hsum = hsum_ref[...]
    mean = _dot3(sum(y_ref[vh] for vh in range(V_HI)), hsum) * (1.0 / HEAD_DIM)
    d = [y_ref[vh] - mean for vh in range(V_HI)]
    var = _dot3(sum(dv * dv for dv in d), hsum) * (1.0 / HEAD_DIM)
    inv = lax.rsqrt(var + RWKV_LN_EPS)
    for vh in range(V_HI):
        sl = slice(vh * LANES, (vh + 1) * LANES)
        yn = d[vh] * inv * lng_ref[:, sl] + lnb_ref[:, sl]
        ya_ref[:, sl] = ((yn + bonus_ref[:, sl]) * g_ref[:, sl]).astype(BF16)

    pa = jnp.dot(ya_ref[...], wa_ref[...], preferred_element_type=F32)
    pb = jnp.dot(yb_ref[...], wb_ref[...], preferred_element_type=F32)
    o_ref[...] = (jax.nn.sigmoid(ga_ref[...]) * pa + jax.nn.sigmoid(gb_ref[...]) * pb).astype(BF16)


def _mix(y, bonus, g, yb, p_all, ln_g, ln_b, hsum, wa, wb, tm):
    m = bonus.shape[0]
    row = pl.BlockSpec((tm, RWKV_DIM), lambda i: (i, 0))
    one = lambda a: pl.BlockSpec(a.shape, lambda i: (0,) * a.ndim)
    resident = lambda a: pl.BlockSpec(a.shape, lambda i: (0, 0), pipeline_mode=pl.Buffered(1))
    return pl.pallas_call(
        _mix_kernel,
        grid=(m // tm,),
        in_specs=[pl.BlockSpec((V_HI, tm, LANES), lambda i: (0, i, 0)), row, row, row,
                  pl.BlockSpec((tm, D_MODEL), lambda i: (i, COL_GA // D_MODEL)),
                  pl.BlockSpec((tm, D_MODEL), lambda i: (i, COL_GB // D_MODEL)),
                  one(ln_g), one(ln_b), one(hsum), resident(wa), resident(wb)],
        out_specs=pl.BlockSpec((tm, D_MODEL), lambda i: (i, 0)),
        out_shape=jax.ShapeDtypeStruct((m, D_MODEL), BF16),
        scratch_shapes=[pltpu.VMEM((tm, RWKV_DIM), BF16)],
        compiler_params=_cparams(("parallel",)),
        name="branch_mix",
    )(y, bonus, g, yb, p_all, p_all, ln_g, ln_b, hsum, wa, wb)


def _out_norm_kernel(*refs, prefixed):
    if prefixed:
        mix_ref, w_ref, x_ref, prefix_ref, g_ref, o_ref = refs
    else:
        (mix_ref, w_ref, x_ref, g_ref, o_ref), prefix_ref = refs, None
    update = _rms(jnp.dot(mix_ref[...], w_ref[...], preferred_element_type=F32), g_ref[...])

    def add_residual(x):
        o_ref[...] = x + update

    _for_prefixed_tile(x_ref, prefix_ref, add_residual)


def _out_norm(mixed, w, x, prefix, g, tm):
    m = mixed.shape[0]
    if prefix is None:
        x_specs, x_args = [pl.BlockSpec((tm, D_MODEL), lambda i: (i, 0))], (x,)
    else:
        lead = prefix.shape[0]
        x_specs = [_prefixed_rows_spec(tm, D_MODEL, lead), pl.BlockSpec((lead, D_MODEL), lambda i: (0, 0))]
        x_args = (x, prefix)
    return pl.pallas_call(
        functools.partial(_out_norm_kernel, prefixed=prefix is not None),
        grid=(m // tm,),
        in_specs=[pl.BlockSpec((tm, D_MODEL), lambda i: (i, 0)),
                  pl.BlockSpec((D_MODEL, D_MODEL), lambda i: (0, 0))] + x_specs
                 + [pl.BlockSpec((1, D_MODEL), lambda i: (0, 0))],
        out_specs=pl.BlockSpec((tm, D_MODEL), lambda i: (i, 0)),
        out_shape=jax.ShapeDtypeStruct((m, D_MODEL), F32),
        compiler_params=_cparams(("parallel",)),
        name="out_norm",
    )(mixed, w, *x_args, g)


HALO = 16


def _gelu_tanh(x):
    return 0.5 * x * (1.0 + jnp.tanh(0.7978845608028654 * (x + 0.044715 * x * x * x)))


def _ffn_kernel(*refs, seq_len):
    if seq_len is None:
        (x_ref, halo_ref, gpre_ref, wg_ref, wv_ref, wd_ref, cw_ref, cb_ref, gpost_ref,
         o_ref, tail_ref, h_ref) = refs
    else:
        (x_ref, sa_ref, sb_ref, gpre_ref, wg_ref, wv_ref, wd_ref, cw_ref, cb_ref, gpost_ref,
         o_ref, tail_ref, h_ref) = refs
    acc_ref = o_ref
    tm = x_ref.shape[0]
    j = pl.program_id(1)
    top = HALO if seq_len is None else 0

    @pl.when(j == 0)
    def _():
        if seq_len is None:
            h_ref[:HALO] = _rms(halo_ref[...], gpre_ref[...]).astype(BF16)
        h_ref[top:] = _rms(x_ref[...], gpre_ref[...]).astype(BF16)
        acc_ref[...] = jnp.zeros_like(acc_ref)

    h_all = h_ref[...]
    gate_all = jnp.dot(h_all, wg_ref[...], preferred_element_type=F32)
    val = jnp.dot(h_all[top:], wv_ref[...], preferred_element_type=F32)
    g0 = gate_all[top:]
    g1 = pltpu.roll(gate_all, 1, 0)[top:]
    g2 = pltpu.roll(gate_all, 2, 0)[top:]
    if seq_len is not None:
        nseq = tm // seq_len
        tf = g0.shape[1]
        tau = lax.broadcasted_iota(jnp.int32, (tm, 1), 0) % seq_len
        expand = lambda s: jnp.concatenate(
            [jnp.broadcast_to(s[b:b + 1, :], (seq_len, tf)) for b in range(nseq)], axis=0)
        st_a = expand(sa_ref[...])
        st_b = expand(sb_ref[...])
        g1 = jnp.where(tau == 0, st_b, g1)
        g2 = jnp.where(tau == 0, st_a, jnp.where(tau == 1, st_b, g2))
    conv = cb_ref[...] + g2 * cw_ref[0:1, :] + g1 * cw_ref[1:2, :] + g0 * cw_ref[2:3, :]
    act = (_gelu_tanh(conv) * val).astype(BF16)
    acc_ref[...] += jnp.dot(act, wd_ref[...], preferred_element_type=F32)

    if seq_len is None:
        tail_ref[...] = g0[tm - SUBLANES:]
    else:
        for b in range(tm // seq_len):
            tail_ref[b] = g0[(b + 1) * seq_len - SUBLANES:(b + 1) * seq_len]

    @pl.when(j == pl.num_programs(1) - 1)
    def _():
        o_ref[...] = x_ref[...] + _rms(acc_ref[...], gpost_ref[...])


def _ffn(x1, conv_prev, fw, tm, tf, seq_len):
    m = x1.shape[0] - (HALO if seq_len is None else 0)
    nm, nf = m // tm, D_FF // tf
    one = lambda a: pl.BlockSpec(a.shape, lambda i, j: (0,) * a.ndim)
    wspecs = [pl.BlockSpec((D_MODEL, tf), lambda i, j: (0, j)),
              pl.BlockSpec((D_MODEL, tf), lambda i, j: (0, D_FF // tf + j)),
              pl.BlockSpec((tf, D_MODEL), lambda i, j: (j, 0)),
              pl.BlockSpec((FFN_CONV, tf), lambda i, j: (0, j)),
              pl.BlockSpec((1, tf), lambda i, j: (0, j)),
              one(fw["g_post"])]
    wargs = (fw["w_up"], fw["w_up"], fw["w_down"], fw["conv_w"], fw["conv_b"], fw["g_post"])
    if seq_len is None:
        assert tm % HALO == 0
        pre_specs = [pl.BlockSpec((pl.Element(tm), pl.Element(D_MODEL)),
                                  lambda i, j: (pl.multiple_of(HALO + i * tm, HALO), 0)),
                     pl.BlockSpec((HALO, D_MODEL), lambda i, j: (i * (tm // HALO), 0)),
                     one(fw["g_pre"])]
        pre_args = (x1, x1, fw["g_pre"])
        tail_shape = (nm, SUBLANES, D_FF)
        tail_spec = pl.BlockSpec((None, SUBLANES, tf), lambda i, j: (i, 0, j))
        hrows = tm + HALO
    else:
        nseq = tm // seq_len
        assert nm == 1
        pre_specs = [pl.BlockSpec((tm, D_MODEL), lambda i, j: (i, 0)),
                     pl.BlockSpec((nseq, tf), lambda i, j: (0, j)),
                     pl.BlockSpec((nseq, tf), lambda i, j: (0, j)),
                     one(fw["g_pre"])]
        pre_args = (x1, conv_prev[:, 0, :], conv_prev[:, 1, :], fw["g_pre"])
        tail_shape = (nseq, SUBLANES, D_FF)
        tail_spec = pl.BlockSpec((nseq, SUBLANES, tf), lambda i, j: (0, 0, j))
        hrows = tm
    return pl.pallas_call(
        functools.partial(_ffn_kernel, seq_len=seq_len),
        grid=(nm, nf),
        in_specs=pre_specs + wspecs,
        out_specs=[pl.BlockSpec((tm, D_MODEL), lambda i, j: (i, 0), pipeline_mode=pl.Buffered(1)), tail_spec],
        out_shape=[jax.ShapeDtypeStruct((m, D_MODEL), F32),
                   jax.ShapeDtypeStruct(tail_shape, F32)],
        scratch_shapes=[pltpu.VMEM((hrows, D_MODEL), BF16)],
        compiler_params=_cparams(("parallel", "arbitrary")),
        name="conv_ffn",
    )(*pre_args, *wargs)


def _pack_lora(x, axis=1):
    cut = lambda a, b: lax.slice_in_dim(x, a, b, axis=axis)

    def z(n):
        shape = list(x.shape)
        shape[axis] = n
        return jnp.zeros(shape, x.dtype)

    xw = cut(0, DECAY_LORA)
    xa = cut(DECAY_LORA, DECAY_LORA + ICLR_LORA)
    xg = cut(DECAY_LORA + ICLR_LORA, DECAY_LORA + ICLR_LORA + GATE_LORA)
    return jnp.concatenate([xw, z(LORA_XA - DECAY_LORA), xa, z(LORA_XG - LORA_XA - ICLR_LORA),
                            xg, z(LORA_W - LORA_XG - GATE_LORA)], axis=axis)


def _unpack_lora(x):
    return jnp.concatenate([x[:, LORA_XW:LORA_XW + DECAY_LORA], x[:, LORA_XA:LORA_XA + ICLR_LORA],
                            x[:, LORA_XG:LORA_XG + GATE_LORA]], axis=1)


def _pack_in_proj_t(wt):
    c = 3 * RWKV_DIM
    q0 = RWKV_PROJ
    kv0 = q0 + Q_DIM
    g0 = kv0 + 2 * KV_DIM
    pieces = [wt[:2 * RWKV_DIM], _to_scan_order(wt[2 * RWKV_DIM:c], 0), wt[q0:kv0],
              wt[g0:g0 + 2 * D_MODEL], _pack_lora(wt[c:RWKV_PROJ], axis=0), wt[kv0:g0]]
    return jnp.concatenate([p.astype(BF16) for p in pieces], axis=0)


def _pack_rwkv_row(x):
    c = 3 * RWKV_DIM
    return jnp.concatenate([x[:, :2 * RWKV_DIM], _to_scan_order(x[:, 2 * RWKV_DIM:c], 1), _pack_lora(x[:, c:])],
                           axis=1)


def _unpack_rwkv_row(p_row):
    return jnp.concatenate([p_row[:, :2 * RWKV_DIM], _to_natural_order(p_row[:, COL_V:COL_V + RWKV_DIM], 1),
                            _unpack_lora(p_row[:, COL_LORA:COL_LORA + LORA_W])], axis=1)


def _pick_tile(m, candidates):
    for c in candidates:
        if m % c == 0:
            return c
    raise ValueError(f"no row tile in {candidates} divides {m}")


def _pad_rows(w, rows):
    return jnp.concatenate([w, jnp.zeros((rows - w.shape[0], w.shape[1]), w.dtype)], axis=0)


def _trunk(x, prefix, lw, *, tm, tm_mix, tb_prep, shift_rows, nseq, cos, sin, scan_attend, ffn_tm,
           ffn_seq_len, conv_prev):
    rows = x.shape[0] + (0 if prefix is None else prefix.shape[0])
    seq_rows = rows // nseq
    p_all = _norm_matmul(x, prefix, lw["g_pre_mix"], lw["w_in_t"], tm, 1792)
    ab, kr, cum, v_scan, g, bonus, q_roped, k_roped, v_att = _rwkv_prep(
        p_all, shift_rows, lw, cos, sin, tb_prep, seq_rows // tb_prep)
    y, s_fin, att = scan_attend((ab, kr, cum, v_scan), q_roped, k_roped, v_att)

    mixed = _mix(y, bonus, g, att, p_all, lw["ln_g"], lw["ln_b"], lw["hsum_scan"],
                 lw["w_branch_a"], lw["w_branch_b"], tm_mix)
    x1 = _out_norm(mixed, lw["w_out"], x, prefix, lw["g_post_mix"], tm_mix)
    x2, tail = _ffn(x1, conv_prev, lw["ffn"], ffn_tm, 512, ffn_seq_len)
    return x2, p_all, k_roped, v_att, s_fin, tail


def kernel(x_prompt, x_sample, cache_meta_k, cache_meta_v, cache_win_k, cache_win_v, state_rwkv, state_shift, state_ffn_conv, meta_tokens, g_pre_mix, w_in, rwkv_mu, rwkv_w0, rwkv_w_lora_b, rwkv_a0, rwkv_a_lora_b, rwkv_g_lora_b, rwkv_k_k, rwkv_k_a, rwkv_r_k, rwkv_ln_g, rwkv_ln_b, attn_sinks, w_branch_a, w_branch_b, w_out, g_post_mix, g_pre_ffn, w_up, ffn_conv_w, ffn_conv_b, w_down, g_post_ffn):
    depth = w_in.shape[0]
    assert depth == 1
    l = 0
    seq = x_prompt.shape[1]
    nb, tn = x_sample.shape[0], x_sample.shape[1]
    row = lambda a: a.reshape(1, -1)
    lane = np.arange(LANES)
    chan = np.arange(RWKV_DIM)
    ones_bd = jnp.asarray(lane[:, None] // HEAD_DIM == lane[None, :] // HEAD_DIM, BF16)
    hsum_nat = jnp.asarray((chan // HEAD_DIM)[:, None] == (lane // V_LO)[None, :], BF16)
    hsum_scan = jnp.asarray((lane // V_LO)[:, None] == (lane // V_LO)[None, :], BF16)
    lw = dict(
        g_pre_mix=row(g_pre_mix[l]),
        w_in_t=_pack_in_proj_t(jnp.swapaxes(w_in[l], 0, 1)),
        mu=_pack_rwkv_row(row(rwkv_mu[l])),
        w0=row(rwkv_w0[l]), a0=row(rwkv_a0[l]), k_k=row(rwkv_k_k[l]), k_a=row(rwkv_k_a[l]),
        r_k=row(rwkv_r_k[l]),
        w_lora=_pad_rows(rwkv_w_lora_b[l], LANES).astype(BF16),
        a_lora=_pad_rows(rwkv_a_lora_b[l], LANES).astype(BF16),
        g_lora=_pad_rows(_to_scan_order(rwkv_g_lora_b[l], 1), 2 * LANES).astype(BF16),
        ones_bd=ones_bd, hsum_nat=hsum_nat, hsum_scan=hsum_scan,
        ln_g=row(_to_scan_order(rwkv_ln_g[l], 0)), ln_b=row(_to_scan_order(rwkv_ln_b[l], 0)),
        w_branch_a=_to_scan_order(w_branch_a[l], 0).astype(BF16), w_branch_b=w_branch_b[l].astype(BF16),
        w_out=w_out[l].astype(BF16), g_post_mix=row(g_post_mix[l]),
        ffn=dict(g_pre=row(g_pre_ffn[l]), w_up=w_up[l].astype(BF16), w_down=w_down[l].astype(BF16),
                 conv_w=ffn_conv_w[l], conv_b=row(ffn_conv_b[l]), g_post=row(g_post_ffn[l])),
    )
    sinks = row(attn_sinks[l])

    mp = N_META + seq
    meta_rows = meta_tokens.astype(x_prompt.dtype)
    nchunk = seq // CHUNK
    tm_p = _pick_tile(mp, (912, 144, 48))
    tm_mix_p = _pick_tile(mp, (432, 144, 48))
    tb_prep_p = _pick_tile(mp, (144, 48))
    cos_p, sin_p = _rope_tables(jnp.arange(mp, dtype=jnp.int32))

    def scan_attend_prompt(scan_ops, q, k, v):
        scan_tb = _pick_tile(mp, (432, 144, 48))
        y, s_fin = _rwkv_scan(*scan_ops, jnp.zeros((1,) + SCAN_STATE, F32), scan_tb, mp // scan_tb)
        meta = lambda a: (a, (N_META, KV_DIM), lambda c: (0, 0))
        o_m = _attention(q, sinks, [meta(k)], [meta(v)], N_META, 1, False)
        chunk_block = (pl.Element(CHUNK), pl.Element(KV_DIM))
        chunk_at = lambda d: lambda c: (pl.multiple_of(N_META + jnp.maximum(c - d, 0) * CHUNK, SUBLANES), 0)
        band = lambda a: [(a, chunk_block, chunk_at(d)) for d in (2, 1, 0)]
        o_f = _attention(q, sinks, [meta(k)] + band(k), [meta(v)] + band(v), CHUNK, nchunk, True,
                         q_row0=N_META)
        return y, s_fin, jnp.concatenate([o_m, o_f], axis=0)

    assert N_META == HALO
    yp, p_p, k_p, v_p, sfin_p, tail_p = _trunk(
        x_prompt[0], meta_rows, lw, tm=tm_p, tm_mix=tm_mix_p, tb_prep=tb_prep_p,
        shift_rows=jnp.zeros((1, 1, RWKV_PACK), F32),
        nseq=1, cos=cos_p, sin=sin_p, scan_attend=scan_attend_prompt,
        ffn_tm=_pick_tile(seq, (1024, 64)), ffn_seq_len=None, conv_prev=None)

    ms = nb * tn
    xs = x_sample.reshape(ms, D_MODEL)
    pos_s = N_META + PAST_LEN + jnp.arange(tn, dtype=jnp.int32)
    cos_s, sin_s = _rope_tables(jnp.tile(pos_s, nb))
    nwin = cache_win_k.shape[2]
    cmk = cache_meta_k[l].reshape(nb, N_META, KV_DIM)
    cmv = cache_meta_v[l].reshape(nb, N_META, KV_DIM)
    cwk = cache_win_k[l].reshape(nb, nwin, KV_DIM)
    cwv = cache_win_v[l].reshape(nb, nwin, KV_DIM)

    shift_s = _pack_rwkv_row(state_shift[l].reshape(nb, RWKV_PROJ))[:, None, :]

    def scan_attend_sample(scan_ops, q, k, v):
        y, s_fin = _rwkv_scan(*scan_ops, _state_to_scan(state_rwkv[l]), tn, 1)
        per_b = lambda a, n: (a, (None, n, KV_DIM), lambda b: (b, 0, 0))
        own = lambda a: (a, (tn, KV_DIM), lambda b: (b, 0))
        att = _attention(q, sinks, [per_b(cmk, N_META), per_b(cwk, nwin), own(k)],
                         [per_b(cmv, N_META), per_b(cwv, nwin), own(v)], tn, nb, False)
        return y, s_fin, att

    xs2, p_s, k_s, v_s, sfin_s, tail_s = _trunk(
        xs, None, lw, tm=ms, tm_mix=ms, tb_prep=tn, shift_rows=shift_s, nseq=nb, cos=cos_s, sin=sin_s,
        scan_attend=scan_attend_sample, ffn_tm=ms, ffn_seq_len=tn, conv_prev=state_ffn_conv[l])

    kv5 = lambda a, n: a.reshape(1, -1, n, KV_HEADS, HEAD_DIM)
    y_prompt = yp[None]
    y_sample = xs2.reshape(nb, tn, D_MODEL)
    prompt_meta_k = kv5(k_p[:N_META], N_META)
    prompt_meta_v = kv5(v_p[:N_META], N_META)
    prompt_win_k = kv5(k_p[mp - WINDOW:], WINDOW)
    prompt_win_v = kv5(v_p[mp - WINDOW:], WINDOW)
    prompt_rwkv = _state_from_scan(sfin_p)[None]
    prompt_shift = _unpack_rwkv_row(p_p[mp - 1:mp]).reshape(1, 1, 1, RWKV_PROJ)
    prompt_ffn_conv = tail_p[-1, SUBLANES - (FFN_CONV - 1):].reshape(1, 1, FFN_CONV - 1, D_FF)
    sample_win_k = kv5(k_s, tn)
    sample_win_v = kv5(v_s, tn)
    sample_rwkv = _state_from_scan(sfin_s)[None]
    sample_shift = _unpack_rwkv_row(p_s[tn - 1::tn]).reshape(1, nb, 1, RWKV_PROJ)
    sample_ffn_conv = tail_s[:, SUBLANES - (FFN_CONV - 1):].reshape(1, nb, FFN_CONV - 1, D_FF)
    return (y_prompt, y_sample, prompt_meta_k, prompt_meta_v, prompt_win_k, prompt_win_v,
            prompt_rwkv, prompt_shift, prompt_ffn_conv, sample_win_k, sample_win_v,
            sample_rwkv, sample_shift, sample_ffn_conv)
```
